```python
import math
import jax, jax.numpy as jnp
from jax import lax
import numpy as np

D_MODEL = 2048
BATCH = 2
SEQ = 8192
DEPTH = 1

CTX_LEN = 256
GRID_W = 64
CONV_WIDTH = 1024
CONV_K = 3
MLA_HEADS = 8
QK_NOPE = 128
QK_ROPE = 64
V_DIM = 128
Q_LORA = 512
KV_LORA = 512
MLA_WIDTH = MLA_HEADS * V_DIM
MIX_WIDTH = CONV_WIDTH + MLA_WIDTH
IN_COLS = 3 * CONV_WIDTH + Q_LORA + KV_LORA + QK_ROPE
N_EXPERTS = 16
EXPERT_FF = 1408
EC_FACTOR = 2
Q_BLOCK = 128
ROPE_THETA = 10000.0
ROPE_PAIRS = QK_ROPE // 4
ATTN_SCALE = 1.0 / math.sqrt(QK_NOPE + QK_ROPE)
EPS = 1e-6

kernel_name = "hybrid_conv_mla_ec_moe_dit"


def rmsnorm(h, g):
    h32 = h.astype(jnp.float32)
    h32 = h32 * lax.rsqrt(jnp.mean(h32 * h32, axis=-1, keepdims=True) + EPS)
    return h32.astype(h.dtype) * g


def modulate(h, shift, scale):
    return h * (1 + scale) + shift


def rotate_pairs(xa, ang):
    x1, x2 = jnp.split(xa, 2, axis=-1)
    cos, sin = jnp.cos(ang), jnp.sin(ang)
    return jnp.concatenate([x1 * cos - x2 * sin, x1 * sin + x2 * cos], axis=-1)


def rope2d(x, ang_row, ang_col):
    x32 = x.astype(jnp.float32)
    half = QK_ROPE // 2
    out = jnp.concatenate([rotate_pairs(x32[..., :half], ang_row),
                           rotate_pairs(x32[..., half:], ang_col)], axis=-1)
    return out.astype(x.dtype)


def dwconv3(u, w):
    up = jnp.pad(u, ((0, 0), (1, 1), (0, 0)))
    return up[:, :-2] * w[0] + up[:, 1:-1] * w[1] + up[:, 2:] * w[2]


def short_conv_mixer(p, w):
    xin = p[..., :CONV_WIDTH]
    bg = p[..., CONV_WIDTH:2 * CONV_WIDTH]
    cg = p[..., 2 * CONV_WIDTH:3 * CONV_WIDTH]
    return bg * dwconv3(cg * xin, w)


def mla_latents(p):
    o = 3 * CONV_WIDTH
    cq = p[..., o:o + Q_LORA]
    ckv = p[..., o + Q_LORA:o + Q_LORA + KV_LORA]
    kr = p[..., o + Q_LORA + KV_LORA:]
    return cq, ckv, kr


def mla_queries(cq, q_g, w_uq):
    q = jnp.einsum('bsr,rhd->bshd', rmsnorm(cq, q_g), w_uq)
    return q[..., :QK_NOPE], q[..., QK_NOPE:]


def mla_kv(ckv, kv_g, w_ukv):
    kv = jnp.einsum('bsr,rhd->bshd', rmsnorm(ckv, kv_g), w_ukv)
    return kv[..., :QK_NOPE], kv[..., QK_NOPE:]


def mla_attend(qn, qr, kn, kr, v):
    s = jnp.einsum('bqhd,bkhd->bhqk', qn, kn) + jnp.einsum('bqhr,bkr->bhqk', qr, kr)
    p = jax.nn.softmax(s.astype(jnp.float32) * ATTN_SCALE, axis=-1).astype(v.dtype)
    return jnp.einsum('bhqk,bkhd->bqhd', p, v)


def blocked_attention(qn, qr, kn, kr, v):
    b, s, h, _ = qn.shape
    nblk = s // Q_BLOCK
    qn_b = qn.reshape(b, nblk, Q_BLOCK, h, QK_NOPE).transpose(1, 0, 2, 3, 4)
    qr_b = qr.reshape(b, nblk, Q_BLOCK, h, QK_ROPE).transpose(1, 0, 2, 3, 4)
    o = lax.map(lambda a: mla_attend(a[0], a[1], kn, kr, v), (qn_b, qr_b))
    return o.transpose(1, 0, 2, 3, 4).reshape(b, s, h * V_DIM)


def expert_choice_ffn(h, w_router, w_gate, w_up, w_down):
    b, n, d = h.shape
    cap = EC_FACTOR * n // N_EXPERTS
    logits = jnp.einsum('bnd,de->bne', h, w_router).astype(jnp.float32)
    aff = jax.nn.softmax(logits, axis=-1)
    g, idx = lax.top_k(jnp.swapaxes(aff, 1, 2), cap)
    idx_flat = idx.reshape(b, N_EXPERTS * cap)
    xs = jax.vmap(lambda hb, ib: hb[ib])(h, idx_flat).reshape(b, N_EXPERTS, cap, d)
    hid = jax.nn.silu(jnp.einsum('becd,edf->becf', xs, w_gate)) * jnp.einsum('becd,edf->becf', xs, w_up)
    y = jnp.einsum('becf,efd->becd', hid, w_down) * g[..., None].astype(h.dtype)
    y = y.reshape(b, N_EXPERTS * cap, d)
    return jax.vmap(lambda ib, yb: jnp.zeros((n, d), h.dtype).at[ib].add(yb))(idx_flat, y)


def setup_inputs(seed: int = 0) -> dict:
    key = jax.random.key(seed)
    ks = jax.random.split(key, 24)
    f32 = jnp.float32
    nrm = lambda k, shape, s: jax.random.normal(k, shape, f32) * s
    return {
        "x": nrm(ks[0], (BATCH, SEQ, D_MODEL), 1.0),
        "c": nrm(ks[1], (BATCH, D_MODEL), 1.0),
        "ctx": nrm(ks[2], (BATCH, CTX_LEN, D_MODEL), 1.0),
        "c_ctx": nrm(ks[3], (D_MODEL,), 1.0),
        "w_mod": nrm(ks[4], (DEPTH, D_MODEL, 6 * D_MODEL), 0.5 * D_MODEL ** -0.5),
        "b_mod": nrm(ks[5], (DEPTH, 6 * D_MODEL), 0.01),
        "norm1_g": 1.0 + nrm(ks[6], (DEPTH, D_MODEL), 0.02),
        "norm2_g": 1.0 + nrm(ks[7], (DEPTH, D_MODEL), 0.02),
        "w_in": nrm(ks[8], (DEPTH, D_MODEL, IN_COLS), D_MODEL ** -0.5),
        "conv_w": nrm(ks[9], (DEPTH, CONV_K, CONV_WIDTH), CONV_K ** -0.5),
        "q_norm_g": 1.0 + nrm(ks[10], (DEPTH, Q_LORA), 0.02),
        "w_uq": nrm(ks[11], (DEPTH, Q_LORA, MLA_HEADS, QK_NOPE + QK_ROPE), Q_LORA ** -0.5),
        "kv_norm_g": 1.0 + nrm(ks[12], (DEPTH, KV_LORA), 0.02),
        "w_ukv": nrm(ks[13], (DEPTH, KV_LORA, MLA_HEADS, QK_NOPE + V_DIM), KV_LORA ** -0.5),
        "w_out": nrm(ks[14], (DEPTH, MIX_WIDTH, D_MODEL), MIX_WIDTH ** -0.5),
        "w_router": nrm(ks[15], (DEPTH, D_MODEL, N_EXPERTS), D_MODEL ** -0.5),
        "w_gate": nrm(ks[16], (DEPTH, N_EXPERTS, D_MODEL, EXPERT_FF), D_MODEL ** -0.5),
        "w_up": nrm(ks[17], (DEPTH, N_EXPERTS, D_MODEL, EXPERT_FF), D_MODEL ** -0.5),
        "w_down": nrm(ks[18], (DEPTH, N_EXPERTS, EXPERT_FF, D_MODEL), EXPERT_FF ** -0.5),
        "final_g": 1.0 + nrm(ks[19], (D_MODEL,), 0.02),
    }


def reference(x, c, ctx, c_ctx, w_mod, b_mod, norm1_g, norm2_g, w_in, conv_w, q_norm_g, w_uq,
              kv_norm_g, w_ukv, w_out, w_router, w_gate, w_up, w_down, final_g):
    s = x.shape[1]
    n_rows = s // GRID_W
    row = jnp.repeat(jnp.arange(n_rows), GRID_W).astype(jnp.float32)
    col = jnp.tile(jnp.arange(GRID_W), n_rows).astype(jnp.float32)
    inv_freq = ROPE_THETA ** (-jnp.arange(ROPE_PAIRS, dtype=jnp.float32) / ROPE_PAIRS)
    ang_r = row[:, None] * inv_freq[None, :]
    ang_c = col[:, None] * inv_freq[None, :]

    for l in range(DEPTH):
        mod = jax.nn.silu(c) @ w_mod[l] + b_mod[l]
        mod_c = jax.nn.silu(c_ctx) @ w_mod[l] + b_mod[l]
        sh1, sc1, g1, sh2, sc2, g2 = jnp.split(mod[:, None, :], 6, axis=-1)
        shc1, scc1, gc1, shc2, scc2, gc2 = jnp.split(mod_c, 6, axis=-1)

        h = modulate(rmsnorm(x, norm1_g[l]), sh1, sc1)
        hc = modulate(rmsnorm(ctx, norm1_g[l]), shc1, scc1)
        p = h @ w_in[l]
        pc = hc @ w_in[l]

        y_conv = short_conv_mixer(p, conv_w[l])

        cq, ckv, kr = mla_latents(p)
        cq_c, ckv_c, kr_c = mla_latents(pc)
        q_nope, q_rope = mla_queries(cq, q_norm_g[l], w_uq[l])
        q_rope = rope2d(q_rope, ang_r[:, None, :], ang_c[:, None, :])
        k_nope, v = mla_kv(ckv, kv_norm_g[l], w_ukv[l])
        k_rope = rope2d(kr, ang_r, ang_c)
        kc_nope, vc = mla_kv(ckv_c, kv_norm_g[l], w_ukv[l])
        kn_all = jnp.concatenate([k_nope, kc_nope], axis=1)
        kr_all = jnp.concatenate([k_rope, kr_c], axis=1)
        v_all = jnp.concatenate([v, vc], axis=1)
        y_attn = blocked_attention(q_nope, q_rope, kn_all, kr_all, v_all)

        y = jnp.concatenate([y_conv, y_attn], axis=-1) @ w_out[l]
        x = x + g1 * y

        if l < DEPTH - 1:
            yc_conv = short_conv_mixer(pc, conv_w[l])
            qc_nope, qc_rope = mla_queries(cq_c, q_norm_g[l], w_uq[l])
            b_, lc = ctx.shape[0], ctx.shape[1]
            yc_attn = mla_attend(qc_nope, qc_rope, kc_nope, kr_c, vc).reshape(b_, lc, MLA_WIDTH)
            ctx = ctx + gc1 * (jnp.concatenate([yc_conv, yc_attn], axis=-1) @ w_out[l])
            hc2 = modulate(rmsnorm(ctx, norm2_g[l]), shc2, scc2)
            ctx = ctx + gc2 * expert_choice_ffn(hc2, w_router[l], w_gate[l], w_up[l], w_down[l])

        h2 = modulate(rmsnorm(x, norm2_g[l]), sh2, sc2)
        x = x + g2 * expert_choice_ffn(h2, w_router[l], w_gate[l], w_up[l], w_down[l])

    return rmsnorm(x, final_g)
```

```python
import functools
import math

import jax
import jax.numpy as jnp
import numpy as np
from jax import lax
from jax.experimental import pallas as pl
from jax.experimental.pallas import tpu as pltpu

F32 = jnp.float32
BF16 = jnp.bfloat16
I32 = jnp.int32

GRID_W = 64
CONV_WIDTH = 1024
MLA_HEADS = 8
QK_NOPE = 128
QK_ROPE = 64
V_DIM = 128
Q_LORA = 512
KV_LORA = 512
QK_DIM = QK_NOPE + QK_ROPE
MLA_WIDTH = MLA_HEADS * V_DIM
N_EXPERTS = 16
EC_FACTOR = 2
ROPE_THETA = 10000.0
ROPE_PAIRS = QK_ROPE // 4
ATTN_SCALE = 1.0 / math.sqrt(QK_DIM)
LOG2E = math.log2(math.e)
EPS = 1e-6

LANES = 128
VMEM_LIMIT = 56 * 1024 * 1024

MOD_TN = 1024
ROW_TM = 512
ATT_TQ = 512
ATT_TK = 512
FFN_T = 256
CMB_TT = 256
CMB_W = 64
IDX_PC = 64


def _dot(a, b):
    return jnp.dot(a, b, preferred_element_type=F32)


def _params(sem, vmem=VMEM_LIMIT):
    return pltpu.CompilerParams(dimension_semantics=sem, vmem_limit_bytes=vmem)


def _resident(shape):
    nd = len(shape)
    return pl.BlockSpec(shape, lambda *_: (0,) * nd, pipeline_mode=pl.Buffered(1))


def _mod_kernel(c_ref, w_ref, b_ref, o_ref):
    cv = c_ref[...]
    a = cv / (1.0 + jnp.exp(-cv))
    a_hi = a.astype(BF16)
    a_lo = (a - a_hi.astype(F32)).astype(BF16)
    w = w_ref[...]
    w_hi = w.astype(BF16)
    w_lo = (w - w_hi.astype(F32)).astype(BF16)
    o_ref[...] = _dot(a_hi, w_hi) + _dot(a_lo, w_hi) + _dot(a_hi, w_lo) + b_ref[...]


def _modulation(cvec, w_mod, b_mod):
    d, n = w_mod.shape
    return pl.pallas_call(
        _mod_kernel,
        grid=(n // MOD_TN,),
        in_specs=[pl.BlockSpec((8, d), lambda j: (0, 0)),
                  pl.BlockSpec((d, MOD_TN), lambda j: (0, j)),
                  pl.BlockSpec((1, MOD_TN), lambda j: (0, j))],
        out_specs=pl.BlockSpec((8, MOD_TN), lambda j: (0, j)),
        out_shape=jax.ShapeDtypeStruct((8, n), F32),
        compiler_params=_params(("arbitrary",)),
        name="modulation",
    )(cvec, w_mod, b_mod)


def _rms(v):
    return v * lax.rsqrt(jnp.mean(v * v, axis=-1, keepdims=True) + EPS)


def _inproj_kernel(x_ref, sc_ref, sh_ref, g_ref, w_ref, wkr_ref, qg_ref, kvg_ref, *outs, with_conv):
    h = _rms(x_ref[0]) * g_ref[...]
    h = h * (1.0 + sc_ref[0]) + sh_ref[0]
    hb = h.astype(BF16)
    c = CONV_WIDTH
    if with_conv:
        u_ref, bg_ref, cq_ref, ckv_ref, kr_ref = outs
        xin = _dot(hb, w_ref[:, 0:c])
        cg = _dot(hb, w_ref[:, 2 * c:3 * c])
        u_ref[0] = (cg * xin).astype(BF16)
        bg_ref[0] = _dot(hb, w_ref[:, c:2 * c]).astype(BF16)
    else:
        cq_ref, ckv_ref, kr_ref = outs
    o = 3 * c
    cq = _dot(hb, w_ref[:, o:o + Q_LORA])
    cq_ref[0] = (_rms(cq) * qg_ref[...]).astype(BF16)
    ckv = _dot(hb, w_ref[:, o + Q_LORA:o + Q_LORA + KV_LORA])
    ckv_ref[0] = (_rms(ckv) * kvg_ref[...]).astype(BF16)
    kr_ref[0] = _dot(hb, wkr_ref[...])


def _inproj(x, scale, shift, gain, w_main, w_kr2, q_g, kv_g, *, with_conv, tm):
    b, s, d = x.shape
    per_batch = scale.shape[0] > 1
    vec = pl.BlockSpec((1, 1, d), (lambda bi, i: (bi, 0, 0)) if per_batch else (lambda bi, i: (0, 0, 0)))
    row = lambda n: pl.BlockSpec((1, tm, n), lambda bi, i: (bi, i, 0))
    outs, specs = [], []
    if with_conv:
        outs += [jax.ShapeDtypeStruct((b, s, CONV_WIDTH), BF16)] * 2
        specs += [row(CONV_WIDTH)] * 2
    outs += [jax.ShapeDtypeStruct((b, s, Q_LORA), BF16), jax.ShapeDtypeStruct((b, s, KV_LORA), BF16),
             jax.ShapeDtypeStruct((b, s, 2 * QK_ROPE), F32)]
    specs += [row(Q_LORA), row(KV_LORA), row(2 * QK_ROPE)]
    return pl.pallas_call(
        functools.partial(_inproj_kernel, with_conv=with_conv),
        grid=(b, s // tm),
        in_specs=[row(d), vec, vec, _resident((1, d)), _resident(w_main.shape), _resident(w_kr2.shape),
                  _resident((1, Q_LORA)), _resident((1, KV_LORA))],
        out_specs=specs,
        out_shape=outs,
        compiler_params=_params(("arbitrary", "arbitrary")),
        name="inproj_conv" if with_conv else "inproj_ctx",
    )(x, scale, shift, gain, w_main, w_kr2, q_g, kv_g)


def _qproj_kernel(cq_ref, cos_ref, sin_ref, wn_ref, wr_ref, wrs_ref, q_ref):
    cq = cq_ref[0]
    qn = _dot(cq, wn_ref[...])
    qr = _dot(cq, wr_ref[...])
    qrs = _dot(cq, wrs_ref[...])
    reps = MLA_HEADS * QK_ROPE // LANES
    cos = jnp.concatenate([cos_ref[...]] * reps, axis=-1)
    sin = jnp.concatenate([sin_ref[...]] * reps, axis=-1)
    qrot = qr * cos + qrs * sin
    sc = ATTN_SCALE * LOG2E
    for h in range(MLA_HEADS):
        q_ref[0, h, :, 0:QK_NOPE] = (qn[:, h * QK_NOPE:(h + 1) * QK_NOPE] * sc).astype(BF16)
        q_ref[0, h, :, QK_NOPE:QK_DIM] = (qrot[:, h * QK_ROPE:(h + 1) * QK_ROPE] * sc).astype(BF16)


def _qproj(cqn, cos2, sin2, w_qn, w_qr, w_qrs, *, tm):
    b, s, r = cqn.shape
    return pl.pallas_call(
        _qproj_kernel,
        grid=(b, s // tm),
        in_specs=[pl.BlockSpec((1, tm, r), lambda bi, i: (bi, i, 0)),
                  pl.BlockSpec((tm, LANES), lambda bi, i: (i, 0)),
                  pl.BlockSpec((tm, LANES), lambda bi, i: (i, 0)),
                  _resident(w_qn.shape), _resident(w_qr.shape), _resident(w_qrs.shape)],
        out_specs=pl.BlockSpec((1, MLA_HEADS, tm, QK_DIM), lambda bi, i: (bi, 0, i, 0)),
        out_shape=jax.ShapeDtypeStruct((b, MLA_HEADS, s, QK_DIM), BF16),
        compiler_params=_params(("arbitrary", "arbitrary")),
        name="qproj",
    )(cqn, cos2, sin2, w_qn, w_qr, w_qrs)


def _kvproj_kernel(ckv_ref, kr_ref, *rest, rope):
    if rope:
        tab_ref, wk_ref, wv_ref, k_ref, v_ref = rest
        prod = kr_ref[0] * tab_ref[...]
        krot = prod[:, 0:QK_ROPE] + prod[:, QK_ROPE:2 * QK_ROPE]
    else:
        wk_ref, wv_ref, k_ref, v_ref = rest
        krot = kr_ref[0][:, 0:QK_ROPE]
    ckv = ckv_ref[0]
    kn = _dot(ckv, wk_ref[...])
    vv = _dot(ckv, wv_ref[...])
    krot = krot.astype(BF16)
    for h in range(MLA_HEADS):
        k_ref[0, h, :, 0:QK_NOPE] = kn[:, h * QK_NOPE:(h + 1) * QK_NOPE].astype(BF16)
        k_ref[0, h, :, QK_NOPE:QK_DIM] = krot
        v_ref[0, h] = vv[:, h * V_DIM:(h + 1) * V_DIM].astype(BF16)


def _kvproj(ckvn, kr2, tab, w_kn, w_v, *, tm):
    b, s, r = ckvn.shape
    rope = tab is not None
    ins = [ckvn, kr2]
    specs = [pl.BlockSpec((1, tm, r), lambda bi, i: (bi, i, 0)),
             pl.BlockSpec((1, tm, 2 * QK_ROPE), lambda bi, i: (bi, i, 0))]
    if rope:
        ins.append(tab)
        specs.append(pl.BlockSpec((tm, 2 * QK_ROPE), lambda bi, i: (i, 0)))
    ins += [w_kn, w_v]
    specs += [_resident(w_kn.shape), _resident(w_v.shape)]
    return pl.pallas_call(
        functools.partial(_kvproj_kernel, rope=rope),
        grid=(b, s // tm),
        in_specs=specs,
        out_specs=[pl.BlockSpec((1, MLA_HEADS, tm, QK_DIM), lambda bi, i: (bi, 0, i, 0)),
                   pl.BlockSpec((1, MLA_HEADS, tm, V_DIM), lambda bi, i: (bi, 0, i, 0))],
        out_shape=[jax.ShapeDtypeStruct((b, MLA_HEADS, s, QK_DIM), BF16),
                   jax.ShapeDtypeStruct((b, MLA_HEADS, s, V_DIM), BF16)],
        compiler_params=_params(("arbitrary", "arbitrary")),
        name="kvproj_rope" if rope else "kvproj_ctx",
    )(*ins)


def _attn_kernel(q_ref, k_ref, v_ref, kc_ref, vc_ref, o_ref, m_ref, l_ref, acc_ref, *, tk):
    q = q_ref[0, 0]
    m_ref[...] = jnp.full(m_ref.shape, -jnp.inf, F32)
    l_ref[...] = jnp.zeros(l_ref.shape, F32)
    acc_ref[...] = jnp.zeros(acc_ref.shape, F32)

    def update(k, v):
        s = lax.dot_general(q, k, (((1,), (1,)), ((), ())), preferred_element_type=F32)
        m_old = m_ref[...]
        m_new = jnp.maximum(m_old, jnp.max(s, axis=-1, keepdims=True))
        alpha = jnp.exp2(m_old - m_new)
        p = jnp.exp2(s - m_new)
        l_ref[...] = alpha * l_ref[...] + jnp.sum(p, axis=-1, keepdims=True)
        acc_ref[...] = alpha * acc_ref[...] + _dot(p.astype(BF16), v)
        m_ref[...] = m_new

    def body(j, carry):
        off = pl.multiple_of(j * tk, tk)
        update(k_ref[0, 0, pl.ds(off, tk), :], v_ref[0, 0, pl.ds(off, tk), :])
        return carry

    lax.fori_loop(0, k_ref.shape[2] // tk, body, 0)
    update(kc_ref[0, 0], vc_ref[0, 0])
    o_ref[0] = (acc_ref[...] / l_ref[...]).astype(BF16)


def _attention(q, k, v, kc, vc, *, tq, tk):
    b, h, s, dq = q.shape
    lc = kc.shape[2]
    return pl.pallas_call(
        functools.partial(_attn_kernel, tk=tk),
        grid=(b, h, s // tq),
        in_specs=[pl.BlockSpec((1, 1, tq, dq), lambda bi, hi, i: (bi, hi, i, 0)),
                  pl.BlockSpec((1, 1, s, dq), lambda bi, hi, i: (bi, hi, 0, 0)),
                  pl.BlockSpec((1, 1, s, V_DIM), lambda bi, hi, i: (bi, hi, 0, 0)),
                  pl.BlockSpec((1, 1, lc, dq), lambda bi, hi, i: (bi, hi, 0, 0)),
                  pl.BlockSpec((1, 1, lc, V_DIM), lambda bi, hi, i: (bi, hi, 0, 0))],
        out_specs=pl.BlockSpec((1, tq, V_DIM), lambda bi, hi, i: (bi, i, hi)),
        out_shape=jax.ShapeDtypeStruct((b, s, h * V_DIM), BF16),
        scratch_shapes=[pltpu.VMEM((tq, 1), F32), pltpu.VMEM((tq, 1), F32), pltpu.VMEM((tq, V_DIM), F32)],
        compiler_params=_params(("arbitrary", "arbitrary", "arbitrary")),
        name="attention",
    )(q, k, v, kc, vc)


HALO = 16


def _outproj_kernel(u_ref, up_ref, un_ref, bg_ref, cw_ref, ya_ref, wo_ref, x_ref, g1_ref, n2_ref, sc_ref, sh_ref,
                    wrh_ref, wrl_ref, x1_ref, tab_ref, afft_ref, *, tm, d):
    i = pl.program_id(1)
    last = pl.num_programs(1) - 1
    u = u_ref[0].astype(F32)
    rows = lax.broadcasted_iota(I32, (tm, 1), 0)
    prev_row = jnp.where(i > 0, up_ref[0][HALO - 1:HALO, :].astype(F32), 0.0)
    next_row = jnp.where(i < last, un_ref[0][0:1, :].astype(F32), 0.0)
    um1 = jnp.where(rows == 0, prev_row, pltpu.roll(u, 1, 0))
    up1 = jnp.where(rows == tm - 1, next_row, pltpu.roll(u, tm - 1, 0))
    cw = cw_ref[...]
    yc = bg_ref[0].astype(F32) * (um1 * cw[0:1] + u * cw[1:2] + up1 * cw[2:3])
    y = _dot(yc.astype(BF16), wo_ref[0:CONV_WIDTH, :]) + _dot(ya_ref[0], wo_ref[CONV_WIDTH:, :])
    x1 = x_ref[0] + g1_ref[0] * y
    x1_ref[0] = x1
    h2 = _rms(x1) * n2_ref[...]
    h2 = h2 * (1.0 + sc_ref[0]) + sh_ref[0]
    tab_ref[0, :, 0:d] = h2
    h_hi = h2.astype(BF16)
    h_lo = (h2 - h_hi.astype(F32)).astype(BF16)
    logits = _dot(h_hi, wrh_ref[...]) + _dot(h_lo, wrh_ref[...]) + _dot(h_hi, wrl_ref[...])
    lane = lax.broadcasted_iota(I32, logits.shape, 1)
    logits = jnp.where(lane < N_EXPERTS, logits, -jnp.inf)
    e = jnp.exp(logits - jnp.max(logits, axis=-1, keepdims=True))
    aff = e / jnp.sum(e, axis=-1, keepdims=True)
    tab_ref[0, :, d:d + LANES] = aff
    afft_ref[0] = aff.T[0:N_EXPERTS, :]


def _outproj(u, bg, conv_w, y_attn, w_out, x, g1, n2, sc2, sh2, wr_hi, wr_lo, *, tm):
    b, s, d = x.shape
    nh = tm // HALO
    nhb = s // HALO
    row = lambda n: pl.BlockSpec((1, tm, n), lambda bi, i: (bi, i, 0))
    vec = pl.BlockSpec((1, 1, d), lambda bi, i: (bi, 0, 0))
    return pl.pallas_call(
        functools.partial(_outproj_kernel, tm=tm, d=d),
        grid=(b, s // tm),
        in_specs=[row(CONV_WIDTH),
                  pl.BlockSpec((1, HALO, CONV_WIDTH), lambda bi, i: (bi, jnp.maximum(i * nh - 1, 0), 0)),
                  pl.BlockSpec((1, HALO, CONV_WIDTH), lambda bi, i: (bi, jnp.minimum((i + 1) * nh, nhb - 1), 0)),
                  row(CONV_WIDTH), _resident(conv_w.shape), row(MLA_WIDTH), _resident(w_out.shape), row(d),
                  vec, _resident((1, d)), vec, vec, _resident(wr_hi.shape), _resident(wr_lo.shape)],
        out_specs=[row(d), row(d + LANES), pl.BlockSpec((1, N_EXPERTS, tm), lambda bi, i: (bi, 0, i))],
        out_shape=[jax.ShapeDtypeStruct((b, s, d), F32), jax.ShapeDtypeStruct((b, s, d + LANES), F32),
                   jax.ShapeDtypeStruct((b, N_EXPERTS, s), F32)],
        compiler_params=_params(("arbitrary", "arbitrary")),
        name="outproj_router",
    )(u, u, u, bg, conv_w, y_attn, w_out, x, g1, n2, sc2, sh2, wr_hi, wr_lo)


def _routing_kernel(aff_ref, tri_ref, pos_ref, offs_ref, idx_ref, *, cap, blk, rows_per_expert):
    bi = pl.program_id(0)
    aff = aff_ref[0]
    ne, s = aff.shape
    capf = jnp.float32(cap)

    def bisect(t, prefix):
        cand = prefix | jnp.left_shift(jnp.int32(1), 30 - t)
        cnt = jnp.sum(jnp.where(aff >= lax.bitcast_convert_type(cand, F32), 1.0, 0.0), axis=1, keepdims=True)
        return jnp.where(cnt >= capf, cand, prefix)

    floor_bits = lax.fori_loop(0, 31, bisect, jnp.zeros((ne, 1), I32))
    thr = jnp.min(jnp.where(aff >= lax.bitcast_convert_type(floor_bits, F32), aff, jnp.inf), axis=1, keepdims=True)
    gt = aff > thr
    eq = aff == thr
    need = capf - jnp.sum(jnp.where(gt, 1.0, 0.0), axis=1, keepdims=True)
    tri = tri_ref[...]

    def cumsum_blocks(mask_f32):
        run = jnp.zeros((ne, 1), F32)
        parts, starts = [], []
        for kb in range(s // blk):
            c = _dot(mask_f32[:, kb * blk:(kb + 1) * blk].astype(BF16), tri)
            starts.append(run)
            parts.append(c + run)
            run = run + c[:, blk - 1:blk]
        return jnp.concatenate(parts, axis=1), starts, run

    eqf = jnp.where(eq, 1.0, 0.0)
    eq_incl, _, _ = cumsum_blocks(eqf)
    sel = jnp.where(gt, 1.0, jnp.where(eq & (eq_incl - eqf < need), 1.0, 0.0))
    incl, starts, _ = cumsum_blocks(sel)

    base = (lax.broadcasted_iota(I32, (ne, 1), 0) * rows_per_expert + bi * cap)
    pos_ref[0] = jnp.where(sel > 0.0, (incl - sel).astype(I32) + base, -1)
    lane = lax.broadcasted_iota(I32, (ne, LANES), 1)
    offs = jnp.full((ne, LANES), cap, I32) + base
    for kb, st in enumerate(starts):
        offs = jnp.where(lane == kb, st.astype(I32) + base, offs)
    offs_ref[0] = offs

    tok_base = bi * s
    for e in range(ne):
        crow = incl[e:e + 1, :]

        def chunk(pc, carry):
            p0 = pc * IDX_PC
            pcol = (lax.broadcasted_iota(I32, (IDX_PC, 1), 0) + p0).astype(F32)
            cnt = jnp.sum(jnp.where(crow <= pcol, 1.0, 0.0), axis=1, keepdims=True)
            idx_ref[0, pl.ds(pl.multiple_of(p0, IDX_PC), IDX_PC), e:e + 1] = cnt.astype(I32) + tok_base
            return carry

        lax.fori_loop(0, cap // IDX_PC, chunk, 0)


def _routing(aff_t, *, cap, blk):
    b, ne, s = aff_t.shape
    assert s // blk + 1 <= LANES
    tri = jnp.asarray(np.triu(np.ones((blk, blk), np.float32)), BF16)
    return pl.pallas_call(
        functools.partial(_routing_kernel, cap=cap, blk=blk, rows_per_expert=b * cap),
        grid=(b,),
        in_specs=[pl.BlockSpec((1, ne, s), lambda bi: (bi, 0, 0)), _resident(tri.shape)],
        out_specs=[pl.BlockSpec((1, ne, s), lambda bi: (bi, 0, 0)),
                   pl.BlockSpec((1, ne, LANES), lambda bi: (bi, 0, 0)),
                   pl.BlockSpec((1, cap, ne), lambda bi: (bi, 0, 0))],
        out_shape=[jax.ShapeDtypeStruct((b, ne, s), I32), jax.ShapeDtypeStruct((b, ne, LANES), I32),
                   jax.ShapeDtypeStruct((b, cap, ne), I32)],
        compiler_params=_params(("arbitrary",)),
        name="routing",
    )(aff_t, tri)


def _ffn_kernel(idx_ref, tab_hbm, wg_ref, wu_ref, wd_ref, y_ref, buf, sem, *, t, d):
    e = pl.program_id(0)
    step = e * pl.num_programs(1) + pl.program_id(1)
    nsteps = pl.num_programs(0) * pl.num_programs(1)
    slot = step % 2

    def gather(st, sl):
        def issue(i, carry):
            r = idx_ref[st * t + i]
            pltpu.make_async_copy(tab_hbm.at[pl.ds(r, 1)], buf.at[sl, pl.ds(i, 1)], sem.at[sl]).start()
            return carry
        lax.fori_loop(0, t, issue, 0)

    @pl.when(step == 0)
    def _():
        gather(0, 0)

    @pl.when(step + 1 < nsteps)
    def _():
        gather(step + 1, 1 - slot)

    pltpu.make_async_copy(tab_hbm.at[pl.ds(0, t)], buf.at[slot], sem.at[slot]).wait()
    xt = buf[slot]
    xb = xt[:, 0:d].astype(BF16)
    aff = xt[:, d:d + LANES]
    lane = lax.broadcasted_iota(I32, aff.shape, 1)
    gate_w = jnp.sum(jnp.where(lane == e, aff, 0.0), axis=-1, keepdims=True)
    g = _dot(xb, wg_ref[0])
    up = _dot(xb, wu_ref[0])
    hid = (g / (1.0 + jnp.exp(-g))) * up
    y = _dot(hid.astype(BF16), wd_ref[0])
    y_ref[...] = (y * gate_w).astype(BF16)


def _expert_ffn(idx_flat, tab, w_gate, w_up, w_down, *, t):
    ne, d, ff = w_gate.shape
    rows = idx_flat.shape[0]
    nt = rows // ne // t
    dw = tab.shape[1]
    return pl.pallas_call(
        functools.partial(_ffn_kernel, t=t, d=d),
        grid_spec=pltpu.PrefetchScalarGridSpec(
            num_scalar_prefetch=1,
            grid=(ne, nt),
            in_specs=[pl.BlockSpec(memory_space=pl.ANY),
                      pl.BlockSpec((1, d, ff), lambda e, j, idx: (e, 0, 0)),
                      pl.BlockSpec((1, d, ff), lambda e, j, idx: (e, 0, 0)),
                      pl.BlockSpec((1, ff, d), lambda e, j, idx: (e, 0, 0))],
            out_specs=pl.BlockSpec((t, d), lambda e, j, idx: (e * nt + j, 0)),
            scratch_shapes=[pltpu.VMEM((2, t, dw), F32), pltpu.SemaphoreType.DMA((2,))]),
        out_shape=jax.ShapeDtypeStruct((rows, d), BF16),
        compiler_params=_params(("arbitrary", "arbitrary")),
        name="expert_ffn",
    )(idx_flat, tab, w_gate, w_up, w_down)


def _combine_kernel(offs_ref, pos_ref, y_hbm, x1_ref, g2_ref, fg_ref, o_ref, ybuf, acc_ref, sem, *, tt, w, total_rows):
    bi = pl.program_id(0)
    k = pl.program_id(1)
    ne = pos_ref.shape[1]
    nk1 = pl.num_programs(1) + 1
    o0 = (bi * nk1 + k) * ne
    lo = [offs_ref[o0 + e] for e in range(ne)]
    hi = [offs_ref[o0 + ne + e] for e in range(ne)]
    lo8 = [(v // 8) * 8 for v in lo]
    rounds = jnp.int32(0)
    for e in range(ne):
        rounds = jnp.maximum(rounds, (hi[e] - lo8[e] + w - 1) // w)
    acc_ref[...] = jnp.zeros(acc_ref.shape, F32)
    pos = pos_ref[0]

    def round_body(r, carry):
        starts = [pl.multiple_of(jnp.minimum(lo8[e] + r * w, total_rows - w), 8) for e in range(ne)]
        copies = [pltpu.make_async_copy(y_hbm.at[pl.ds(starts[e], w)], ybuf.at[pl.ds(e * w, w)], sem.at[e])
                  for e in range(ne)]
        for cp in copies:
            cp.start()
        sub = lax.broadcasted_iota(I32, (w, tt), 0)
        want = [jnp.where(pos[e:e + 1, :] >= lo8[e] + r * w, pos[e:e + 1, :], -1) for e in range(ne)]
        blocks = [jnp.where(sub + starts[e] == want[e], 1.0, 0.0).astype(BF16) for e in range(ne)]
        onehot_t = jnp.concatenate(blocks, axis=0)
        for cp in copies:
            cp.wait()
        acc_ref[...] += lax.dot_general(onehot_t, ybuf[...], (((0,), (0,)), ((), ())), preferred_element_type=F32)
        return carry

    lax.fori_loop(0, rounds, round_body, 0)
    x2 = x1_ref[0] + g2_ref[0] * acc_ref[...]
    o_ref[0] = _rms(x2) * fg_ref[...]


def _combine(offs_flat, pos, y, x1, g2, final_g, *, tt, w):
    b, s, d = x1.shape
    ne = pos.shape[1]
    total_rows = y.shape[0]
    return pl.pallas_call(
        functools.partial(_combine_kernel, tt=tt, w=w, total_rows=total_rows),
        grid_spec=pltpu.PrefetchScalarGridSpec(
            num_scalar_prefetch=1,
            grid=(b, s // tt),
            in_specs=[pl.BlockSpec((1, ne, tt), lambda bi, k, o: (bi, 0, k)),
                      pl.BlockSpec(memory_space=pl.ANY),
                      pl.BlockSpec((1, tt, d), lambda bi, k, o: (bi, k, 0)),
                      pl.BlockSpec((1, 1, d), lambda bi, k, o: (bi, 0, 0)),
                      pl.BlockSpec((1, d), lambda bi, k, o: (0, 0))],
            out_specs=pl.BlockSpec((1, tt, d), lambda bi, k, o: (bi, k, 0)),
            scratch_shapes=[pltpu.VMEM((ne * w, d), BF16), pltpu.VMEM((tt, d), F32),
                            pltpu.SemaphoreType.DMA((ne,))]),
        out_shape=jax.ShapeDtypeStruct((b, s, d), F32),
        compiler_params=_params(("arbitrary", "arbitrary")),
        name="combine_final",
    )(offs_flat, pos, y, x1, g2, final_g)


def _rope_tables(s):
    n_rows = s // GRID_W
    row = jnp.repeat(jnp.arange(n_rows), GRID_W).astype(F32)
    col = jnp.tile(jnp.arange(GRID_W), n_rows).astype(F32)
    inv_freq = ROPE_THETA ** (-jnp.arange(ROPE_PAIRS, dtype=F32) / ROPE_PAIRS)
    ang_r = row[:, None] * inv_freq[None, :]
    ang_c = col[:, None] * inv_freq[None, :]
    cos64 = jnp.concatenate([jnp.cos(ang_r)] * 2 + [jnp.cos(ang_c)] * 2, axis=-1)
    sin64 = jnp.concatenate([-jnp.sin(ang_r), jnp.sin(ang_r), -jnp.sin(ang_c), jnp.sin(ang_c)], axis=-1)
    return cos64, sin64


ROPE_SWAP = np.concatenate([np.arange(16, 32), np.arange(0, 16), np.arange(48, 64), np.arange(32, 48)])


def kernel(x, c, ctx, c_ctx, w_mod, b_mod, norm1_g, norm2_g, w_in, conv_w, q_norm_g, w_uq, kv_norm_g, w_ukv, w_out,
           w_router, w_gate, w_up, w_down, final_g):
    b, s, d = x.shape
    lc = ctx.shape[1]
    assert w_in.shape[0] == 1, "single-layer stack"
    assert b <= 7 and s % max(ROW_TM, ATT_TQ, ATT_TK, CMB_TT) == 0
    cap = EC_FACTOR * s // N_EXPERTS
    assert (b * cap) % FFN_T == 0 and cap % IDX_PC == 0

    cvec = jnp.zeros((8, d), F32).at[0:b].set(c).at[b].set(c_ctx)
    mod = _modulation(cvec, w_mod[0], b_mod[0][None, :])
    sh1, sc1, g1, sh2, sc2, g2 = [mod[0:b, None, i * d:(i + 1) * d] for i in range(6)]
    shc1, scc1 = mod[b:b + 1, None, 0:d], mod[b:b + 1, None, d:2 * d]

    n_main = 3 * CONV_WIDTH + Q_LORA + KV_LORA
    w_main = w_in[0, :, 0:n_main].astype(BF16)
    w_kr = w_in[0, :, n_main:]
    w_kr2 = jnp.concatenate([w_kr, w_kr[:, ROPE_SWAP]], axis=-1).astype(BF16)
    w_qn = w_uq[0, :, :, 0:QK_NOPE].reshape(Q_LORA, MLA_HEADS * QK_NOPE).astype(BF16)
    w_qr3 = w_uq[0, :, :, QK_NOPE:]
    w_qr = w_qr3.reshape(Q_LORA, MLA_HEADS * QK_ROPE).astype(BF16)
    w_qrs = w_qr3[:, :, ROPE_SWAP].reshape(Q_LORA, MLA_HEADS * QK_ROPE).astype(BF16)
    w_kn = w_ukv[0, :, :, 0:QK_NOPE].reshape(KV_LORA, MLA_HEADS * QK_NOPE).astype(BF16)
    w_v = w_ukv[0, :, :, QK_NOPE:].reshape(KV_LORA, MLA_HEADS * V_DIM).astype(BF16)
    w_o = w_out[0].astype(BF16)
    wr = jnp.zeros((d, LANES), F32).at[:, 0:N_EXPERTS].set(w_router[0])
    wr_hi = wr.astype(BF16)
    wr_lo = (wr - wr_hi.astype(F32)).astype(BF16)
    wg, wu, wd = w_gate[0].astype(BF16), w_up[0].astype(BF16), w_down[0].astype(BF16)

    cos64, sin64 = _rope_tables(s)
    cos2 = jnp.concatenate([cos64, cos64], axis=-1)
    sin2 = jnp.concatenate([sin64, sin64], axis=-1)
    ktab = jnp.concatenate([cos64, sin64], axis=-1)

    n1 = norm1_g[0][None, :]
    qg, kvg = q_norm_g[0][None, :], kv_norm_g[0][None, :]
    u, bg, cqn, ckvn, kr2 = _inproj(x, sc1, sh1, n1, w_main, w_kr2, qg, kvg, with_conv=True, tm=ROW_TM)
    _, ckvn_c, kr2_c = _inproj(ctx, scc1, shc1, n1, w_main, w_kr2, qg, kvg, with_conv=False, tm=lc)

    q = _qproj(cqn, cos2, sin2, w_qn, w_qr, w_qrs, tm=ROW_TM)
    k, v = _kvproj(ckvn, kr2, ktab, w_kn, w_v, tm=ROW_TM)
    kc, vc = _kvproj(ckvn_c, kr2_c, None, w_kn, w_v, tm=lc)
    y_attn = _attention(q, k, v, kc, vc, tq=ATT_TQ, tk=ATT_TK)

    x1, tab, aff_t = _outproj(u, bg, conv_w[0], y_attn, w_o, x, g1, norm2_g[0][None, :], sc2, sh2, wr_hi, wr_lo,
                              tm=ROW_TM)

    pos, offs, idx_t = _routing(aff_t, cap=cap, blk=CMB_TT)
    nk1 = s // CMB_TT + 1
    offs_flat = jnp.swapaxes(offs[:, :, 0:nk1], 1, 2).reshape(-1)
    idx_flat = jnp.transpose(idx_t, (2, 0, 1)).reshape(-1)
    y = _expert_ffn(idx_flat, tab.reshape(b * s, d + LANES), wg, wu, wd, t=FFN_T)
    return _combine(offs_flat, pos, y, x1, g2, final_g[None, :], tt=CMB_TT, w=CMB_W)
```

```python
import functools
import math

import jax
import jax.numpy as jnp
import numpy as np
from jax import lax
from jax.experimental import pallas as pl
from jax.experimental.pallas import tpu as pltpu

F32 = jnp.float32
BF16 = jnp.bfloat16
I32 = jnp.int32

GRID_W = 64
CONV_WIDTH = 1024
MLA_HEADS = 8
QK_NOPE = 128
QK_ROPE = 64
V_DIM = 128
Q_LORA = 512
KV_LORA = 512
QK_DIM = QK_NOPE + QK_ROPE
MLA_WIDTH = MLA_HEADS * V_DIM
V_EXT = 2 * V_DIM
N_EXPERTS = 16
EC_FACTOR = 2
ROPE_THETA = 10000.0
ROPE_PAIRS = QK_ROPE // 4
ATTN_SCALE = 1.0 / math.sqrt(QK_DIM)
LOG2E = math.log2(math.e)
EPS = 1e-6

LANES = 128
VMEM_LIMIT = 56 * 1024 * 1024

MOD_TN = 1024
ROW_TM = 512
ATT_TQ = 512
ATT_TK = 512
FFN_T = 256
CMB_TT = 256
CMB_W = 64
IDX_PC = 64


def _dot(a, b):
    return jnp.dot(a, b, preferred_element_type=F32)


def _params(sem, vmem=VMEM_LIMIT):
    return pltpu.CompilerParams(dimension_semantics=sem, vmem_limit_bytes=vmem)


def _resident(shape):
    nd = len(shape)
    return pl.BlockSpec(shape, lambda *_: (0,) * nd, pipeline_mode=pl.Buffered(1))


def _mod_kernel(c_ref, w_ref, b_ref, o_ref):
    cv = c_ref[...]
    a = cv / (1.0 + jnp.exp(-cv))
    a_hi = a.astype(BF16)
    a_lo = (a - a_hi.astype(F32)).astype(BF16)
    w = w_ref[...]
    w_hi = w.astype(BF16)
    w_lo = (w - w_hi.astype(F32)).astype(BF16)
    o_ref[...] = _dot(a_hi, w_hi) + _dot(a_lo, w_hi) + _dot(a_hi, w_lo) + b_ref[...]


def _modulation(cvec, w_mod, b_mod):
    d, n = w_mod.shape
    return pl.pallas_call(
        _mod_kernel,
        grid=(n // MOD_TN,),
        in_specs=[pl.BlockSpec((8, d), lambda j: (0, 0)),
                  pl.BlockSpec((d, MOD_TN), lambda j: (0, j)),
                  pl.BlockSpec((1, MOD_TN), lambda j: (0, j))],
        out_specs=pl.BlockSpec((8, MOD_TN), lambda j: (0, j)),
        out_shape=jax.ShapeDtypeStruct((8, n), F32),
        compiler_params=_params(("arbitrary",)),
        name="modulation",
    )(cvec, w_mod, b_mod)


def _rms(v):
    return v * lax.rsqrt(jnp.mean(v * v, axis=-1, keepdims=True) + EPS)


def _inproj_kernel(x_ref, sc_ref, sh_ref, g_ref, w_ref, wkr_ref, qg_ref, kvg_ref, *outs, with_conv):
    h = _rms(x_ref[0]) * g_ref[...]
    h = h * (1.0 + sc_ref[0]) + sh_ref[0]
    hb = h.astype(BF16)
    c = CONV_WIDTH
    if with_conv:
        u_ref, bg_ref, cq_ref, ckv_ref, kr_ref = outs
        xin = _dot(hb, w_ref[:, 0:c])
        cg = _dot(hb, w_ref[:, 2 * c:3 * c])
        u_ref[0] = (cg * xin).astype(BF16)
        bg_ref[0] = _dot(hb, w_ref[:, c:2 * c]).astype(BF16)
    else:
        cq_ref, ckv_ref, kr_ref = outs
    o = 3 * c
    cq = _dot(hb, w_ref[:, o:o + Q_LORA])
    cq_ref[0] = (_rms(cq) * qg_ref[...]).astype(BF16)
    ckv = _dot(hb, w_ref[:, o + Q_LORA:o + Q_LORA + KV_LORA])
    ckv_ref[0] = (_rms(ckv) * kvg_ref[...]).astype(BF16)
    kr_ref[0] = _dot(hb, wkr_ref[...])


def _inproj(x, scale, shift, gain, w_main, w_kr2, q_g, kv_g, *, with_conv, tm):
    b, s, d = x.shape
    per_batch = scale.shape[0] > 1
    vec = pl.BlockSpec((1, 1, d), (lambda bi, i: (bi, 0, 0)) if per_batch else (lambda bi, i: (0, 0, 0)))
    row = lambda n: pl.BlockSpec((1, tm, n), lambda bi, i: (bi, i, 0))
    outs, specs = [], []
    if with_conv:
        outs += [jax.ShapeDtypeStruct((b, s, CONV_WIDTH), BF16)] * 2
        specs += [row(CONV_WIDTH)] * 2
    outs += [jax.ShapeDtypeStruct((b, s, Q_LORA), BF16), jax.ShapeDtypeStruct((b, s, KV_LORA), BF16),
             jax.ShapeDtypeStruct((b, s, 2 * QK_ROPE), F32)]
    specs += [row(Q_LORA), row(KV_LORA), row(2 * QK_ROPE)]
    return pl.pallas_call(
        functools.partial(_inproj_kernel, with_conv=with_conv),
        grid=(b, s // tm),
        in_specs=[row(d), vec, vec, _resident((1, d)), _resident(w_main.shape), _resident(w_kr2.shape),
                  _resident((1, Q_LORA)), _resident((1, KV_LORA))],
        out_specs=specs,
        out_shape=outs,
        compiler_params=_params(("arbitrary", "arbitrary")),
        name="inproj_conv" if with_conv else "inproj_ctx",
    )(x, scale, shift, gain, w_main, w_kr2, q_g, kv_g)


def _qproj_kernel(cq_ref, cos_ref, sin_ref, wn_ref, wr_ref, wrs_ref, q_ref):
    cq = cq_ref[0]
    qn = _dot(cq, wn_ref[...])
    qr = _dot(cq, wr_ref[...])
    qrs = _dot(cq, wrs_ref[...])
    reps = MLA_HEADS * QK_ROPE // LANES
    cos = jnp.concatenate([cos_ref[...]] * reps, axis=-1)
    sin = jnp.concatenate([sin_ref[...]] * reps, axis=-1)
    qrot = qr * cos + qrs * sin
    sc = ATTN_SCALE * LOG2E
    for h in range(MLA_HEADS):
        q_ref[0, h, :, 0:QK_NOPE] = (qn[:, h * QK_NOPE:(h + 1) * QK_NOPE] * sc).astype(BF16)
        q_ref[0, h, :, QK_NOPE:QK_DIM] = (qrot[:, h * QK_ROPE:(h + 1) * QK_ROPE] * sc).astype(BF16)


def _qproj(cqn, cos2, sin2, w_qn, w_qr, w_qrs, *, tm):
    b, s, r = cqn.shape
    return pl.pallas_call(
        _qproj_kernel,
        grid=(b, s // tm),
        in_specs=[pl.BlockSpec((1, tm, r), lambda bi, i: (bi, i, 0)),
                  pl.BlockSpec((tm, LANES), lambda bi, i: (i, 0)),
                  pl.BlockSpec((tm, LANES), lambda bi, i: (i, 0)),
                  _resident(w_qn.shape), _resident(w_qr.shape), _resident(w_qrs.shape)],
        out_specs=pl.BlockSpec((1, MLA_HEADS, tm, QK_DIM), lambda bi, i: (bi, 0, i, 0)),
        out_shape=jax.ShapeDtypeStruct((b, MLA_HEADS, s, QK_DIM), BF16),
        compiler_params=_params(("arbitrary", "arbitrary")),
        name="qproj",
    )(cqn, cos2, sin2, w_qn, w_qr, w_qrs)


def _kvproj_kernel(ckv_ref, kr_ref, *rest, rope):
    if rope:
        tab_ref, wk_ref, wv_ref, k_ref, v_ref = rest
        prod = kr_ref[0] * tab_ref[...]
        krot = prod[:, 0:QK_ROPE] + prod[:, QK_ROPE:2 * QK_ROPE]
    else:
        wk_ref, wv_ref, k_ref, v_ref = rest
        krot = kr_ref[0][:, 0:QK_ROPE]
    ckv = ckv_ref[0]
    kn = _dot(ckv, wk_ref[...])
    vv = _dot(ckv, wv_ref[...])
    krot = krot.astype(BF16)
    for h in range(MLA_HEADS):
        k_ref[0, h, :, 0:QK_NOPE] = kn[:, h * QK_NOPE:(h + 1) * QK_NOPE].astype(BF16)
        k_ref[0, h, :, QK_NOPE:QK_DIM] = krot
        v_ref[0, h, :, 0:V_DIM] = vv[:, h * V_DIM:(h + 1) * V_DIM].astype(BF16)
        v_ref[0, h, :, V_DIM:V_EXT] = jnp.ones((vv.shape[0], V_EXT - V_DIM), BF16)


def _kvproj(ckvn, kr2, tab, w_kn, w_v, *, tm):
    b, s, r = ckvn.shape
    rope = tab is not None
    ins = [ckvn, kr2]
    specs = [pl.BlockSpec((1, tm, r), lambda bi, i: (bi, i, 0)),
             pl.BlockSpec((1, tm, 2 * QK_ROPE), lambda bi, i: (bi, i, 0))]
    if rope:
        ins.append(tab)
        specs.append(pl.BlockSpec((tm, 2 * QK_ROPE), lambda bi, i: (i, 0)))
    ins += [w_kn, w_v]
    specs += [_resident(w_kn.shape), _resident(w_v.shape)]
    return pl.pallas_call(
        functools.partial(_kvproj_kernel, rope=rope),
        grid=(b, s // tm),
        in_specs=specs,
        out_specs=[pl.BlockSpec((1, MLA_HEADS, tm, QK_DIM), lambda bi, i: (bi, 0, i, 0)),
                   pl.BlockSpec((1, MLA_HEADS, tm, V_EXT), lambda bi, i: (bi, 0, i, 0))],
        out_shape=[jax.ShapeDtypeStruct((b, MLA_HEADS, s, QK_DIM), BF16),
                   jax.ShapeDtypeStruct((b, MLA_HEADS, s, V_EXT), BF16)],
        compiler_params=_params(("arbitrary", "arbitrary")),
        name="kvproj_rope" if rope else "kvproj_ctx",
    )(*ins)


def _attn_kernel(q_ref, k_ref, v_ref, kc_ref, vc_ref, o_ref, m_ref, acc_ref, *, tk):
    q = q_ref[0, 0]
    m_ref[...] = jnp.full(m_ref.shape, -jnp.inf, F32)
    acc_ref[...] = jnp.zeros(acc_ref.shape, F32)

    def lanes(a, n):
        return jnp.concatenate([a] * (n // LANES), axis=1)

    def scores(k):
        return lax.dot_general(q, k, (((1,), (1,)), ((), ())), preferred_element_type=F32)

    def update(s, v):
        m_old = m_ref[...]
        m_new = jnp.maximum(m_old, jnp.max(s, axis=-1, keepdims=True))
        alpha = jnp.exp2(m_old - m_new)
        p = jnp.exp2(s - lanes(m_new, s.shape[1]))
        acc_ref[...] = lanes(alpha, acc_ref.shape[1]) * acc_ref[...] + _dot(p.astype(BF16), v)
        m_ref[...] = m_new

    def chunk(ref, j):
        return ref[0, 0, pl.ds(pl.multiple_of(j * tk, tk), tk), :]

    def body(j, s):
        s_next = scores(chunk(k_ref, j + 1))
        update(s, chunk(v_ref, j))
        return s_next

    n = k_ref.shape[2] // tk
    s_last = lax.fori_loop(0, n - 1, body, scores(chunk(k_ref, 0)))
    s_ctx = scores(kc_ref[0, 0])
    update(s_last, chunk(v_ref, n - 1))
    update(s_ctx, vc_ref[0, 0])
    o_ref[0] = (acc_ref[:, 0:V_DIM] / acc_ref[:, V_DIM:V_EXT]).astype(BF16)


def _attention(q, k, v, kc, vc, *, tq, tk):
    b, h, s, dq = q.shape
    lc = kc.shape[2]
    return pl.pallas_call(
        functools.partial(_attn_kernel, tk=tk),
        grid=(b, h, s // tq),
        in_specs=[pl.BlockSpec((1, 1, tq, dq), lambda bi, hi, i: (bi, hi, i, 0)),
                  pl.BlockSpec((1, 1, s, dq), lambda bi, hi, i: (bi, hi, 0, 0)),
                  pl.BlockSpec((1, 1, s, V_EXT), lambda bi, hi, i: (bi, hi, 0, 0)),
                  pl.BlockSpec((1, 1, lc, dq), lambda bi, hi, i: (bi, hi, 0, 0)),
                  pl.BlockSpec((1, 1, lc, V_EXT), lambda bi, hi, i: (bi, hi, 0, 0))],
        out_specs=pl.BlockSpec((1, tq, V_DIM), lambda bi, hi, i: (bi, i, hi)),
        out_shape=jax.ShapeDtypeStruct((b, s, h * V_DIM), BF16),
        scratch_shapes=[pltpu.VMEM((tq, LANES), F32), pltpu.VMEM((tq, V_EXT), F32)],
        compiler_params=_params(("arbitrary", "arbitrary", "arbitrary")),
        name="attention",
    )(q, k, v, kc, vc)


HALO = 16


def _outproj_kernel(u_ref, up_ref, un_ref, bg_ref, cw_ref, ya_ref, wo_ref, x_ref, g1_ref, n2_ref, sc_ref, sh_ref,
                    wrh_ref, wrl_ref, x1_ref, tab_ref, afft_ref, *, tm, d):
    i = pl.program_id(1)
    last = pl.num_programs(1) - 1
    u = u_ref[0].astype(F32)
    rows = lax.broadcasted_iota(I32, (tm, 1), 0)
    prev_row = jnp.where(i > 0, up_ref[0][HALO - 1:HALO, :].astype(F32), 0.0)
    next_row = jnp.where(i < last, un_ref[0][0:1, :].astype(F32), 0.0)
    um1 = jnp.where(rows == 0, prev_row, pltpu.roll(u, 1, 0))
    up1 = jnp.where(rows == tm - 1, next_row, pltpu.roll(u, tm - 1, 0))
    cw = cw_ref[...]
    yc = bg_ref[0].astype(F32) * (um1 * cw[0:1] + u * cw[1:2] + up1 * cw[2:3])
    y = _dot(yc.astype(BF16), wo_ref[0:CONV_WIDTH, :]) + _dot(ya_ref[0], wo_ref[CONV_WIDTH:, :])
    x1 = x_ref[0] + g1_ref[0] * y
    x1_ref[0] = x1
    h2 = _rms(x1) * n2_ref[...]
    h2 = h2 * (1.0 + sc_ref[0]) + sh_ref[0]
    tab_ref[0, :, 0:d] = h2
    h_hi = h2.astype(BF16)
    h_lo = (h2 - h_hi.astype(F32)).astype(BF16)
    logits = _dot(h_hi, wrh_ref[...]) + _dot(h_lo, wrh_ref[...]) + _dot(h_hi, wrl_ref[...])
    lane = lax.broadcasted_iota(I32, logits.shape, 1)
    logits = jnp.where(lane < N_EXPERTS, logits, -jnp.inf)
    e = jnp.exp(logits - jnp.max(logits, axis=-1, keepdims=True))
    aff = e / jnp.sum(e, axis=-1, keepdims=True)
    tab_ref[0, :, d:d + LANES] = aff
    afft_ref[0] = aff.T[0:N_EXPERTS, :]


def _outproj(u, bg, conv_w, y_attn, w_out, x, g1, n2, sc2, sh2, wr_hi, wr_lo, *, tm):
    b, s, d = x.shape
    nh = tm // HALO
    nhb = s // HALO
    row = lambda n: pl.BlockSpec((1, tm, n), lambda bi, i: (bi, i, 0))
    vec = pl.BlockSpec((1, 1, d), lambda bi, i: (bi, 0, 0))
    return pl.pallas_call(
        functools.partial(_outproj_kernel, tm=tm, d=d),
        grid=(b, s // tm),
        in_specs=[row(CONV_WIDTH),
                  pl.BlockSpec((1, HALO, CONV_WIDTH), lambda bi, i: (bi, jnp.maximum(i * nh - 1, 0), 0)),
                  pl.BlockSpec((1, HALO, CONV_WIDTH), lambda bi, i: (bi, jnp.minimum((i + 1) * nh, nhb - 1), 0)),
                  row(CONV_WIDTH), _resident(conv_w.shape), row(MLA_WIDTH), _resident(w_out.shape), row(d),
                  vec, _resident((1, d)), vec, vec, _resident(wr_hi.shape), _resident(wr_lo.shape)],
        out_specs=[row(d), row(d + LANES), pl.BlockSpec((1, N_EXPERTS, tm), lambda bi, i: (bi, 0, i))],
        out_shape=[jax.ShapeDtypeStruct((b, s, d), F32), jax.ShapeDtypeStruct((b, s, d + LANES), F32),
                   jax.ShapeDtypeStruct((b, N_EXPERTS, s), F32)],
        compiler_params=_params(("arbitrary", "arbitrary")),
        name="outproj_router",
    )(u, u, u, bg, conv_w, y_attn, w_out, x, g1, n2, sc2, sh2, wr_hi, wr_lo)


def _routing_kernel(aff_ref, tri_ref, pos_ref, offs_ref, idx_ref, *, cap, blk, rows_per_expert):
    bi = pl.program_id(0)
    aff = aff_ref[0]
    ne, s = aff.shape
    capf = jnp.float32(cap)

    def bisect(t, prefix):
        cand = prefix | jnp.left_shift(jnp.int32(1), 30 - t)
        cnt = jnp.sum(jnp.where(aff >= lax.bitcast_convert_type(cand, F32), 1.0, 0.0), axis=1, keepdims=True)
        return jnp.where(cnt >= capf, cand, prefix)

    floor_bits = lax.fori_loop(0, 31, bisect, jnp.zeros((ne, 1), I32))
    thr = jnp.min(jnp.where(aff >= lax.bitcast_convert_type(floor_bits, F32), aff, jnp.inf), axis=1, keepdims=True)
    gt = aff > thr
    eq = aff == thr
    need = capf - jnp.sum(jnp.where(gt, 1.0, 0.0), axis=1, keepdims=True)
    tri = tri_ref[...]

    def cumsum_blocks(mask_f32):
        run = jnp.zeros((ne, 1), F32)
        parts, starts = [], []
        for kb in range(s // blk):
            c = _dot(mask_f32[:, kb * blk:(kb + 1) * blk].astype(BF16), tri)
            starts.append(run)
            parts.append(c + run)
            run = run + c[:, blk - 1:blk]
        return jnp.concatenate(parts, axis=1), starts, run

    eqf = jnp.where(eq, 1.0, 0.0)
    eq_incl, _, _ = cumsum_blocks(eqf)
    sel = jnp.where(gt, 1.0, jnp.where(eq & (eq_incl - eqf < need), 1.0, 0.0))
    incl, starts, _ = cumsum_blocks(sel)

    base = (lax.broadcasted_iota(I32, (ne, 1), 0) * rows_per_expert + bi * cap)
    pos_ref[0] = jnp.where(sel > 0.0, (incl - sel).astype(I32) + base, -1)
    lane = lax.broadcasted_iota(I32, (ne, LANES), 1)
    offs = jnp.full((ne, LANES), cap, I32) + base
    for kb, st in enumerate(starts):
        offs = jnp.where(lane == kb, st.astype(I32) + base, offs)
    offs_ref[0] = offs

    tok_base = bi * s
    for e in range(ne):
        crow = incl[e:e + 1, :]

        def chunk(pc, carry):
            p0 = pc * IDX_PC
            pcol = (lax.broadcasted_iota(I32, (IDX_PC, 1), 0) + p0).astype(F32)
            cnt = jnp.sum(jnp.where(crow <= pcol, 1.0, 0.0), axis=1, keepdims=True)
            idx_ref[0, pl.ds(pl.multiple_of(p0, IDX_PC), IDX_PC), e:e + 1] = cnt.astype(I32) + tok_base
            return carry

        lax.fori_loop(0, cap // IDX_PC, chunk, 0)


def _routing(aff_t, *, cap, blk):
    b, ne, s = aff_t.shape
    assert s // blk + 1 <= LANES
    tri = jnp.asarray(np.triu(np.ones((blk, blk), np.float32)), BF16)
    return pl.pallas_call(
        functools.partial(_routing_kernel, cap=cap, blk=blk, rows_per_expert=b * cap),
        grid=(b,),
        in_specs=[pl.BlockSpec((1, ne, s), lambda bi: (bi, 0, 0)), _resident(tri.shape)],
        out_specs=[pl.BlockSpec((1, ne, s), lambda bi: (bi, 0, 0)),
                   pl.BlockSpec((1, ne, LANES), lambda bi: (bi, 0, 0)),
                   pl.BlockSpec((1, cap, ne), lambda bi: (bi, 0, 0))],
        out_shape=[jax.ShapeDtypeStruct((b, ne, s), I32), jax.ShapeDtypeStruct((b, ne, LANES), I32),
                   jax.ShapeDtypeStruct((b, cap, ne), I32)],
        compiler_params=_params(("arbitrary",)),
        name="routing",
    )(aff_t, tri)


def _ffn_kernel(idx_ref, tab_hbm, wg_ref, wu_ref, wd_ref, y_ref, buf_a, buf_b, sem, *, t, d):
    e = pl.program_id(0)
    step = e * pl.num_programs(1) + pl.program_id(1)
    nsteps = pl.num_programs(0) * pl.num_programs(1)

    def row_copy(base, i, buf, sl):
        return pltpu.make_async_copy(tab_hbm.at[pl.ds(idx_ref[base + i], 1)], buf.at[pl.ds(i, 1)], sem.at[sl])

    def wait_rows(buf, sl):
        pltpu.make_async_copy(tab_hbm.at[pl.ds(0, t)], buf, sem.at[sl]).wait()

    @pl.when(step == 0)
    def _():
        def issue(i, carry):
            row_copy(0, i, buf_a, 0).start()
            return carry
        lax.fori_loop(0, t, issue, 0)

    def run(cur, cur_sl, nxt, nxt_sl):
        wait_rows(cur, cur_sl)
        base = jnp.minimum(step + 1, nsteps - 1) * t
        for i in range(t):
            row_copy(base, i, nxt, nxt_sl).start()
        xb = cur[:, 0:d].astype(BF16)
        aff = cur[:, d:d + LANES]
        lane = lax.broadcasted_iota(I32, aff.shape, 1)
        gate_w = jnp.sum(jnp.where(lane == e, aff, 0.0), axis=-1, keepdims=True)
        g = _dot(xb, wg_ref[0])
        up = _dot(xb, wu_ref[0])
        hid = (g / (1.0 + jnp.exp(-g))) * up
        y = _dot(hid.astype(BF16), wd_ref[0])
        y_ref[...] = (y * gate_w).astype(BF16)

        @pl.when(step == nsteps - 1)
        def _():
            wait_rows(nxt, nxt_sl)

    @pl.when(step % 2 == 0)
    def _():
        run(buf_a, 0, buf_b, 1)

    @pl.when(step % 2 == 1)
    def _():
        run(buf_b, 1, buf_a, 0)


def _expert_ffn(idx_flat, tab, w_gate, w_up, w_down, *, t):
    ne, d, ff = w_gate.shape
    rows = idx_flat.shape[0]
    nt = rows // ne // t
    dw = tab.shape[1]
    return pl.pallas_call(
        functools.partial(_ffn_kernel, t=t, d=d),
        grid_spec=pltpu.PrefetchScalarGridSpec(
            num_scalar_prefetch=1,
            grid=(ne, nt),
            in_specs=[pl.BlockSpec(memory_space=pl.ANY),
                      pl.BlockSpec((1, d, ff), lambda e, j, idx: (e, 0, 0)),
                      pl.BlockSpec((1, d, ff), lambda e, j, idx: (e, 0, 0)),
                      pl.BlockSpec((1, ff, d), lambda e, j, idx: (e, 0, 0))],
            out_specs=pl.BlockSpec((t, d), lambda e, j, idx: (e * nt + j, 0)),
            scratch_shapes=[pltpu.VMEM((t, dw), F32), pltpu.VMEM((t, dw), F32), pltpu.SemaphoreType.DMA((2,))]),
        out_shape=jax.ShapeDtypeStruct((rows, d), BF16),
        compiler_params=_params(("arbitrary", "arbitrary")),
        name="expert_ffn",
    )(idx_flat, tab, w_gate, w_up, w_down)


def _combine_kernel(offs_ref, pos_ref, y_hbm, x1_ref, g2_ref, fg_ref, o_ref, ybuf, acc_ref, sem, *, tt, w, total_rows):
    bi = pl.program_id(0)
    k = pl.program_id(1)
    ne = pos_ref.shape[1]
    nk1 = pl.num_programs(1) + 1
    o0 = (bi * nk1 + k) * ne
    lo = [offs_ref[o0 + e] for e in range(ne)]
    hi = [offs_ref[o0 + ne + e] for e in range(ne)]
    lo8 = [(v // 8) * 8 for v in lo]
    rounds = jnp.int32(0)
    for e in range(ne):
        rounds = jnp.maximum(rounds, (hi[e] - lo8[e] + w - 1) // w)
    acc_ref[...] = jnp.zeros(acc_ref.shape, F32)
    pos = pos_ref[0]

    def round_body(r, carry):
        starts = [pl.multiple_of(jnp.minimum(lo8[e] + r * w, total_rows - w), 8) for e in range(ne)]
        copies = [pltpu.make_async_copy(y_hbm.at[pl.ds(starts[e], w)], ybuf.at[pl.ds(e * w, w)], sem.at[e])
                  for e in range(ne)]
        for cp in copies:
            cp.start()
        sub = lax.broadcasted_iota(I32, (w, tt), 0)
        want = [jnp.where(pos[e:e + 1, :] >= lo8[e] + r * w, pos[e:e + 1, :], -1) for e in range(ne)]
        blocks = [jnp.where(sub + starts[e] == want[e], 1.0, 0.0).astype(BF16) for e in range(ne)]
        onehot_t = jnp.concatenate(blocks, axis=0)
        for cp in copies:
            cp.wait()
        acc_ref[...] += lax.dot_general(onehot_t, ybuf[...], (((0,), (0,)), ((), ())), preferred_element_type=F32)
        return carry

    lax.fori_loop(0, rounds, round_body, 0)
    x2 = x1_ref[0] + g2_ref[0] * acc_ref[...]
    o_ref[0] = _rms(x2) * fg_ref[...]


def _combine(offs_flat, pos, y, x1, g2, final_g, *, tt, w):
    b, s, d = x1.shape
    ne = pos.shape[1]
    total_rows = y.shape[0]
    return pl.pallas_call(
        functools.partial(_combine_kernel, tt=tt, w=w, total_rows=total_rows),
        grid_spec=pltpu.PrefetchScalarGridSpec(
            num_scalar_prefetch=1,
            grid=(b, s // tt),
            in_specs=[pl.BlockSpec((1, ne, tt), lambda bi, k, o: (bi, 0, k)),
                      pl.BlockSpec(memory_space=pl.ANY),
                      pl.BlockSpec((1, tt, d), lambda bi, k, o: (bi, k, 0)),
                      pl.BlockSpec((1, 1, d), lambda bi, k, o: (bi, 0, 0)),
                      pl.BlockSpec((1, d), lambda bi, k, o: (0, 0))],
            out_specs=pl.BlockSpec((1, tt, d), lambda bi, k, o: (bi, k, 0)),
            scratch_shapes=[pltpu.VMEM((ne * w, d), BF16), pltpu.VMEM((tt, d), F32),
                            pltpu.SemaphoreType.DMA((ne,))]),
        out_shape=jax.ShapeDtypeStruct((b, s, d), F32),
        compiler_params=_params(("arbitrary", "arbitrary")),
        name="combine_final",
    )(offs_flat, pos, y, x1, g2, final_g)


def _rope_tables(s):
    n_rows = s // GRID_W
    row = jnp.repeat(jnp.arange(n_rows), GRID_W).astype(F32)
    col = jnp.tile(jnp.arange(GRID_W), n_rows).astype(F32)
    inv_freq = ROPE_THETA ** (-jnp.arange(ROPE_PAIRS, dtype=F32) / ROPE_PAIRS)
    ang_r = row[:, None] * inv_freq[None, :]
    ang_c = col[:, None] * inv_freq[None, :]
    cos64 = jnp.concatenate([jnp.cos(ang_r)] * 2 + [jnp.cos(ang_c)] * 2, axis=-1)
    sin64 = jnp.concatenate([-jnp.sin(ang_r), jnp.sin(ang_r), -jnp.sin(ang_c), jnp.sin(ang_c)], axis=-1)
    return cos64, sin64


ROPE_SWAP = np.concatenate([np.arange(16, 32), np.arange(0, 16), np.arange(48, 64), np.arange(32, 48)])


def kernel(x, c, ctx, c_ctx, w_mod, b_mod, norm1_g, norm2_g, w_in, conv_w, q_norm_g, w_uq, kv_norm_g, w_ukv, w_out,
           w_router, w_gate, w_up, w_down, final_g):
    b, s, d = x.shape
    lc = ctx.shape[1]
    assert w_in.shape[0] == 1, "single-layer stack"
    assert b <= 7 and s % max(ROW_TM, ATT_TQ, ATT_TK, CMB_TT) == 0
    cap = EC_FACTOR * s // N_EXPERTS
    assert (b * cap) % FFN_T == 0 and cap % IDX_PC == 0

    cvec = jnp.zeros((8, d), F32).at[0:b].set(c).at[b].set(c_ctx)
    mod = _modulation(cvec, w_mod[0], b_mod[0][None, :])
    sh1, sc1, g1, sh2, sc2, g2 = [mod[0:b, None, i * d:(i + 1) * d] for i in range(6)]
    shc1, scc1 = mod[b:b + 1, None, 0:d], mod[b:b + 1, None, d:2 * d]

    n_main = 3 * CONV_WIDTH + Q_LORA + KV_LORA
    w_main = w_in[0, :, 0:n_main].astype(BF16)
    w_kr = w_in[0, :, n_main:]
    w_kr2 = jnp.concatenate([w_kr, w_kr[:, ROPE_SWAP]], axis=-1).astype(BF16)
    w_qn = w_uq[0, :, :, 0:QK_NOPE].reshape(Q_LORA, MLA_HEADS * QK_NOPE).astype(BF16)
    w_qr3 = w_uq[0, :, :, QK_NOPE:]
    w_qr = w_qr3.reshape(Q_LORA, MLA_HEADS * QK_ROPE).astype(BF16)
    w_qrs = w_qr3[:, :, ROPE_SWAP].reshape(Q_LORA, MLA_HEADS * QK_ROPE).astype(BF16)
    w_kn = w_ukv[0, :, :, 0:QK_NOPE].reshape(KV_LORA, MLA_HEADS * QK_NOPE).astype(BF16)
    w_v = w_ukv[0, :, :, QK_NOPE:].reshape(KV_LORA, MLA_HEADS * V_DIM).astype(BF16)
    w_o = w_out[0].astype(BF16)
    wr = jnp.zeros((d, LANES), F32).at[:, 0:N_EXPERTS].set(w_router[0])
    wr_hi = wr.astype(BF16)
    wr_lo = (wr - wr_hi.astype(F32)).astype(BF16)
    wg, wu, wd = w_gate[0].astype(BF16), w_up[0].astype(BF16), w_down[0].astype(BF16)

    cos64, sin64 = _rope_tables(s)
    cos2 = jnp.concatenate([cos64, cos64], axis=-1)
    sin2 = jnp.concatenate([sin64, sin64], axis=-1)
    ktab = jnp.concatenate([cos64, sin64], axis=-1)

    n1 = norm1_g[0][None, :]
    qg, kvg = q_norm_g[0][None, :], kv_norm_g[0][None, :]
    u, bg, cqn, ckvn, kr2 = _inproj(x, sc1, sh1, n1, w_main, w_kr2, qg, kvg, with_conv=True, tm=ROW_TM)
    _, ckvn_c, kr2_c = _inproj(ctx, scc1, shc1, n1, w_main, w_kr2, qg, kvg, with_conv=False, tm=lc)

    q = _qproj(cqn, cos2, sin2, w_qn, w_qr, w_qrs, tm=ROW_TM)
    k, v = _kvproj(ckvn, kr2, ktab, w_kn, w_v, tm=ROW_TM)
    kc, vc = _kvproj(ckvn_c, kr2_c, None, w_kn, w_v, tm=lc)
    y_attn = _attention(q, k, v, kc, vc, tq=ATT_TQ, tk=ATT_TK)

    x1, tab, aff_t = _outproj(u, bg, conv_w[0], y_attn, w_o, x, g1, norm2_g[0][None, :], sc2, sh2, wr_hi, wr_lo,
                              tm=ROW_TM)

    pos, offs, idx_t = _routing(aff_t, cap=cap, blk=CMB_TT)
    nk1 = s // CMB_TT + 1
    offs_flat = jnp.swapaxes(offs[:, :, 0:nk1], 1, 2).reshape(-1)
    idx_flat = jnp.transpose(idx_t, (2, 0, 1)).reshape(-1)
    y = _expert_ffn(idx_flat, tab.reshape(b * s, d + LANES), wg, wu, wd, t=FFN_T)
    return _combine(offs_flat, pos, y, x1, g2, final_g[None, :], tt=CMB_TT, w=CMB_W)
```

```python
import functools
import math

import jax
import jax.numpy as jnp
import numpy as np
from jax import lax
from jax.experimental import pallas as pl
from jax.experimental.pallas import tpu as pltpu

F32 = jnp.float32
BF16 = jnp.bfloat16
I32 = jnp.int32

GRID_W = 64
CONV_WIDTH = 1024
MLA_HEADS = 8
QK_NOPE = 128
QK_ROPE = 64
V_DIM = 128
Q_LORA = 512
KV_LORA = 512
QK_DIM = QK_NOPE + QK_ROPE
MLA_WIDTH = MLA_HEADS * V_DIM
V_EXT = 2 * V_DIM
N_EXPERTS = 16
EC_FACTOR = 2
ROPE_THETA = 10000.0
ROPE_PAIRS = QK_ROPE // 4
ATTN_SCALE = 1.0 / math.sqrt(QK_DIM)
LOG2E = math.log2(math.e)
EPS = 1e-6

LANES = 128
VMEM_LIMIT = 56 * 1024 * 1024

MOD_TN = 1024
ROW_TM = 512
ATT_TQ = 512
ATT_TK = 512
FFN_T = 256
CMB_TT = 256
CMB_W = 64
IDX_PC = 64


def _dot(a, b):
    return jnp.dot(a, b, preferred_element_type=F32)


def _params(sem, vmem=VMEM_LIMIT):
    return pltpu.CompilerParams(dimension_semantics=sem, vmem_limit_bytes=vmem)


def _resident(shape):
    nd = len(shape)
    return pl.BlockSpec(shape, lambda *_: (0,) * nd, pipeline_mode=pl.Buffered(1))


def _mod_kernel(c_ref, w_ref, b_ref, o_ref):
    cv = c_ref[...]
    a = cv / (1.0 + jnp.exp(-cv))
    a_hi = a.astype(BF16)
    a_lo = (a - a_hi.astype(F32)).astype(BF16)
    w = w_ref[...]
    w_hi = w.astype(BF16)
    w_lo = (w - w_hi.astype(F32)).astype(BF16)
    o_ref[...] = _dot(a_hi, w_hi) + _dot(a_lo, w_hi) + _dot(a_hi, w_lo) + b_ref[...]


def _modulation(cvec, w_mod, b_mod):
    d, n = w_mod.shape
    return pl.pallas_call(
        _mod_kernel,
        grid=(n // MOD_TN,),
        in_specs=[pl.BlockSpec((8, d), lambda j: (0, 0)),
                  pl.BlockSpec((d, MOD_TN), lambda j: (0, j)),
                  pl.BlockSpec((1, MOD_TN), lambda j: (0, j))],
        out_specs=pl.BlockSpec((8, MOD_TN), lambda j: (0, j)),
        out_shape=jax.ShapeDtypeStruct((8, n), F32),
        compiler_params=_params(("arbitrary",)),
        name="modulation",
    )(cvec, w_mod, b_mod)


def _rms(v):
    return v * lax.rsqrt(jnp.mean(v * v, axis=-1, keepdims=True) + EPS)


def _inproj_kernel(x_ref, sc_ref, sh_ref, g_ref, w_ref, wkr_ref, qg_ref, kvg_ref, *outs, with_conv):
    h = _rms(x_ref[0]) * g_ref[...]
    h = h * (1.0 + sc_ref[0]) + sh_ref[0]
    hb = h.astype(BF16)
    c = CONV_WIDTH
    if with_conv:
        u_ref, bg_ref, cq_ref, ckv_ref, kr_ref = outs
        xin = _dot(hb, w_ref[:, 0:c])
        cg = _dot(hb, w_ref[:, 2 * c:3 * c])
        u_ref[0] = (cg * xin).astype(BF16)
        bg_ref[0] = _dot(hb, w_ref[:, c:2 * c]).astype(BF16)
    else:
        cq_ref, ckv_ref, kr_ref = outs
    o = 3 * c
    cq = _dot(hb, w_ref[:, o:o + Q_LORA])
    cq_ref[0] = (_rms(cq) * qg_ref[...]).astype(BF16)
    ckv = _dot(hb, w_ref[:, o + Q_LORA:o + Q_LORA + KV_LORA])
    ckv_ref[0] = (_rms(ckv) * kvg_ref[...]).astype(BF16)
    kr_ref[0] = _dot(hb, wkr_ref[...])


def _inproj(x, scale, shift, gain, w_main, w_kr2, q_g, kv_g, *, with_conv, tm):
    b, s, d = x.shape
    per_batch = scale.shape[0] > 1
    vec = pl.BlockSpec((1, 1, d), (lambda bi, i: (bi, 0, 0)) if per_batch else (lambda bi, i: (0, 0, 0)))
    row = lambda n: pl.BlockSpec((1, tm, n), lambda bi, i: (bi, i, 0))
    outs, specs = [], []
    if with_conv:
        outs += [jax.ShapeDtypeStruct((b, s, CONV_WIDTH), BF16)] * 2
        specs += [row(CONV_WIDTH)] * 2
    outs += [jax.ShapeDtypeStruct((b, s, Q_LORA), BF16), jax.ShapeDtypeStruct((b, s, KV_LORA), BF16),
             jax.ShapeDtypeStruct((b, s, 2 * QK_ROPE), F32)]
    specs += [row(Q_LORA), row(KV_LORA), row(2 * QK_ROPE)]
    return pl.pallas_call(
        functools.partial(_inproj_kernel, with_conv=with_conv),
        grid=(b, s // tm),
        in_specs=[row(d), vec, vec, _resident((1, d)), _resident(w_main.shape), _resident(w_kr2.shape),
                  _resident((1, Q_LORA)), _resident((1, KV_LORA))],
        out_specs=specs,
        out_shape=outs,
        compiler_params=_params(("arbitrary", "arbitrary")),
        name="inproj_conv" if with_conv else "inproj_ctx",
    )(x, scale, shift, gain, w_main, w_kr2, q_g, kv_g)


def _qproj_kernel(cq_ref, cos_ref, sin_ref, wn_ref, wr_ref, wrs_ref, q_ref):
    cq = cq_ref[0]
    qn = _dot(cq, wn_ref[...])
    qr = _dot(cq, wr_ref[...])
    qrs = _dot(cq, wrs_ref[...])
    reps = MLA_HEADS * QK_ROPE // LANES
    cos = jnp.concatenate([cos_ref[...]] * reps, axis=-1)
    sin = jnp.concatenate([sin_ref[...]] * reps, axis=-1)
    qrot = qr * cos + qrs * sin
    sc = ATTN_SCALE * LOG2E
    for h in range(MLA_HEADS):
        q_ref[0, h, :, 0:QK_NOPE] = (qn[:, h * QK_NOPE:(h + 1) * QK_NOPE] * sc).astype(BF16)
        q_ref[0, h, :, QK_NOPE:QK_DIM] = (qrot[:, h * QK_ROPE:(h + 1) * QK_ROPE] * sc).astype(BF16)


def _qproj(cqn, cos2, sin2, w_qn, w_qr, w_qrs, *, tm):
    b, s, r = cqn.shape
    return pl.pallas_call(
        _qproj_kernel,
        grid=(b, s // tm),
        in_specs=[pl.BlockSpec((1, tm, r), lambda bi, i: (bi, i, 0)),
                  pl.BlockSpec((tm, LANES), lambda bi, i: (i, 0)),
                  pl.BlockSpec((tm, LANES), lambda bi, i: (i, 0)),
                  _resident(w_qn.shape), _resident(w_qr.shape), _resident(w_qrs.shape)],
        out_specs=pl.BlockSpec((1, MLA_HEADS, tm, QK_DIM), lambda bi, i: (bi, 0, i, 0)),
        out_shape=jax.ShapeDtypeStruct((b, MLA_HEADS, s, QK_DIM), BF16),
        compiler_params=_params(("arbitrary", "arbitrary")),
        name="qproj",
    )(cqn, cos2, sin2, w_qn, w_qr, w_qrs)


def _kvproj_kernel(ckv_ref, kr_ref, *rest, rope):
    if rope:
        tab_ref, wk_ref, wv_ref, k_ref, v_ref = rest
        prod = kr_ref[0] * tab_ref[...]
        krot = prod[:, 0:QK_ROPE] + prod[:, QK_ROPE:2 * QK_ROPE]
    else:
        wk_ref, wv_ref, k_ref, v_ref = rest
        krot = kr_ref[0][:, 0:QK_ROPE]
    ckv = ckv_ref[0]
    kn = _dot(ckv, wk_ref[...])
    vv = _dot(ckv, wv_ref[...])
    krot = krot.astype(BF16)
    for h in range(MLA_HEADS):
        k_ref[0, h, :, 0:QK_NOPE] = kn[:, h * QK_NOPE:(h + 1) * QK_NOPE].astype(BF16)
        k_ref[0, h, :, QK_NOPE:QK_DIM] = krot
        v_ref[0, h, :, 0:V_DIM] = vv[:, h * V_DIM:(h + 1) * V_DIM].astype(BF16)
        v_ref[0, h, :, V_DIM:V_EXT] = jnp.ones((vv.shape[0], V_EXT - V_DIM), BF16)


def _kvproj(ckvn, kr2, tab, w_kn, w_v, *, tm):
    b, s, r = ckvn.shape
    rope = tab is not None
    ins = [ckvn, kr2]
    specs = [pl.BlockSpec((1, tm, r), lambda bi, i: (bi, i, 0)),
             pl.BlockSpec((1, tm, 2 * QK_ROPE), lambda bi, i: (bi, i, 0))]
    if rope:
        ins.append(tab)
        specs.append(pl.BlockSpec((tm, 2 * QK_ROPE), lambda bi, i: (i, 0)))
    ins += [w_kn, w_v]
    specs += [_resident(w_kn.shape), _resident(w_v.shape)]
    return pl.pallas_call(
        functools.partial(_kvproj_kernel, rope=rope),
        grid=(b, s // tm),
        in_specs=specs,
        out_specs=[pl.BlockSpec((1, MLA_HEADS, tm, QK_DIM), lambda bi, i: (bi, 0, i, 0)),
                   pl.BlockSpec((1, MLA_HEADS, tm, V_EXT), lambda bi, i: (bi, 0, i, 0))],
        out_shape=[jax.ShapeDtypeStruct((b, MLA_HEADS, s, QK_DIM), BF16),
                   jax.ShapeDtypeStruct((b, MLA_HEADS, s, V_EXT), BF16)],
        compiler_params=_params(("arbitrary", "arbitrary")),
        name="kvproj_rope" if rope else "kvproj_ctx",
    )(*ins)


def _attn_kernel(q_ref, k_ref, v_ref, kc_ref, vc_ref, o_ref, m_ref, acc_ref, sa_ref, sb_ref, *, tk):
    q = q_ref[0, 0]
    m_ref[...] = jnp.full(m_ref.shape, -jnp.inf, F32)
    acc_ref[...] = jnp.zeros(acc_ref.shape, F32)

    def lanes(a, n):
        return jnp.concatenate([a] * (n // LANES), axis=1)

    def scores(k):
        return lax.dot_general(q, k, (((1,), (1,)), ((), ())), preferred_element_type=F32)

    def update(s, v):
        m_old = m_ref[...]
        m_new = jnp.maximum(m_old, jnp.max(s, axis=-1, keepdims=True))
        alpha = jnp.exp2(m_old - m_new)
        p = jnp.exp2(s - lanes(m_new, s.shape[1]))
        acc_ref[...] = lanes(alpha, acc_ref.shape[1]) * acc_ref[...] + _dot(p.astype(BF16), v)
        m_ref[...] = m_new

    def chunk(ref, j):
        return ref[0, 0, pl.ds(pl.multiple_of(j * tk, tk), tk), :]

    n = k_ref.shape[2] // tk
    bufs = (sa_ref, sb_ref)
    sa_ref[...] = scores(chunk(k_ref, 0))
    for j in range(n):
        nxt = bufs[(j + 1) % 2]
        if j + 1 < n:
            nxt[...] = scores(chunk(k_ref, j + 1))
        else:
            s_ctx = scores(kc_ref[0, 0])
        update(bufs[j % 2][...], chunk(v_ref, j))
    update(s_ctx, vc_ref[0, 0])
    o_ref[0] = (acc_ref[:, 0:V_DIM] / acc_ref[:, V_DIM:V_EXT]).astype(BF16)


def _attention(q, k, v, kc, vc, *, tq, tk):
    b, h, s, dq = q.shape
    lc = kc.shape[2]
    return pl.pallas_call(
        functools.partial(_attn_kernel, tk=tk),
        grid=(b, h, s // tq),
        in_specs=[pl.BlockSpec((1, 1, tq, dq), lambda bi, hi, i: (bi, hi, i, 0)),
                  pl.BlockSpec((1, 1, s, dq), lambda bi, hi, i: (bi, hi, 0, 0)),
                  pl.BlockSpec((1, 1, s, V_EXT), lambda bi, hi, i: (bi, hi, 0, 0)),
                  pl.BlockSpec((1, 1, lc, dq), lambda bi, hi, i: (bi, hi, 0, 0)),
                  pl.BlockSpec((1, 1, lc, V_EXT), lambda bi, hi, i: (bi, hi, 0, 0))],
        out_specs=pl.BlockSpec((1, tq, V_DIM), lambda bi, hi, i: (bi, i, hi)),
        out_shape=jax.ShapeDtypeStruct((b, s, h * V_DIM), BF16),
        scratch_shapes=[pltpu.VMEM((tq, LANES), F32), pltpu.VMEM((tq, V_EXT), F32),
                        pltpu.VMEM((tq, tk), F32), pltpu.VMEM((tq, tk), F32)],
        compiler_params=_params(("arbitrary", "arbitrary", "arbitrary")),
        name="attention",
    )(q, k, v, kc, vc)


HALO = 16


def _outproj_kernel(u_ref, up_ref, un_ref, bg_ref, cw_ref, ya_ref, wo_ref, x_ref, g1_ref, n2_ref, sc_ref, sh_ref,
                    wr2_ref, x1_ref, tab_ref, afft_ref, *, tm, d):
    i = pl.program_id(1)
    last = pl.num_programs(1) - 1
    u = u_ref[0].astype(F32)
    rows = lax.broadcasted_iota(I32, (tm, 1), 0)
    prev_row = jnp.where(i > 0, up_ref[0][HALO - 1:HALO, :].astype(F32), 0.0)
    next_row = jnp.where(i < last, un_ref[0][0:1, :].astype(F32), 0.0)
    um1 = jnp.where(rows == 0, prev_row, pltpu.roll(u, 1, 0))
    up1 = jnp.where(rows == tm - 1, next_row, pltpu.roll(u, tm - 1, 0))
    cw = cw_ref[...]
    yc = (bg_ref[0].astype(F32) * (um1 * cw[0:1] + u * cw[1:2] + up1 * cw[2:3])).astype(BF16)
    nsub = 2
    tr = tm // nsub
    for r in range(nsub):
        rs = slice(r * tr, (r + 1) * tr)
        y = _dot(yc[rs], wo_ref[0:CONV_WIDTH, :]) + _dot(ya_ref[0, rs, :], wo_ref[CONV_WIDTH:, :])
        x1 = x_ref[0, rs, :] + g1_ref[0] * y
        x1_ref[0, rs, :] = x1
        h2 = _rms(x1) * n2_ref[...]
        h2 = h2 * (1.0 + sc_ref[0]) + sh_ref[0]
        tab_ref[0, rs, 0:d] = h2
        h_hi = h2.astype(BF16)
        h_lo = (h2 - h_hi.astype(F32)).astype(BF16)
        l2 = _dot(h_hi, wr2_ref[...])
        logits = l2[:, 0:LANES] + l2[:, LANES:2 * LANES] + _dot(h_lo, wr2_ref[:, 0:LANES])
        lane = lax.broadcasted_iota(I32, logits.shape, 1)
        logits = jnp.where(lane < N_EXPERTS, logits, -jnp.inf)
        e = jnp.exp(logits - jnp.max(logits, axis=-1, keepdims=True))
        aff = e / jnp.sum(e, axis=-1, keepdims=True)
        tab_ref[0, rs, d:d + LANES] = aff
        afft_ref[0, :, rs] = aff.T[0:N_EXPERTS, :]


def _outproj(u, bg, conv_w, y_attn, w_out, x, g1, n2, sc2, sh2, wr2, *, tm):
    b, s, d = x.shape
    nh = tm // HALO
    nhb = s // HALO
    row = lambda n: pl.BlockSpec((1, tm, n), lambda bi, i: (bi, i, 0))
    vec = pl.BlockSpec((1, 1, d), lambda bi, i: (bi, 0, 0))
    return pl.pallas_call(
        functools.partial(_outproj_kernel, tm=tm, d=d),
        grid=(b, s // tm),
        in_specs=[row(CONV_WIDTH),
                  pl.BlockSpec((1, HALO, CONV_WIDTH), lambda bi, i: (bi, jnp.maximum(i * nh - 1, 0), 0)),
                  pl.BlockSpec((1, HALO, CONV_WIDTH), lambda bi, i: (bi, jnp.minimum((i + 1) * nh, nhb - 1), 0)),
                  row(CONV_WIDTH), _resident(conv_w.shape), row(MLA_WIDTH), _resident(w_out.shape), row(d),
                  vec, _resident((1, d)), vec, vec, _resident(wr2.shape)],
        out_specs=[row(d), row(d + LANES), pl.BlockSpec((1, N_EXPERTS, tm), lambda bi, i: (bi, 0, i))],
        out_shape=[jax.ShapeDtypeStruct((b, s, d), F32), jax.ShapeDtypeStruct((b, s, d + LANES), F32),
                   jax.ShapeDtypeStruct((b, N_EXPERTS, s), F32)],
        compiler_params=_params(("arbitrary", "arbitrary")),
        name="outproj_router",
    )(u, u, u, bg, conv_w, y_attn, w_out, x, g1, n2, sc2, sh2, wr2)


def _routing_kernel(aff_ref, tri_ref, pos_ref, offs_ref, *, cap, blk, rows_per_expert):
    bi = pl.program_id(0)
    aff = aff_ref[0]
    ne, s = aff.shape
    capf = jnp.float32(cap)

    def bisect(t, prefix):
        cand = prefix | jnp.left_shift(jnp.int32(1), 30 - t)
        cnt = jnp.sum(jnp.where(aff >= lax.bitcast_convert_type(cand, F32), 1.0, 0.0), axis=1, keepdims=True)
        return jnp.where(cnt >= capf, cand, prefix)

    floor_bits = lax.fori_loop(0, 31, bisect, jnp.zeros((ne, 1), I32))
    thr = jnp.min(jnp.where(aff >= lax.bitcast_convert_type(floor_bits, F32), aff, jnp.inf), axis=1, keepdims=True)
    gt = aff > thr
    eq = aff == thr
    need = capf - jnp.sum(jnp.where(gt, 1.0, 0.0), axis=1, keepdims=True)
    tri = tri_ref[...]

    def cumsum_blocks(mask_f32):
        run = jnp.zeros((ne, 1), F32)
        parts, starts = [], []
        for kb in range(s // blk):
            c = _dot(mask_f32[:, kb * blk:(kb + 1) * blk].astype(BF16), tri)
            starts.append(run)
            parts.append(c + run)
            run = run + c[:, blk - 1:blk]
        return jnp.concatenate(parts, axis=1), starts, run

    eqf = jnp.where(eq, 1.0, 0.0)
    eq_incl, _, _ = cumsum_blocks(eqf)
    sel = jnp.where(gt, 1.0, jnp.where(eq & (eq_incl - eqf < need), 1.0, 0.0))
    incl, starts, _ = cumsum_blocks(sel)

    base = (lax.broadcasted_iota(I32, (ne, 1), 0) * rows_per_expert + bi * cap)
    pos_ref[0] = jnp.where(sel > 0.0, (incl - sel).astype(I32) + base, -1)
    lane = lax.broadcasted_iota(I32, (ne, LANES), 1)
    offs = jnp.full((ne, LANES), cap, I32) + base
    for kb, st in enumerate(starts):
        offs = jnp.where(lane == kb, st.astype(I32) + base, offs)
    offs_ref[0] = offs


def _routing(aff_t, *, cap, blk):
    b, ne, s = aff_t.shape
    assert s // blk + 1 <= LANES
    tri = jnp.asarray(np.triu(np.ones((blk, blk), np.float32)), BF16)
    return pl.pallas_call(
        functools.partial(_routing_kernel, cap=cap, blk=blk, rows_per_expert=b * cap),
        grid=(b,),
        in_specs=[pl.BlockSpec((1, ne, s), lambda bi: (bi, 0, 0)), _resident(tri.shape)],
        out_specs=[pl.BlockSpec((1, ne, s), lambda bi: (bi, 0, 0)),
                   pl.BlockSpec((1, ne, LANES), lambda bi: (bi, 0, 0))],
        out_shape=[jax.ShapeDtypeStruct((b, ne, s), I32), jax.ShapeDtypeStruct((b, ne, LANES), I32)],
        compiler_params=_params(("arbitrary",)),
        name="routing",
    )(aff_t, tri)


def _slots_kernel(offs_ref, pos_ref, idx_ref, *, cap, blk, w, rows_per_expert):
    bi = pl.program_id(0)
    ne, s = pos_ref.shape[1], pos_ref.shape[2]
    nb = s // blk
    idx_ref[...] = jnp.zeros(idx_ref.shape, I32)
    sub = lax.broadcasted_iota(I32, (w, blk), 0)
    tok = lax.broadcasted_iota(I32, (w, blk), 1)

    def block(kb, carry):
        tok1 = tok + (kb * blk + bi * s + 1)
        windows, trips = [], []
        for e in range(ne):
            base = e * rows_per_expert + bi * cap
            o = (bi * (nb + 1) + kb) * ne + e
            lo8 = (offs_ref[o] - base) // 8 * 8
            hi = offs_ref[o + ne] - base
            posrow = pos_ref[0, e:e + 1, pl.ds(pl.multiple_of(kb * blk, blk), blk)] - base

            def window(r, c, e=e, lo8=lo8, posrow=posrow):
                start = pl.multiple_of(jnp.minimum(lo8 + r * w, cap - w), 8)
                hit = jnp.where(sub + start == posrow, tok1, 0)
                val = jnp.sum(hit.astype(F32), axis=1, keepdims=True).astype(I32)
                old = idx_ref[0, pl.ds(start, w), e:e + 1]
                idx_ref[0, pl.ds(start, w), e:e + 1] = jnp.where(val > 0, val - 1, old)
                return c

            windows.append(window)
            trips.append((hi - lo8 + w - 1) // w)
        for window in windows:
            window(0, 0)
        for window, n in zip(windows, trips):
            lax.fori_loop(1, n, window, 0)
        return carry

    lax.fori_loop(0, nb, block, 0)


def _slots(offs_flat, pos, *, cap, blk, w):
    b, ne, s = pos.shape
    return pl.pallas_call(
        functools.partial(_slots_kernel, cap=cap, blk=blk, w=w, rows_per_expert=b * cap),
        grid_spec=pltpu.PrefetchScalarGridSpec(
            num_scalar_prefetch=1,
            grid=(b,),
            in_specs=[pl.BlockSpec((1, ne, s), lambda bi, o: (bi, 0, 0))],
            out_specs=pl.BlockSpec((1, cap, ne), lambda bi, o: (bi, 0, 0))),
        out_shape=jax.ShapeDtypeStruct((b, cap, ne), I32),
        compiler_params=_params(("arbitrary",)),
        name="slots",
    )(offs_flat, pos)


def _ffn_kernel(idx_ref, tab_hbm, wg_ref, wu_ref, wd_ref, y_ref, buf_a, buf_b, sem, *, t, d):
    e = pl.program_id(0)
    step = e * pl.num_programs(1) + pl.program_id(1)
    nsteps = pl.num_programs(0) * pl.num_programs(1)

    def row_copy(base, i, buf, sl):
        return pltpu.make_async_copy(tab_hbm.at[pl.ds(idx_ref[base + i], 1)], buf.at[pl.ds(i, 1)], sem.at[sl])

    def wait_rows(buf, sl):
        pltpu.make_async_copy(tab_hbm.at[pl.ds(0, t)], buf, sem.at[sl]).wait()

    @pl.when(step == 0)
    def _():
        def issue(i, carry):
            row_copy(0, i, buf_a, 0).start()
            return carry
        lax.fori_loop(0, t, issue, 0)

    def run(cur, cur_sl, nxt, nxt_sl):
        wait_rows(cur, cur_sl)
        base = jnp.minimum(step + 1, nsteps - 1) * t
        for i in range(t):
            row_copy(base, i, nxt, nxt_sl).start()
        xb = cur[:, 0:d].astype(BF16)
        aff = cur[:, d:d + LANES]
        lane = lax.broadcasted_iota(I32, aff.shape, 1)
        gate_w = jnp.sum(jnp.where(lane == e, aff, 0.0), axis=-1, keepdims=True)
        g = _dot(xb, wg_ref[0])
        up = _dot(xb, wu_ref[0])
        hid = (g / (1.0 + jnp.exp(-g))) * up
        y = _dot(hid.astype(BF16), wd_ref[0])
        y_ref[...] = (y * gate_w).astype(BF16)

        @pl.when(step == nsteps - 1)
        def _():
            wait_rows(nxt, nxt_sl)

    @pl.when(step % 2 == 0)
    def _():
        run(buf_a, 0, buf_b, 1)

    @pl.when(step % 2 == 1)
    def _():
        run(buf_b, 1, buf_a, 0)


def _expert_ffn(idx_flat, tab, w_gate, w_up, w_down, *, t):
    ne, d, ff = w_gate.shape
    rows = idx_flat.shape[0]
    nt = rows // ne // t
    dw = tab.shape[1]
    return pl.pallas_call(
        functools.partial(_ffn_kernel, t=t, d=d),
        grid_spec=pltpu.PrefetchScalarGridSpec(
            num_scalar_prefetch=1,
            grid=(ne, nt),
            in_specs=[pl.BlockSpec(memory_space=pl.ANY),
                      pl.BlockSpec((1, d, ff), lambda e, j, idx: (e, 0, 0)),
                      pl.BlockSpec((1, d, ff), lambda e, j, idx: (e, 0, 0)),
                      pl.BlockSpec((1, ff, d), lambda e, j, idx: (e, 0, 0))],
            out_specs=pl.BlockSpec((t, d), lambda e, j, idx: (e * nt + j, 0)),
            scratch_shapes=[pltpu.VMEM((t, dw), F32), pltpu.VMEM((t, dw), F32), pltpu.SemaphoreType.DMA((2,))]),
        out_shape=jax.ShapeDtypeStruct((rows, d), BF16),
        compiler_params=_params(("arbitrary", "arbitrary")),
        name="expert_ffn",
    )(idx_flat, tab, w_gate, w_up, w_down)


def _combine_kernel(offs_ref, pos_ref, y_hbm, x1_ref, g2_ref, fg_ref, o_ref, ybuf, acc_ref, sem, *, tt, w, total_rows):
    ne = pos_ref.shape[1]
    nk = pl.num_programs(1)
    step = pl.program_id(0) * nk + pl.program_id(1)
    nsteps = pl.num_programs(0) * nk
    slot = step % 2

    def bounds(st):
        o0 = (st // nk * (nk + 1) + st % nk) * ne
        lo8 = [(offs_ref[o0 + e] // 8) * 8 for e in range(ne)]
        hi = [offs_ref[o0 + ne + e] for e in range(ne)]
        return lo8, hi

    def window_copies(lo8, r, sl):
        starts = [pl.multiple_of(jnp.minimum(lo8[e] + r * w, total_rows - w), 8) for e in range(ne)]
        copies = [pltpu.make_async_copy(y_hbm.at[pl.ds(starts[e], w)], ybuf.at[sl, pl.ds(e * w, w)], sem.at[sl, e])
                  for e in range(ne)]
        return starts, copies

    @pl.when(step == 0)
    def _():
        for cp in window_copies(bounds(0)[0], 0, 0)[1]:
            cp.start()

    lo8, hi = bounds(step)
    starts0, copies0 = window_copies(lo8, 0, slot)
    for cp in copies0:
        cp.wait()

    @pl.when(step + 1 < nsteps)
    def _():
        for cp in window_copies(bounds(step + 1)[0], 0, 1 - slot)[1]:
            cp.start()

    pos = pos_ref[0]

    def contribution(r, starts):
        sub = lax.broadcasted_iota(I32, (w, tt), 0)
        want = [jnp.where(pos[e:e + 1, :] >= lo8[e] + r * w, pos[e:e + 1, :], -1) for e in range(ne)]
        blocks = [jnp.where(sub + starts[e] == want[e], 1.0, 0.0).astype(BF16) for e in range(ne)]
        onehot_t = jnp.concatenate(blocks, axis=0)
        return lax.dot_general(onehot_t, ybuf[slot], (((0,), (0,)), ((), ())), preferred_element_type=F32)

    acc_ref[...] = contribution(0, starts0)

    rounds = jnp.int32(1)
    for e in range(ne):
        rounds = jnp.maximum(rounds, (hi[e] - lo8[e] + w - 1) // w)

    def extra_round(r, carry):
        starts, copies = window_copies(lo8, r, slot)
        for cp in copies:
            cp.start()
        for cp in copies:
            cp.wait()
        acc_ref[...] += contribution(r, starts)
        return carry

    lax.fori_loop(1, rounds, extra_round, 0)
    x2 = x1_ref[0] + g2_ref[0] * acc_ref[...]
    o_ref[0] = _rms(x2) * fg_ref[...]


def _combine(offs_flat, pos, y, x1, g2, final_g, *, tt, w):
    b, s, d = x1.shape
    ne = pos.shape[1]
    total_rows = y.shape[0]
    return pl.pallas_call(
        functools.partial(_combine_kernel, tt=tt, w=w, total_rows=total_rows),
        grid_spec=pltpu.PrefetchScalarGridSpec(
            num_scalar_prefetch=1,
            grid=(b, s // tt),
            in_specs=[pl.BlockSpec((1, ne, tt), lambda bi, k, o: (bi, 0, k)),
                      pl.BlockSpec(memory_space=pl.ANY),
                      pl.BlockSpec((1, tt, d), lambda bi, k, o: (bi, k, 0)),
                      pl.BlockSpec((1, 1, d), lambda bi, k, o: (bi, 0, 0)),
                      pl.BlockSpec((1, d), lambda bi, k, o: (0, 0))],
            out_specs=pl.BlockSpec((1, tt, d), lambda bi, k, o: (bi, k, 0)),
            scratch_shapes=[pltpu.VMEM((2, ne * w, d), BF16), pltpu.VMEM((tt, d), F32),
                            pltpu.SemaphoreType.DMA((2, ne))]),
        out_shape=jax.ShapeDtypeStruct((b, s, d), F32),
        compiler_params=_params(("arbitrary", "arbitrary")),
        name="combine_final",
    )(offs_flat, pos, y, x1, g2, final_g)


def _rope_tables(s):
    n_rows = s // GRID_W
    row = np.repeat(np.arange(n_rows), GRID_W).astype(np.float64)
    col = np.tile(np.arange(GRID_W), n_rows).astype(np.float64)
    inv_freq = ROPE_THETA ** (-np.arange(ROPE_PAIRS, dtype=np.float64) / ROPE_PAIRS)
    ang_r = row[:, None] * inv_freq[None, :]
    ang_c = col[:, None] * inv_freq[None, :]
    cos64 = np.concatenate([np.cos(ang_r)] * 2 + [np.cos(ang_c)] * 2, axis=-1).astype(np.float32)
    sin64 = np.concatenate([-np.sin(ang_r), np.sin(ang_r), -np.sin(ang_c), np.sin(ang_c)], axis=-1).astype(np.float32)
    return cos64, sin64


ROPE_SWAP = np.concatenate([np.arange(16, 32), np.arange(0, 16), np.arange(48, 64), np.arange(32, 48)])


def kernel(x, c, ctx, c_ctx, w_mod, b_mod, norm1_g, norm2_g, w_in, conv_w, q_norm_g, w_uq, kv_norm_g, w_ukv, w_out,
           w_router, w_gate, w_up, w_down, final_g):
    b, s, d = x.shape
    lc = ctx.shape[1]
    assert w_in.shape[0] == 1, "single-layer stack"
    assert b <= 7 and s % max(ROW_TM, ATT_TQ, ATT_TK, CMB_TT) == 0
    cap = EC_FACTOR * s // N_EXPERTS
    assert (b * cap) % FFN_T == 0 and cap % IDX_PC == 0

    cvec = jnp.zeros((8, d), F32).at[0:b].set(c).at[b].set(c_ctx)
    mod = _modulation(cvec, w_mod[0], b_mod[0][None, :])
    sh1, sc1, g1, sh2, sc2, g2 = [mod[0:b, None, i * d:(i + 1) * d] for i in range(6)]
    shc1, scc1 = mod[b:b + 1, None, 0:d], mod[b:b + 1, None, d:2 * d]

    n_main = 3 * CONV_WIDTH + Q_LORA + KV_LORA
    w_main = w_in[0, :, 0:n_main].astype(BF16)
    w_kr = w_in[0, :, n_main:]
    w_kr2 = jnp.concatenate([w_kr, w_kr[:, ROPE_SWAP]], axis=-1).astype(BF16)
    w_qn = w_uq[0, :, :, 0:QK_NOPE].reshape(Q_LORA, MLA_HEADS * QK_NOPE).astype(BF16)
    w_qr3 = w_uq[0, :, :, QK_NOPE:]
    w_qr = w_qr3.reshape(Q_LORA, MLA_HEADS * QK_ROPE).astype(BF16)
    w_qrs = w_qr3[:, :, ROPE_SWAP].reshape(Q_LORA, MLA_HEADS * QK_ROPE).astype(BF16)
    w_kn = w_ukv[0, :, :, 0:QK_NOPE].reshape(KV_LORA, MLA_HEADS * QK_NOPE).astype(BF16)
    w_v = w_ukv[0, :, :, QK_NOPE:].reshape(KV_LORA, MLA_HEADS * V_DIM).astype(BF16)
    w_o = w_out[0].astype(BF16)
    wr = jnp.zeros((d, LANES), F32).at[:, 0:N_EXPERTS].set(w_router[0])
    wr_hi = wr.astype(BF16)
    wr2 = jnp.concatenate([wr_hi, (wr - wr_hi.astype(F32)).astype(BF16)], axis=-1)
    wg, wu, wd = w_gate[0].astype(BF16), w_up[0].astype(BF16), w_down[0].astype(BF16)

    cos64, sin64 = _rope_tables(s)
    cos2 = jnp.asarray(np.concatenate([cos64, cos64], axis=-1))
    sin2 = jnp.asarray(np.concatenate([sin64, sin64], axis=-1))
    ktab = jnp.asarray(np.concatenate([cos64, sin64], axis=-1))

    n1 = norm1_g[0][None, :]
    qg, kvg = q_norm_g[0][None, :], kv_norm_g[0][None, :]
    u, bg, cqn, ckvn, kr2 = _inproj(x, sc1, sh1, n1, w_main, w_kr2, qg, kvg, with_conv=True, tm=ROW_TM)
    _, ckvn_c, kr2_c = _inproj(ctx, scc1, shc1, n1, w_main, w_kr2, qg, kvg, with_conv=False, tm=lc)

    q = _qproj(cqn, cos2, sin2, w_qn, w_qr, w_qrs, tm=ROW_TM)
    k, v = _kvproj(ckvn, kr2, ktab, w_kn, w_v, tm=ROW_TM)
    kc, vc = _kvproj(ckvn_c, kr2_c, None, w_kn, w_v, tm=lc)
    y_attn = _attention(q, k, v, kc, vc, tq=ATT_TQ, tk=ATT_TK)

    x1, tab, aff_t = _outproj(u, bg, conv_w[0], y_attn, w_o, x, g1, norm2_g[0][None, :], sc2, sh2, wr2, tm=ROW_TM)

    pos, offs = _routing(aff_t, cap=cap, blk=CMB_TT)
    nk1 = s // CMB_TT + 1
    offs_flat = jnp.swapaxes(offs[:, :, 0:nk1], 1, 2).reshape(-1)
    idx_t = _slots(offs_flat, pos, cap=cap, blk=CMB_TT, w=CMB_W)
    idx_flat = jnp.transpose(idx_t, (2, 0, 1)).reshape(-1)
    y = _expert_ffn(idx_flat, tab.reshape(b * s, d + LANES), wg, wu, wd, t=FFN_T)
    return _combine(offs_flat, pos, y, x1, g2, final_g[None, :], tt=CMB_TT, w=CMB_W)
```

```python
import functools
import math

import jax
import jax.numpy as jnp
import numpy as np
from jax import lax
from jax.experimental import pallas as pl
from jax.experimental.pallas import tpu as pltpu

F32 = jnp.float32
BF16 = jnp.bfloat16
I32 = jnp.int32

GRID_W = 64
CONV_WIDTH = 1024
MLA_HEADS = 8
QK_NOPE = 128
QK_ROPE = 64
V_DIM = 128
Q_LORA = 512
KV_LORA = 512
QK_DIM = QK_NOPE + QK_ROPE
MLA_WIDTH = MLA_HEADS * V_DIM
V_EXT = 2 * V_DIM
N_EXPERTS = 16
EC_FACTOR = 2
ROPE_THETA = 10000.0
ROPE_PAIRS = QK_ROPE // 4
ATTN_SCALE = 1.0 / math.sqrt(QK_DIM)
LOG2E = math.log2(math.e)
EPS = 1e-6

LANES = 128
VMEM_LIMIT = 56 * 1024 * 1024
FFN_VMEM_LIMIT = 60 * 1024 * 1024

MOD_TN = 1024
ROW_TM = 512
ATT_TQ = 512
ATT_TK = 512
FFN_T = 256
CMB_TT = 256
CMB_W = 64
IDX_PC = 64


def _dot(a, b):
    return jnp.dot(a, b, preferred_element_type=F32)


def _params(sem, vmem=VMEM_LIMIT):
    return pltpu.CompilerParams(dimension_semantics=sem, vmem_limit_bytes=vmem)


def _resident(shape):
    nd = len(shape)
    return pl.BlockSpec(shape, lambda *_: (0,) * nd, pipeline_mode=pl.Buffered(1))


def _mod_kernel(c_ref, w_ref, b_ref, o_ref):
    cv = c_ref[...]
    a = cv / (1.0 + jnp.exp(-cv))
    a_hi = a.astype(BF16)
    a_lo = (a - a_hi.astype(F32)).astype(BF16)
    w = w_ref[...]
    w_hi = w.astype(BF16)
    w_lo = (w - w_hi.astype(F32)).astype(BF16)
    o_ref[...] = _dot(a_hi, w_hi) + _dot(a_lo, w_hi) + _dot(a_hi, w_lo) + b_ref[...]


def _modulation(cvec, w_mod, b_mod):
    d, n = w_mod.shape
    return pl.pallas_call(
        _mod_kernel,
        grid=(n // MOD_TN,),
        in_specs=[pl.BlockSpec((8, d), lambda j: (0, 0)),
                  pl.BlockSpec((d, MOD_TN), lambda j: (0, j)),
                  pl.BlockSpec((1, MOD_TN), lambda j: (0, j))],
        out_specs=pl.BlockSpec((8, MOD_TN), lambda j: (0, j)),
        out_shape=jax.ShapeDtypeStruct((8, n), F32),
        compiler_params=_params(("arbitrary",)),
        name="modulation",
    )(cvec, w_mod, b_mod)


def _rms(v):
    return v * lax.rsqrt(jnp.mean(v * v, axis=-1, keepdims=True) + EPS)


def _inproj_kernel(x_ref, sc_ref, sh_ref, g_ref, w_ref, wkr_ref, qg_ref, kvg_ref, *outs, with_conv):
    h = _rms(x_ref[0]) * g_ref[...]
    h = h * (1.0 + sc_ref[0]) + sh_ref[0]
    hb = h.astype(BF16)
    c = CONV_WIDTH
    if with_conv:
        u_ref, bg_ref, cq_ref, ckv_ref, kr_ref = outs
        xin = _dot(hb, w_ref[:, 0:c])
        cg = _dot(hb, w_ref[:, 2 * c:3 * c])
        u_ref[0] = (cg * xin).astype(BF16)
        bg_ref[0] = _dot(hb, w_ref[:, c:2 * c]).astype(BF16)
    else:
        cq_ref, ckv_ref, kr_ref = outs
    o = 3 * c
    cq = _dot(hb, w_ref[:, o:o + Q_LORA])
    cq_ref[0] = (_rms(cq) * qg_ref[...]).astype(BF16)
    ckv = _dot(hb, w_ref[:, o + Q_LORA:o + Q_LORA + KV_LORA])
    ckv_ref[0] = (_rms(ckv) * kvg_ref[...]).astype(BF16)
    kr_ref[0] = _dot(hb, wkr_ref[...])


def _inproj(x, scale, shift, gain, w_main, w_kr2, q_g, kv_g, *, with_conv, tm):
    b, s, d = x.shape
    per_batch = scale.shape[0] > 1
    vec = pl.BlockSpec((1, 1, d), (lambda bi, i: (bi, 0, 0)) if per_batch else (lambda bi, i: (0, 0, 0)))
    row = lambda n: pl.BlockSpec((1, tm, n), lambda bi, i: (bi, i, 0))
    outs, specs = [], []
    if with_conv:
        outs += [jax.ShapeDtypeStruct((b, s, CONV_WIDTH), BF16)] * 2
        specs += [row(CONV_WIDTH)] * 2
    outs += [jax.ShapeDtypeStruct((b, s, Q_LORA), BF16), jax.ShapeDtypeStruct((b, s, KV_LORA), BF16),
             jax.ShapeDtypeStruct((b, s, 2 * QK_ROPE), F32)]
    specs += [row(Q_LORA), row(KV_LORA), row(2 * QK_ROPE)]
    return pl.pallas_call(
        functools.partial(_inproj_kernel, with_conv=with_conv),
        grid=(b, s // tm),
        in_specs=[row(d), vec, vec, _resident((1, d)), _resident(w_main.shape), _resident(w_kr2.shape),
                  _resident((1, Q_LORA)), _resident((1, KV_LORA))],
        out_specs=specs,
        out_shape=outs,
        compiler_params=_params(("arbitrary", "arbitrary")),
        name="inproj_conv" if with_conv else "inproj_ctx",
    )(x, scale, shift, gain, w_main, w_kr2, q_g, kv_g)


def _qproj_kernel(cq_ref, cos_ref, sin_ref, wn_ref, wr_ref, wrs_ref, q_ref):
    cq = cq_ref[0]
    qn = _dot(cq, wn_ref[...])
    qr = _dot(cq, wr_ref[...])
    qrs = _dot(cq, wrs_ref[...])
    reps = MLA_HEADS * QK_ROPE // LANES
    cos = jnp.concatenate([cos_ref[...]] * reps, axis=-1)
    sin = jnp.concatenate([sin_ref[...]] * reps, axis=-1)
    qrot = qr * cos + qrs * sin
    sc = ATTN_SCALE * LOG2E
    for h in range(MLA_HEADS):
        q_ref[0, h, :, 0:QK_NOPE] = (qn[:, h * QK_NOPE:(h + 1) * QK_NOPE] * sc).astype(BF16)
        q_ref[0, h, :, QK_NOPE:QK_DIM] = (qrot[:, h * QK_ROPE:(h + 1) * QK_ROPE] * sc).astype(BF16)


def _qproj(cqn, cos2, sin2, w_qn, w_qr, w_qrs, *, tm):
    b, s, r = cqn.shape
    return pl.pallas_call(
        _qproj_kernel,
        grid=(b, s // tm),
        in_specs=[pl.BlockSpec((1, tm, r), lambda bi, i: (bi, i, 0)),
                  pl.BlockSpec((tm, LANES), lambda bi, i: (i, 0)),
                  pl.BlockSpec((tm, LANES), lambda bi, i: (i, 0)),
                  _resident(w_qn.shape), _resident(w_qr.shape), _resident(w_qrs.shape)],
        out_specs=pl.BlockSpec((1, MLA_HEADS, tm, QK_DIM), lambda bi, i: (bi, 0, i, 0)),
        out_shape=jax.ShapeDtypeStruct((b, MLA_HEADS, s, QK_DIM), BF16),
        compiler_params=_params(("arbitrary", "arbitrary")),
        name="qproj",
    )(cqn, cos2, sin2, w_qn, w_qr, w_qrs)


def _kvproj_kernel(ckv_ref, kr_ref, *rest, rope):
    if rope:
        tab_ref, wk_ref, wv_ref, k_ref, v_ref = rest
        prod = kr_ref[0] * tab_ref[...]
        krot = prod[:, 0:QK_ROPE] + prod[:, QK_ROPE:2 * QK_ROPE]
    else:
        wk_ref, wv_ref, k_ref, v_ref = rest
        krot = kr_ref[0][:, 0:QK_ROPE]
    ckv = ckv_ref[0]
    kn = _dot(ckv, wk_ref[...])
    vv = _dot(ckv, wv_ref[...])
    krot = krot.astype(BF16)
    for h in range(MLA_HEADS):
        k_ref[0, h, :, 0:QK_NOPE] = kn[:, h * QK_NOPE:(h + 1) * QK_NOPE].astype(BF16)
        k_ref[0, h, :, QK_NOPE:QK_DIM] = krot
        v_ref[0, h, :, 0:V_DIM] = vv[:, h * V_DIM:(h + 1) * V_DIM].astype(BF16)
        v_ref[0, h, :, V_DIM:V_EXT] = jnp.ones((vv.shape[0], V_EXT - V_DIM), BF16)


def _kvproj(ckvn, kr2, tab, w_kn, w_v, *, tm):
    b, s, r = ckvn.shape
    rope = tab is not None
    ins = [ckvn, kr2]
    specs = [pl.BlockSpec((1, tm, r), lambda bi, i: (bi, i, 0)),
             pl.BlockSpec((1, tm, 2 * QK_ROPE), lambda bi, i: (bi, i, 0))]
    if rope:
        ins.append(tab)
        specs.append(pl.BlockSpec((tm, 2 * QK_ROPE), lambda bi, i: (i, 0)))
    ins += [w_kn, w_v]
    specs += [_resident(w_kn.shape), _resident(w_v.shape)]
    return pl.pallas_call(
        functools.partial(_kvproj_kernel, rope=rope),
        grid=(b, s // tm),
        in_specs=specs,
        out_specs=[pl.BlockSpec((1, MLA_HEADS, tm, QK_DIM), lambda bi, i: (bi, 0, i, 0)),
                   pl.BlockSpec((1, MLA_HEADS, tm, V_EXT), lambda bi, i: (bi, 0, i, 0))],
        out_shape=[jax.ShapeDtypeStruct((b, MLA_HEADS, s, QK_DIM), BF16),
                   jax.ShapeDtypeStruct((b, MLA_HEADS, s, V_EXT), BF16)],
        compiler_params=_params(("arbitrary", "arbitrary")),
        name="kvproj_rope" if rope else "kvproj_ctx",
    )(*ins)


def _attn_kernel(q_ref, k_ref, v_ref, kc_ref, vc_ref, o_ref, m_ref, acc_ref, sa_ref, sb_ref, *, tk):
    q = q_ref[0, 0]
    m_ref[...] = jnp.full(m_ref.shape, -jnp.inf, F32)
    acc_ref[...] = jnp.zeros(acc_ref.shape, F32)

    def lanes(a, n):
        return jnp.concatenate([a] * (n // LANES), axis=1)

    def scores(k):
        return lax.dot_general(q, k, (((1,), (1,)), ((), ())), preferred_element_type=F32)

    def update(s, v):
        m_old = m_ref[...]
        m_new = jnp.maximum(m_old, jnp.max(s, axis=-1, keepdims=True))
        alpha = jnp.exp2(m_old - m_new)
        p = jnp.exp2(s - lanes(m_new, s.shape[1]))
        acc_ref[...] = lanes(alpha, acc_ref.shape[1]) * acc_ref[...] + _dot(p.astype(BF16), v)
        m_ref[...] = m_new

    def chunk(ref, j):
        return ref[0, 0, pl.ds(pl.multiple_of(j * tk, tk), tk), :]

    n = k_ref.shape[2] // tk
    bufs = (sa_ref, sb_ref)
    sa_ref[...] = scores(chunk(k_ref, 0))
    for j in range(n):
        nxt = bufs[(j + 1) % 2]
        if j + 1 < n:
            nxt[...] = scores(chunk(k_ref, j + 1))
        else:
            s_ctx = scores(kc_ref[0, 0])
        update(bufs[j % 2][...], chunk(v_ref, j))
    update(s_ctx, vc_ref[0, 0])
    o_ref[0] = (acc_ref[:, 0:V_DIM] / acc_ref[:, V_DIM:V_EXT]).astype(BF16)


def _attention(q, k, v, kc, vc, *, tq, tk):
    b, h, s, dq = q.shape
    lc = kc.shape[2]
    return pl.pallas_call(
        functools.partial(_attn_kernel, tk=tk),
        grid=(b, h, s // tq),
        in_specs=[pl.BlockSpec((1, 1, tq, dq), lambda bi, hi, i: (bi, hi, i, 0)),
                  pl.BlockSpec((1, 1, s, dq), lambda bi, hi, i: (bi, hi, 0, 0)),
                  pl.BlockSpec((1, 1, s, V_EXT), lambda bi, hi, i: (bi, hi, 0, 0)),
                  pl.BlockSpec((1, 1, lc, dq), lambda bi, hi, i: (bi, hi, 0, 0)),
                  pl.BlockSpec((1, 1, lc, V_EXT), lambda bi, hi, i: (bi, hi, 0, 0))],
        out_specs=pl.BlockSpec((1, tq, V_DIM), lambda bi, hi, i: (bi, i, hi)),
        out_shape=jax.ShapeDtypeStruct((b, s, h * V_DIM), BF16),
        scratch_shapes=[pltpu.VMEM((tq, LANES), F32), pltpu.VMEM((tq, V_EXT), F32),
                        pltpu.VMEM((tq, tk), F32), pltpu.VMEM((tq, tk), F32)],
        compiler_params=_params(("arbitrary", "arbitrary", "arbitrary")),
        name="attention",
    )(q, k, v, kc, vc)


HALO = 16


def _outproj_kernel(u_ref, up_ref, un_ref, bg_ref, cw_ref, ya_ref, wo_ref, x_ref, g1_ref, n2_ref, sc_ref, sh_ref,
                    wr2_ref, x1_ref, tab_ref, afft_ref, *, tm, d):
    i = pl.program_id(1)
    last = pl.num_programs(1) - 1
    u = u_ref[0].astype(F32)
    rows = lax.broadcasted_iota(I32, (tm, 1), 0)
    prev_row = jnp.where(i > 0, up_ref[0][HALO - 1:HALO, :].astype(F32), 0.0)
    next_row = jnp.where(i < last, un_ref[0][0:1, :].astype(F32), 0.0)
    um1 = jnp.where(rows == 0, prev_row, pltpu.roll(u, 1, 0))
    up1 = jnp.where(rows == tm - 1, next_row, pltpu.roll(u, tm - 1, 0))
    cw = cw_ref[...]
    yc = (bg_ref[0].astype(F32) * (um1 * cw[0:1] + u * cw[1:2] + up1 * cw[2:3])).astype(BF16)
    nsub = 2
    tr = tm // nsub
    for r in range(nsub):
        rs = slice(r * tr, (r + 1) * tr)
        y = _dot(yc[rs], wo_ref[0:CONV_WIDTH, :]) + _dot(ya_ref[0, rs, :], wo_ref[CONV_WIDTH:, :])
        x1 = x_ref[0, rs, :] + g1_ref[0] * y
        x1_ref[0, rs, :] = x1
        h2 = _rms(x1) * n2_ref[...]
        h2 = h2 * (1.0 + sc_ref[0]) + sh_ref[0]
        tab_ref[0, rs, 0:d] = h2
        h_hi = h2.astype(BF16)
        h_lo = (h2 - h_hi.astype(F32)).astype(BF16)
        l2 = _dot(h_hi, wr2_ref[...])
        logits = l2[:, 0:LANES] + l2[:, LANES:2 * LANES] + _dot(h_lo, wr2_ref[:, 0:LANES])
        lane = lax.broadcasted_iota(I32, logits.shape, 1)
        logits = jnp.where(lane < N_EXPERTS, logits, -jnp.inf)
        e = jnp.exp(logits - jnp.max(logits, axis=-1, keepdims=True))
        aff = e / jnp.sum(e, axis=-1, keepdims=True)
        tab_ref[0, rs, d:d + LANES] = aff
        afft_ref[0, :, rs] = aff.T[0:N_EXPERTS, :]


def _outproj(u, bg, conv_w, y_attn, w_out, x, g1, n2, sc2, sh2, wr2, *, tm):
    b, s, d = x.shape
    nh = tm // HALO
    nhb = s // HALO
    row = lambda n: pl.BlockSpec((1, tm, n), lambda bi, i: (bi, i, 0))
    vec = pl.BlockSpec((1, 1, d), lambda bi, i: (bi, 0, 0))
    return pl.pallas_call(
        functools.partial(_outproj_kernel, tm=tm, d=d),
        grid=(b, s // tm),
        in_specs=[row(CONV_WIDTH),
                  pl.BlockSpec((1, HALO, CONV_WIDTH), lambda bi, i: (bi, jnp.maximum(i * nh - 1, 0), 0)),
                  pl.BlockSpec((1, HALO, CONV_WIDTH), lambda bi, i: (bi, jnp.minimum((i + 1) * nh, nhb - 1), 0)),
                  row(CONV_WIDTH), _resident(conv_w.shape), row(MLA_WIDTH), _resident(w_out.shape), row(d),
                  vec, _resident((1, d)), vec, vec, _resident(wr2.shape)],
        out_specs=[row(d), row(d + LANES), pl.BlockSpec((1, N_EXPERTS, tm), lambda bi, i: (bi, 0, i))],
        out_shape=[jax.ShapeDtypeStruct((b, s, d), F32), jax.ShapeDtypeStruct((b, s, d + LANES), F32),
                   jax.ShapeDtypeStruct((b, N_EXPERTS, s), F32)],
        compiler_params=_params(("arbitrary", "arbitrary")),
        name="outproj_router",
    )(u, u, u, bg, conv_w, y_attn, w_out, x, g1, n2, sc2, sh2, wr2)


def _routing_kernel(aff_ref, tri_ref, pos_ref, offs_ref, *, cap, blk, rows_per_expert):
    bi = pl.program_id(0)
    aff = aff_ref[0]
    ne, s = aff.shape
    capf = jnp.float32(cap)

    def bisect(t, prefix):
        cand = prefix | jnp.left_shift(jnp.int32(1), 30 - t)
        cnt = jnp.sum(jnp.where(aff >= lax.bitcast_convert_type(cand, F32), 1.0, 0.0), axis=1, keepdims=True)
        return jnp.where(cnt >= capf, cand, prefix)

    floor_bits = lax.fori_loop(0, 31, bisect, jnp.zeros((ne, 1), I32))
    thr = jnp.min(jnp.where(aff >= lax.bitcast_convert_type(floor_bits, F32), aff, jnp.inf), axis=1, keepdims=True)
    gt = aff > thr
    eq = aff == thr
    need = capf - jnp.sum(jnp.where(gt, 1.0, 0.0), axis=1, keepdims=True)
    tri = tri_ref[...]

    def cumsum_blocks(mask_f32):
        run = jnp.zeros((ne, 1), F32)
        parts, starts = [], []
        for kb in range(s // blk):
            c = _dot(mask_f32[:, kb * blk:(kb + 1) * blk].astype(BF16), tri)
            starts.append(run)
            parts.append(c + run)
            run = run + c[:, blk - 1:blk]
        return jnp.concatenate(parts, axis=1), starts, run

    eqf = jnp.where(eq, 1.0, 0.0)
    eq_incl, _, _ = cumsum_blocks(eqf)
    sel = jnp.where(gt, 1.0, jnp.where(eq & (eq_incl - eqf < need), 1.0, 0.0))
    incl, starts, _ = cumsum_blocks(sel)

    base = (lax.broadcasted_iota(I32, (ne, 1), 0) * rows_per_expert + bi * cap)
    pos_ref[0] = jnp.where(sel > 0.0, (incl - sel).astype(I32) + base, -1)
    lane = lax.broadcasted_iota(I32, (ne, LANES), 1)
    offs = jnp.full((ne, LANES), cap, I32) + base
    for kb, st in enumerate(starts):
        offs = jnp.where(lane == kb, st.astype(I32) + base, offs)
    offs_ref[0] = offs


def _routing(aff_t, *, cap, blk):
    b, ne, s = aff_t.shape
    assert s // blk + 1 <= LANES
    tri = jnp.asarray(np.triu(np.ones((blk, blk), np.float32)), BF16)
    return pl.pallas_call(
        functools.partial(_routing_kernel, cap=cap, blk=blk, rows_per_expert=b * cap),
        grid=(b,),
        in_specs=[pl.BlockSpec((1, ne, s), lambda bi: (bi, 0, 0)), _resident(tri.shape)],
        out_specs=[pl.BlockSpec((1, ne, s), lambda bi: (bi, 0, 0)),
                   pl.BlockSpec((1, ne, LANES), lambda bi: (bi, 0, 0))],
        out_shape=[jax.ShapeDtypeStruct((b, ne, s), I32), jax.ShapeDtypeStruct((b, ne, LANES), I32)],
        compiler_params=_params(("arbitrary",)),
        name="routing",
    )(aff_t, tri)


def _slots_kernel(offs_ref, pos_ref, idx_ref, *, cap, blk, w, rows_per_expert):
    bi = pl.program_id(0)
    ne, s = pos_ref.shape[1], pos_ref.shape[2]
    nb = s // blk
    idx_ref[...] = jnp.zeros(idx_ref.shape, I32)
    sub = lax.broadcasted_iota(I32, (w, blk), 0)
    tok = lax.broadcasted_iota(I32, (w, blk), 1)

    def block(kb, carry):
        tok1 = tok + (kb * blk + bi * s + 1)
        windows, trips = [], []
        for e in range(ne):
            base = e * rows_per_expert + bi * cap
            o = (bi * (nb + 1) + kb) * ne + e
            lo8 = (offs_ref[o] - base) // 8 * 8
            hi = offs_ref[o + ne] - base
            posrow = pos_ref[0, e:e + 1, pl.ds(pl.multiple_of(kb * blk, blk), blk)] - base

            def window(r, c, e=e, lo8=lo8, posrow=posrow):
                start = pl.multiple_of(jnp.minimum(lo8 + r * w, cap - w), 8)
                hit = jnp.where(sub + start == posrow, tok1, 0)
                val = jnp.sum(hit.astype(F32), axis=1, keepdims=True).astype(I32)
                old = idx_ref[0, pl.ds(start, w), e:e + 1]
                idx_ref[0, pl.ds(start, w), e:e + 1] = jnp.where(val > 0, val - 1, old)
                return c

            windows.append(window)
            trips.append((hi - lo8 + w - 1) // w)
        for window in windows:
            window(0, 0)
        for window, n in zip(windows, trips):
            lax.fori_loop(1, n, window, 0)
        return carry

    lax.fori_loop(0, nb, block, 0)


def _slots(offs_flat, pos, *, cap, blk, w):
    b, ne, s = pos.shape
    return pl.pallas_call(
        functools.partial(_slots_kernel, cap=cap, blk=blk, w=w, rows_per_expert=b * cap),
        grid_spec=pltpu.PrefetchScalarGridSpec(
            num_scalar_prefetch=1,
            grid=(b,),
            in_specs=[pl.BlockSpec((1, ne, s), lambda bi, o: (bi, 0, 0))],
            out_specs=pl.BlockSpec((1, cap, ne), lambda bi, o: (bi, 0, 0))),
        out_shape=jax.ShapeDtypeStruct((b, cap, ne), I32),
        compiler_params=_params(("arbitrary",)),
        name="slots",
    )(offs_flat, pos)


def _ffn_kernel(idx_ref, tab_hbm, wg_hbm, wu_hbm, wd_hbm, y_ref, buf_a, buf_b, wg_ref, wu_ref, wd_ref,
                sg_ref, su_ref, sd_ref, sem, wsem, *, t, d):
    e = pl.program_id(0)
    j = pl.program_id(1)
    ne, nt = pl.num_programs(0), pl.num_programs(1)
    step = e * nt + j
    nsteps = ne * nt
    rc, fc = sg_ref.shape[0], sd_ref.shape[0]

    def row_copy(base, i, buf, sl):
        return pltpu.make_async_copy(tab_hbm.at[pl.ds(idx_ref[base + i], 1)], buf.at[pl.ds(i, 1)], sem.at[sl])

    def wait_rows(buf, sl):
        pltpu.make_async_copy(tab_hbm.at[pl.ds(0, t)], buf, sem.at[sl]).wait()

    def piece_copies(ex, pc):
        return [pltpu.make_async_copy(wg_hbm.at[ex, pl.ds(pl.multiple_of(pc * rc, rc), rc)], sg_ref, wsem.at[0]),
                pltpu.make_async_copy(wu_hbm.at[ex, pl.ds(pl.multiple_of(pc * rc, rc), rc)], su_ref, wsem.at[1]),
                pltpu.make_async_copy(wd_hbm.at[ex, pl.ds(pl.multiple_of(pc * fc, fc), fc)], sd_ref, wsem.at[2])]

    def cast_piece(st, pc):
        wg_ref[st, pl.ds(pl.multiple_of(pc * rc, rc), rc), :] = sg_ref[...].astype(BF16)
        wu_ref[st, pl.ds(pl.multiple_of(pc * rc, rc), rc), :] = su_ref[...].astype(BF16)
        wd_ref[st, pl.ds(pl.multiple_of(pc * fc, fc), fc), :] = sd_ref[...].astype(BF16)

    @pl.when(step == 0)
    def _():
        def issue(i, carry):
            row_copy(0, i, buf_a, 0).start()
            return carry
        lax.fori_loop(0, t, issue, 0)

        def load(pc, carry):
            cps = piece_copies(0, pc)
            for cp in cps:
                cp.start()
            for cp in cps:
                cp.wait()
            cast_piece(0, pc)
            return carry
        lax.fori_loop(0, nt - 1, load, 0)
        for cp in piece_copies(0, nt - 1):
            cp.start()

    pe = jnp.where(j > 0, e, e - 1)
    pj = jnp.where(j > 0, j - 1, nt - 1)
    nxt_e = jnp.minimum(e + 1, ne - 1)

    def run(cur, cur_sl, nxt, nxt_sl):
        wait_rows(cur, cur_sl)
        for cp in piece_copies(jnp.minimum(pe + 1, ne - 1), pj):
            cp.wait()
        cast_piece((pe + 1) % 2, pj)
        for cp in piece_copies(nxt_e, j):
            cp.start()
        base = jnp.minimum(step + 1, nsteps - 1) * t
        for i in range(t):
            row_copy(base, i, nxt, nxt_sl).start()
        ws = e % 2
        xb = cur[:, 0:d].astype(BF16)
        aff = cur[:, d:d + LANES]
        lane = lax.broadcasted_iota(I32, aff.shape, 1)
        gate_w = jnp.sum(jnp.where(lane == e, aff, 0.0), axis=-1, keepdims=True)
        g = _dot(xb, wg_ref[ws])
        up = _dot(xb, wu_ref[ws])
        hid = (g / (1.0 + jnp.exp(-g))) * up
        y = _dot(hid.astype(BF16), wd_ref[ws])
        y_ref[...] = (y * gate_w).astype(BF16)

        @pl.when(step == nsteps - 1)
        def _():
            wait_rows(nxt, nxt_sl)
            for cp in piece_copies(nxt_e, j):
                cp.wait()

    @pl.when(step % 2 == 0)
    def _():
        run(buf_a, 0, buf_b, 1)

    @pl.when(step % 2 == 1)
    def _():
        run(buf_b, 1, buf_a, 0)


def _expert_ffn(idx_flat, tab, w_gate, w_up, w_down, *, t):
    ne, d, ff = w_gate.shape
    rows = idx_flat.shape[0]
    nt = rows // ne // t
    dw = tab.shape[1]
    bf16_rows = 16
    assert d % (nt * bf16_rows) == 0 and ff % (nt * bf16_rows) == 0
    return pl.pallas_call(
        functools.partial(_ffn_kernel, t=t, d=d),
        grid_spec=pltpu.PrefetchScalarGridSpec(
            num_scalar_prefetch=1,
            grid=(ne, nt),
            in_specs=[pl.BlockSpec(memory_space=pl.ANY)] * 4,
            out_specs=pl.BlockSpec((t, d), lambda e, j, idx: (e * nt + j, 0)),
            scratch_shapes=[pltpu.VMEM((t, dw), F32), pltpu.VMEM((t, dw), F32),
                            pltpu.VMEM((2, d, ff), BF16), pltpu.VMEM((2, d, ff), BF16), pltpu.VMEM((2, ff, d), BF16),
                            pltpu.VMEM((d // nt, ff), F32), pltpu.VMEM((d // nt, ff), F32),
                            pltpu.VMEM((ff // nt, d), F32),
                            pltpu.SemaphoreType.DMA((2,)), pltpu.SemaphoreType.DMA((3,))]),
        out_shape=jax.ShapeDtypeStruct((rows, d), BF16),
        compiler_params=_params(("arbitrary", "arbitrary"), vmem=FFN_VMEM_LIMIT),
        name="expert_ffn",
    )(idx_flat, tab, w_gate, w_up, w_down)


def _combine_kernel(offs_ref, pos_ref, y_hbm, x1_ref, g2_ref, fg_ref, o_ref, ybuf, acc_ref, sem, *, tt, w, total_rows):
    ne = pos_ref.shape[1]
    nk = pl.num_programs(1)
    step = pl.program_id(0) * nk + pl.program_id(1)
    nsteps = pl.num_programs(0) * nk
    slot = step % 2

    def bounds(st):
        o0 = (st // nk * (nk + 1) + st % nk) * ne
        lo8 = [(offs_ref[o0 + e] // 8) * 8 for e in range(ne)]
        hi = [offs_ref[o0 + ne + e] for e in range(ne)]
        return lo8, hi

    def window_copies(lo8, r, sl):
        starts = [pl.multiple_of(jnp.minimum(lo8[e] + r * w, total_rows - w), 8) for e in range(ne)]
        copies = [pltpu.make_async_copy(y_hbm.at[pl.ds(starts[e], w)], ybuf.at[sl, pl.ds(e * w, w)], sem.at[sl, e])
                  for e in range(ne)]
        return starts, copies

    @pl.when(step == 0)
    def _():
        for cp in window_copies(bounds(0)[0], 0, 0)[1]:
            cp.start()

    lo8, hi = bounds(step)
    starts0, copies0 = window_copies(lo8, 0, slot)
    for cp in copies0:
        cp.wait()

    @pl.when(step + 1 < nsteps)
    def _():
        for cp in window_copies(bounds(step + 1)[0], 0, 1 - slot)[1]:
            cp.start()

    pos = pos_ref[0]

    def contribution(r, starts):
        sub = lax.broadcasted_iota(I32, (w, tt), 0)
        want = [jnp.where(pos[e:e + 1, :] >= lo8[e] + r * w, pos[e:e + 1, :], -1) for e in range(ne)]
        blocks = [jnp.where(sub + starts[e] == want[e], 1.0, 0.0).astype(BF16) for e in range(ne)]
        onehot_t = jnp.concatenate(blocks, axis=0)
        return lax.dot_general(onehot_t, ybuf[slot], (((0,), (0,)), ((), ())), preferred_element_type=F32)

    acc_ref[...] = contribution(0, starts0)

    rounds = jnp.int32(1)
    for e in range(ne):
        rounds = jnp.maximum(rounds, (hi[e] - lo8[e] + w - 1) // w)

    def extra_round(r, carry):
        starts, copies = window_copies(lo8, r, slot)
        for cp in copies:
            cp.start()
        for cp in copies:
            cp.wait()
        acc_ref[...] += contribution(r, starts)
        return carry

    lax.fori_loop(1, rounds, extra_round, 0)
    x2 = x1_ref[0] + g2_ref[0] * acc_ref[...]
    o_ref[0] = _rms(x2) * fg_ref[...]


def _combine(offs_flat, pos, y, x1, g2, final_g, *, tt, w):
    b, s, d = x1.shape
    ne = pos.shape[1]
    total_rows = y.shape[0]
    return pl.pallas_call(
        functools.partial(_combine_kernel, tt=tt, w=w, total_rows=total_rows),
        grid_spec=pltpu.PrefetchScalarGridSpec(
            num_scalar_prefetch=1,
            grid=(b, s // tt),
            in_specs=[pl.BlockSpec((1, ne, tt), lambda bi, k, o: (bi, 0, k)),
                      pl.BlockSpec(memory_space=pl.ANY),
                      pl.BlockSpec((1, tt, d), lambda bi, k, o: (bi, k, 0)),
                      pl.BlockSpec((1, 1, d), lambda bi, k, o: (bi, 0, 0)),
                      pl.BlockSpec((1, d), lambda bi, k, o: (0, 0))],
            out_specs=pl.BlockSpec((1, tt, d), lambda bi, k, o: (bi, k, 0)),
            scratch_shapes=[pltpu.VMEM((2, ne * w, d), BF16), pltpu.VMEM((tt, d), F32),
                            pltpu.SemaphoreType.DMA((2, ne))]),
        out_shape=jax.ShapeDtypeStruct((b, s, d), F32),
        compiler_params=_params(("arbitrary", "arbitrary")),
        name="combine_final",
    )(offs_flat, pos, y, x1, g2, final_g)


def _rope_tables(s):
    n_rows = s // GRID_W
    row = np.repeat(np.arange(n_rows), GRID_W).astype(np.float64)
    col = np.tile(np.arange(GRID_W), n_rows).astype(np.float64)
    inv_freq = ROPE_THETA ** (-np.arange(ROPE_PAIRS, dtype=np.float64) / ROPE_PAIRS)
    ang_r = row[:, None] * inv_freq[None, :]
    ang_c = col[:, None] * inv_freq[None, :]
    cos64 = np.concatenate([np.cos(ang_r)] * 2 + [np.cos(ang_c)] * 2, axis=-1).astype(np.float32)
    sin64 = np.concatenate([-np.sin(ang_r), np.sin(ang_r), -np.sin(ang_c), np.sin(ang_c)], axis=-1).astype(np.float32)
    return cos64, sin64


ROPE_SWAP = np.concatenate([np.arange(16, 32), np.arange(0, 16), np.arange(48, 64), np.arange(32, 48)])


def kernel(x, c, ctx, c_ctx, w_mod, b_mod, norm1_g, norm2_g, w_in, conv_w, q_norm_g, w_uq, kv_norm_g, w_ukv, w_out,
           w_router, w_gate, w_up, w_down, final_g):
    b, s, d = x.shape
    lc = ctx.shape[1]
    assert w_in.shape[0] == 1, "single-layer stack"
    assert b <= 7 and s % max(ROW_TM, ATT_TQ, ATT_TK, CMB_TT) == 0
    cap = EC_FACTOR * s // N_EXPERTS
    assert (b * cap) % FFN_T == 0 and cap % IDX_PC == 0

    cvec = jnp.zeros((8, d), F32).at[0:b].set(c).at[b].set(c_ctx)
    mod = _modulation(cvec, w_mod[0], b_mod[0][None, :])
    sh1, sc1, g1, sh2, sc2, g2 = [mod[0:b, None, i * d:(i + 1) * d] for i in range(6)]
    shc1, scc1 = mod[b:b + 1, None, 0:d], mod[b:b + 1, None, d:2 * d]

    n_main = 3 * CONV_WIDTH + Q_LORA + KV_LORA
    w_main = w_in[0, :, 0:n_main].astype(BF16)
    w_kr = w_in[0, :, n_main:]
    w_kr2 = jnp.concatenate([w_kr, w_kr[:, ROPE_SWAP]], axis=-1).astype(BF16)
    w_qn = w_uq[0, :, :, 0:QK_NOPE].reshape(Q_LORA, MLA_HEADS * QK_NOPE).astype(BF16)
    w_qr3 = w_uq[0, :, :, QK_NOPE:]
    w_qr = w_qr3.reshape(Q_LORA, MLA_HEADS * QK_ROPE).astype(BF16)
    w_qrs = w_qr3[:, :, ROPE_SWAP].reshape(Q_LORA, MLA_HEADS * QK_ROPE).astype(BF16)
    w_kn = w_ukv[0, :, :, 0:QK_NOPE].reshape(KV_LORA, MLA_HEADS * QK_NOPE).astype(BF16)
    w_v = w_ukv[0, :, :, QK_NOPE:].reshape(KV_LORA, MLA_HEADS * V_DIM).astype(BF16)
    w_o = w_out[0].astype(BF16)
    wr = jnp.zeros((d, LANES), F32).at[:, 0:N_EXPERTS].set(w_router[0])
    wr_hi = wr.astype(BF16)
    wr2 = jnp.concatenate([wr_hi, (wr - wr_hi.astype(F32)).astype(BF16)], axis=-1)

    cos64, sin64 = _rope_tables(s)
    cos2 = jnp.asarray(np.concatenate([cos64, cos64], axis=-1))
    sin2 = jnp.asarray(np.concatenate([sin64, sin64], axis=-1))
    ktab = jnp.asarray(np.concatenate([cos64, sin64], axis=-1))

    n1 = norm1_g[0][None, :]
    qg, kvg = q_norm_g[0][None, :], kv_norm_g[0][None, :]
    u, bg, cqn, ckvn, kr2 = _inproj(x, sc1, sh1, n1, w_main, w_kr2, qg, kvg, with_conv=True, tm=ROW_TM)
    _, ckvn_c, kr2_c = _inproj(ctx, scc1, shc1, n1, w_main, w_kr2, qg, kvg, with_conv=False, tm=lc)

    q = _qproj(cqn, cos2, sin2, w_qn, w_qr, w_qrs, tm=ROW_TM)
    k, v = _kvproj(ckvn, kr2, ktab, w_kn, w_v, tm=ROW_TM)
    kc, vc = _kvproj(ckvn_c, kr2_c, None, w_kn, w_v, tm=lc)
    y_attn = _attention(q, k, v, kc, vc, tq=ATT_TQ, tk=ATT_TK)

    x1, tab, aff_t = _outproj(u, bg, conv_w[0], y_attn, w_o, x, g1, norm2_g[0][None, :], sc2, sh2, wr2, tm=ROW_TM)

    pos, offs = _routing(aff_t, cap=cap, blk=CMB_TT)
    nk1 = s // CMB_TT + 1
    offs_flat = jnp.swapaxes(offs[:, :, 0:nk1], 1, 2).reshape(-1)
    idx_t = _slots(offs_flat, pos, cap=cap, blk=CMB_TT, w=CMB_W)
    idx_flat = jnp.transpose(idx_t, (2, 0, 1)).reshape(-1)
    y = _expert_ffn(idx_flat, tab.reshape(b * s, d + LANES), w_gate[0], w_up[0], w_down[0], t=FFN_T)
    return _combine(offs_flat, pos, y, x1, g2, final_g[None, :], tt=CMB_TT, w=CMB_W)
```

```python
import functools
import math

import jax
import jax.numpy as jnp
import numpy as np
from jax import lax
from jax.experimental import pallas as pl
from jax.experimental.pallas import tpu as pltpu

F32 = jnp.float32
BF16 = jnp.bfloat16
I32 = jnp.int32

GRID_W = 64
CONV_WIDTH = 1024
MLA_HEADS = 8
QK_NOPE = 128
QK_ROPE = 64
V_DIM = 128
Q_LORA = 512
KV_LORA = 512
QK_DIM = QK_NOPE + QK_ROPE
MLA_WIDTH = MLA_HEADS * V_DIM
V_EXT = 2 * V_DIM
N_EXPERTS = 16
EC_FACTOR = 2
ROPE_THETA = 10000.0
ROPE_PAIRS = QK_ROPE // 4
ATTN_SCALE = 1.0 / math.sqrt(QK_DIM)
LOG2E = math.log2(math.e)
EPS = 1e-6

LANES = 128
VMEM_LIMIT = 56 * 1024 * 1024
FFN_VMEM_LIMIT = 60 * 1024 * 1024

MOD_TN = 1024
ROW_TM = 512
ATT_TQ = 512
ATT_TK = 512
FFN_T = 256
CMB_TT = 256
CMB_W = 64
IDX_PC = 64


def _dot(a, b):
    return jnp.dot(a, b, preferred_element_type=F32)


def _params(sem, vmem=VMEM_LIMIT):
    return pltpu.CompilerParams(dimension_semantics=sem, vmem_limit_bytes=vmem)


def _resident(shape):
    nd = len(shape)
    return pl.BlockSpec(shape, lambda *_: (0,) * nd, pipeline_mode=pl.Buffered(1))


def _mod_kernel(c_ref, w_ref, b_ref, o_ref):
    cv = c_ref[...]
    a = cv / (1.0 + jnp.exp(-cv))
    a_hi = a.astype(BF16)
    a_lo = (a - a_hi.astype(F32)).astype(BF16)
    w = w_ref[...]
    w_hi = w.astype(BF16)
    w_lo = (w - w_hi.astype(F32)).astype(BF16)
    o_ref[...] = _dot(a_hi, w_hi) + _dot(a_lo, w_hi) + _dot(a_hi, w_lo) + b_ref[...]


def _modulation(cvec, w_mod, b_mod):
    d, n = w_mod.shape
    return pl.pallas_call(
        _mod_kernel,
        grid=(n // MOD_TN,),
        in_specs=[pl.BlockSpec((8, d), lambda j: (0, 0)),
                  pl.BlockSpec((d, MOD_TN), lambda j: (0, j)),
                  pl.BlockSpec((1, MOD_TN), lambda j: (0, j))],
        out_specs=pl.BlockSpec((8, MOD_TN), lambda j: (0, j)),
        out_shape=jax.ShapeDtypeStruct((8, n), F32),
        compiler_params=_params(("arbitrary",)),
        name="modulation",
    )(cvec, w_mod, b_mod)


def _rms(v):
    return v * lax.rsqrt(jnp.mean(v * v, axis=-1, keepdims=True) + EPS)


def _inproj_kernel(x_ref, sc_ref, sh_ref, g_ref, w_ref, wkr_ref, qg_ref, kvg_ref, *outs, with_conv):
    h = _rms(x_ref[0]) * g_ref[...]
    h = h * (1.0 + sc_ref[0]) + sh_ref[0]
    hb = h.astype(BF16)
    c = CONV_WIDTH
    if with_conv:
        u_ref, bg_ref, cq_ref, ckv_ref, kr_ref = outs
        xin = _dot(hb, w_ref[:, 0:c])
        cg = _dot(hb, w_ref[:, 2 * c:3 * c])
        u_ref[0] = (cg * xin).astype(BF16)
        bg_ref[0] = _dot(hb, w_ref[:, c:2 * c]).astype(BF16)
    else:
        cq_ref, ckv_ref, kr_ref = outs
    o = 3 * c
    cq = _dot(hb, w_ref[:, o:o + Q_LORA])
    cq_ref[0] = (_rms(cq) * qg_ref[...]).astype(BF16)
    ckv = _dot(hb, w_ref[:, o + Q_LORA:o + Q_LORA + KV_LORA])
    ckv_ref[0] = (_rms(ckv) * kvg_ref[...]).astype(BF16)
    kr_ref[0] = _dot(hb, wkr_ref[...])


def _inproj(x, scale, shift, gain, w_main, w_kr2, q_g, kv_g, *, with_conv, tm):
    b, s, d = x.shape
    per_batch = scale.shape[0] > 1
    vec = pl.BlockSpec((1, 1, d), (lambda bi, i: (bi, 0, 0)) if per_batch else (lambda bi, i: (0, 0, 0)))
    row = lambda n: pl.BlockSpec((1, tm, n), lambda bi, i: (bi, i, 0))
    outs, specs = [], []
    if with_conv:
        outs += [jax.ShapeDtypeStruct((b, s, CONV_WIDTH), BF16)] * 2
        specs += [row(CONV_WIDTH)] * 2
    outs += [jax.ShapeDtypeStruct((b, s, Q_LORA), BF16), jax.ShapeDtypeStruct((b, s, KV_LORA), BF16),
             jax.ShapeDtypeStruct((b, s, 2 * QK_ROPE), F32)]
    specs += [row(Q_LORA), row(KV_LORA), row(2 * QK_ROPE)]
    return pl.pallas_call(
        functools.partial(_inproj_kernel, with_conv=with_conv),
        grid=(b, s // tm),
        in_specs=[row(d), vec, vec, _resident((1, d)), _resident(w_main.shape), _resident(w_kr2.shape),
                  _resident((1, Q_LORA)), _resident((1, KV_LORA))],
        out_specs=specs,
        out_shape=outs,
        compiler_params=_params(("arbitrary", "arbitrary")),
        name="inproj_conv" if with_conv else "inproj_ctx",
    )(x, scale, shift, gain, w_main, w_kr2, q_g, kv_g)


def _qproj_kernel(cq_ref, cos_ref, sin_ref, wn_ref, wr_ref, wrs_ref, q_ref):
    cq = cq_ref[0]
    qn = _dot(cq, wn_ref[...])
    qr = _dot(cq, wr_ref[...])
    qrs = _dot(cq, wrs_ref[...])
    reps = MLA_HEADS * QK_ROPE // LANES
    cos = jnp.concatenate([cos_ref[...]] * reps, axis=-1)
    sin = jnp.concatenate([sin_ref[...]] * reps, axis=-1)
    qrot = qr * cos + qrs * sin
    sc = ATTN_SCALE * LOG2E
    for h in range(MLA_HEADS):
        q_ref[0, h, :, 0:QK_NOPE] = (qn[:, h * QK_NOPE:(h + 1) * QK_NOPE] * sc).astype(BF16)
        q_ref[0, h, :, QK_NOPE:QK_DIM] = (qrot[:, h * QK_ROPE:(h + 1) * QK_ROPE] * sc).astype(BF16)


def _qproj(cqn, cos2, sin2, w_qn, w_qr, w_qrs, *, tm):
    b, s, r = cqn.shape
    return pl.pallas_call(
        _qproj_kernel,
        grid=(b, s // tm),
        in_specs=[pl.BlockSpec((1, tm, r), lambda bi, i: (bi, i, 0)),
                  pl.BlockSpec((tm, LANES), lambda bi, i: (i, 0)),
                  pl.BlockSpec((tm, LANES), lambda bi, i: (i, 0)),
                  _resident(w_qn.shape), _resident(w_qr.shape), _resident(w_qrs.shape)],
        out_specs=pl.BlockSpec((1, MLA_HEADS, tm, QK_DIM), lambda bi, i: (bi, 0, i, 0)),
        out_shape=jax.ShapeDtypeStruct((b, MLA_HEADS, s, QK_DIM), BF16),
        compiler_params=_params(("arbitrary", "arbitrary")),
        name="qproj",
    )(cqn, cos2, sin2, w_qn, w_qr, w_qrs)


def _kvproj_kernel(ckv_ref, kr_ref, *rest, rope):
    if rope:
        tab_ref, wk_ref, wv_ref, k_ref, v_ref = rest
        prod = kr_ref[0] * tab_ref[...]
        krot = prod[:, 0:QK_ROPE] + prod[:, QK_ROPE:2 * QK_ROPE]
    else:
        wk_ref, wv_ref, k_ref, v_ref = rest
        krot = kr_ref[0][:, 0:QK_ROPE]
    ckv = ckv_ref[0]
    kn = _dot(ckv, wk_ref[...])
    vv = _dot(ckv, wv_ref[...])
    krot = krot.astype(BF16)
    for h in range(MLA_HEADS):
        k_ref[0, h, :, 0:QK_NOPE] = kn[:, h * QK_NOPE:(h + 1) * QK_NOPE].astype(BF16)
        k_ref[0, h, :, QK_NOPE:QK_DIM] = krot
        v_ref[0, h, :, 0:V_DIM] = vv[:, h * V_DIM:(h + 1) * V_DIM].astype(BF16)
        v_ref[0, h, :, V_DIM:V_EXT] = jnp.ones((vv.shape[0], V_EXT - V_DIM), BF16)


def _kvproj(ckvn, kr2, tab, w_kn, w_v, *, tm):
    b, s, r = ckvn.shape
    rope = tab is not None
    ins = [ckvn, kr2]
    specs = [pl.BlockSpec((1, tm, r), lambda bi, i: (bi, i, 0)),
             pl.BlockSpec((1, tm, 2 * QK_ROPE), lambda bi, i: (bi, i, 0))]
    if rope:
        ins.append(tab)
        specs.append(pl.BlockSpec((tm, 2 * QK_ROPE), lambda bi, i: (i, 0)))
    ins += [w_kn, w_v]
    specs += [_resident(w_kn.shape), _resident(w_v.shape)]
    return pl.pallas_call(
        functools.partial(_kvproj_kernel, rope=rope),
        grid=(b, s // tm),
        in_specs=specs,
        out_specs=[pl.BlockSpec((1, MLA_HEADS, tm, QK_DIM), lambda bi, i: (bi, 0, i, 0)),
                   pl.BlockSpec((1, MLA_HEADS, tm, V_EXT), lambda bi, i: (bi, 0, i, 0))],
        out_shape=[jax.ShapeDtypeStruct((b, MLA_HEADS, s, QK_DIM), BF16),
                   jax.ShapeDtypeStruct((b, MLA_HEADS, s, V_EXT), BF16)],
        compiler_params=_params(("arbitrary", "arbitrary")),
        name="kvproj_rope" if rope else "kvproj_ctx",
    )(*ins)


def _attn_kernel(q_ref, k_ref, v_ref, kc_ref, vc_ref, o_ref, m_ref, acc_ref, sa_ref, sb_ref, *, tk):
    q = q_ref[0, 0]
    m_ref[...] = jnp.full(m_ref.shape, -jnp.inf, F32)
    acc_ref[...] = jnp.zeros(acc_ref.shape, F32)

    def lanes(a, n):
        return jnp.concatenate([a] * (n // LANES), axis=1)

    def scores(k):
        return lax.dot_general(q, k, (((1,), (1,)), ((), ())), preferred_element_type=F32)

    def update(s, v):
        m_old = m_ref[...]
        m_new = jnp.maximum(m_old, jnp.max(s, axis=-1, keepdims=True))
        alpha = jnp.exp2(m_old - m_new)
        p = jnp.exp2(s - lanes(m_new, s.shape[1]))
        acc_ref[...] = lanes(alpha, acc_ref.shape[1]) * acc_ref[...] + _dot(p.astype(BF16), v)
        m_ref[...] = m_new

    def chunk(ref, j):
        return ref[0, 0, pl.ds(pl.multiple_of(j * tk, tk), tk), :]

    n = k_ref.shape[2] // tk
    bufs = (sa_ref, sb_ref)
    sa_ref[...] = scores(chunk(k_ref, 0))
    for j in range(n):
        nxt = bufs[(j + 1) % 2]
        if j + 1 < n:
            nxt[...] = scores(chunk(k_ref, j + 1))
        else:
            s_ctx = scores(kc_ref[0, 0])
        update(bufs[j % 2][...], chunk(v_ref, j))
    update(s_ctx, vc_ref[0, 0])
    o_ref[0] = (acc_ref[:, 0:V_DIM] / acc_ref[:, V_DIM:V_EXT]).astype(BF16)


def _attention(q, k, v, kc, vc, *, tq, tk):
    b, h, s, dq = q.shape
    lc = kc.shape[2]
    return pl.pallas_call(
        functools.partial(_attn_kernel, tk=tk),
        grid=(b, h, s // tq),
        in_specs=[pl.BlockSpec((1, 1, tq, dq), lambda bi, hi, i: (bi, hi, i, 0)),
                  pl.BlockSpec((1, 1, s, dq), lambda bi, hi, i: (bi, hi, 0, 0)),
                  pl.BlockSpec((1, 1, s, V_EXT), lambda bi, hi, i: (bi, hi, 0, 0)),
                  pl.BlockSpec((1, 1, lc, dq), lambda bi, hi, i: (bi, hi, 0, 0)),
                  pl.BlockSpec((1, 1, lc, V_EXT), lambda bi, hi, i: (bi, hi, 0, 0))],
        out_specs=pl.BlockSpec((1, tq, V_DIM), lambda bi, hi, i: (bi, i, hi)),
        out_shape=jax.ShapeDtypeStruct((b, s, h * V_DIM), BF16),
        scratch_shapes=[pltpu.VMEM((tq, LANES), F32), pltpu.VMEM((tq, V_EXT), F32),
                        pltpu.VMEM((tq, tk), F32), pltpu.VMEM((tq, tk), F32)],
        compiler_params=_params(("arbitrary", "arbitrary", "arbitrary")),
        name="attention",
    )(q, k, v, kc, vc)


HALO = 16


def _outproj_kernel(u_ref, up_ref, un_ref, bg_ref, cw_ref, ya_ref, wo_ref, x_ref, g1_ref, n2_ref, sc_ref, sh_ref,
                    wr2_ref, x1_ref, tab_ref, afft_ref, *, tm, d):
    i = pl.program_id(1)
    last = pl.num_programs(1) - 1
    u = u_ref[0].astype(F32)
    rows = lax.broadcasted_iota(I32, (tm, 1), 0)
    prev_row = jnp.where(i > 0, up_ref[0][HALO - 1:HALO, :].astype(F32), 0.0)
    next_row = jnp.where(i < last, un_ref[0][0:1, :].astype(F32), 0.0)
    um1 = jnp.where(rows == 0, prev_row, pltpu.roll(u, 1, 0))
    up1 = jnp.where(rows == tm - 1, next_row, pltpu.roll(u, tm - 1, 0))
    cw = cw_ref[...]
    yc = (bg_ref[0].astype(F32) * (um1 * cw[0:1] + u * cw[1:2] + up1 * cw[2:3])).astype(BF16)
    nsub = 2
    tr = tm // nsub
    for r in range(nsub):
        rs = slice(r * tr, (r + 1) * tr)
        y = _dot(yc[rs], wo_ref[0:CONV_WIDTH, :]) + _dot(ya_ref[0, rs, :], wo_ref[CONV_WIDTH:, :])
        x1 = x_ref[0, rs, :] + g1_ref[0] * y
        x1_ref[0, rs, :] = x1
        h2 = _rms(x1) * n2_ref[...]
        h2 = h2 * (1.0 + sc_ref[0]) + sh_ref[0]
        tab_ref[0, rs, 0:d] = h2
        h_hi = h2.astype(BF16)
        h_lo = (h2 - h_hi.astype(F32)).astype(BF16)
        l2 = _dot(h_hi, wr2_ref[...])
        logits = l2[:, 0:LANES] + l2[:, LANES:2 * LANES] + _dot(h_lo, wr2_ref[:, 0:LANES])
        lane = lax.broadcasted_iota(I32, logits.shape, 1)
        logits = jnp.where(lane < N_EXPERTS, logits, -jnp.inf)
        e = jnp.exp(logits - jnp.max(logits, axis=-1, keepdims=True))
        aff = e / jnp.sum(e, axis=-1, keepdims=True)
        tab_ref[0, rs, d:d + LANES] = aff
        afft_ref[0, :, rs] = aff.T[0:N_EXPERTS, :]


def _outproj(u, bg, conv_w, y_attn, w_out, x, g1, n2, sc2, sh2, wr2, *, tm):
    b, s, d = x.shape
    nh = tm // HALO
    nhb = s // HALO
    row = lambda n: pl.BlockSpec((1, tm, n), lambda bi, i: (bi, i, 0))
    vec = pl.BlockSpec((1, 1, d), lambda bi, i: (bi, 0, 0))
    return pl.pallas_call(
        functools.partial(_outproj_kernel, tm=tm, d=d),
        grid=(b, s // tm),
        in_specs=[row(CONV_WIDTH),
                  pl.BlockSpec((1, HALO, CONV_WIDTH), lambda bi, i: (bi, jnp.maximum(i * nh - 1, 0), 0)),
                  pl.BlockSpec((1, HALO, CONV_WIDTH), lambda bi, i: (bi, jnp.minimum((i + 1) * nh, nhb - 1), 0)),
                  row(CONV_WIDTH), _resident(conv_w.shape), row(MLA_WIDTH), _resident(w_out.shape), row(d),
                  vec, _resident((1, d)), vec, vec, _resident(wr2.shape)],
        out_specs=[row(d), row(d + LANES), pl.BlockSpec((1, N_EXPERTS, tm), lambda bi, i: (bi, 0, i))],
        out_shape=[jax.ShapeDtypeStruct((b, s, d), F32), jax.ShapeDtypeStruct((b, s, d + LANES), F32),
                   jax.ShapeDtypeStruct((b, N_EXPERTS, s), F32)],
        compiler_params=_params(("arbitrary", "arbitrary")),
        name="outproj_router",
    )(u, u, u, bg, conv_w, y_attn, w_out, x, g1, n2, sc2, sh2, wr2)


def _routing_kernel(aff_ref, tri_ref, pos_ref, offs_ref, *, cap, blk, rows_per_expert):
    bi = pl.program_id(0)
    aff = aff_ref[0]
    ne, s = aff.shape
    capf = jnp.float32(cap)

    def bisect(t, prefix):
        cand = prefix | jnp.left_shift(jnp.int32(1), 30 - t)
        cnt = jnp.sum(jnp.where(aff >= lax.bitcast_convert_type(cand, F32), 1.0, 0.0), axis=1, keepdims=True)
        return jnp.where(cnt >= capf, cand, prefix)

    floor_bits = lax.fori_loop(0, 31, bisect, jnp.zeros((ne, 1), I32))
    thr = jnp.min(jnp.where(aff >= lax.bitcast_convert_type(floor_bits, F32), aff, jnp.inf), axis=1, keepdims=True)
    gt = aff > thr
    eq = aff == thr
    need = capf - jnp.sum(jnp.where(gt, 1.0, 0.0), axis=1, keepdims=True)
    tri = tri_ref[...]

    def cumsum_blocks(mask_f32):
        run = jnp.zeros((ne, 1), F32)
        parts, starts = [], []
        for kb in range(s // blk):
            c = _dot(mask_f32[:, kb * blk:(kb + 1) * blk].astype(BF16), tri)
            starts.append(run)
            parts.append(c + run)
            run = run + c[:, blk - 1:blk]
        return jnp.concatenate(parts, axis=1), starts, run

    eqf = jnp.where(eq, 1.0, 0.0)
    eq_incl, _, _ = cumsum_blocks(eqf)
    sel = jnp.where(gt, 1.0, jnp.where(eq & (eq_incl - eqf < need), 1.0, 0.0))
    incl, starts, _ = cumsum_blocks(sel)

    base = (lax.broadcasted_iota(I32, (ne, 1), 0) * rows_per_expert + bi * cap)
    pos_ref[0] = jnp.where(sel > 0.0, (incl - sel).astype(I32) + base, -1)
    lane = lax.broadcasted_iota(I32, (ne, LANES), 1)
    offs = jnp.full((ne, LANES), cap, I32) + base
    for kb, st in enumerate(starts):
        offs = jnp.where(lane == kb, st.astype(I32) + base, offs)
    offs_ref[0] = offs


def _routing(aff_t, *, cap, blk):
    b, ne, s = aff_t.shape
    assert s // blk + 1 <= LANES
    tri = jnp.asarray(np.triu(np.ones((blk, blk), np.float32)), BF16)
    return pl.pallas_call(
        functools.partial(_routing_kernel, cap=cap, blk=blk, rows_per_expert=b * cap),
        grid=(b,),
        in_specs=[pl.BlockSpec((1, ne, s), lambda bi: (bi, 0, 0)), _resident(tri.shape)],
        out_specs=[pl.BlockSpec((1, ne, s), lambda bi: (bi, 0, 0)),
                   pl.BlockSpec((1, ne, LANES), lambda bi: (bi, 0, 0))],
        out_shape=[jax.ShapeDtypeStruct((b, ne, s), I32), jax.ShapeDtypeStruct((b, ne, LANES), I32)],
        compiler_params=_params(("arbitrary",)),
        name="routing",
    )(aff_t, tri)


def _slots_kernel(offs_ref, pos_ref, idx_ref, *, cap, blk, w, rows_per_expert):
    bi = pl.program_id(0)
    ne, s = pos_ref.shape[1], pos_ref.shape[2]
    nb = s // blk
    idx_ref[...] = jnp.zeros(idx_ref.shape, I32)
    sub = lax.broadcasted_iota(I32, (w, blk), 0)
    tok = lax.broadcasted_iota(I32, (w, blk), 1)

    def block(kb, carry):
        tok1 = tok + (kb * blk + bi * s + 1)
        windows, trips = [], []
        for e in range(ne):
            base = e * rows_per_expert + bi * cap
            o = (bi * (nb + 1) + kb) * ne + e
            lo8 = (offs_ref[o] - base) // 8 * 8
            hi = offs_ref[o + ne] - base
            posrow = pos_ref[0, e:e + 1, pl.ds(pl.multiple_of(kb * blk, blk), blk)] - base

            def window(r, c, e=e, lo8=lo8, posrow=posrow):
                start = pl.multiple_of(jnp.minimum(lo8 + r * w, cap - w), 8)
                hit = jnp.where(sub + start == posrow, tok1, 0)
                val = jnp.sum(hit.astype(F32), axis=1, keepdims=True).astype(I32)
                old = idx_ref[0, pl.ds(start, w), e:e + 1]
                idx_ref[0, pl.ds(start, w), e:e + 1] = jnp.where(val > 0, val - 1, old)
                return c

            windows.append(window)
            trips.append((hi - lo8 + w - 1) // w)
        for window in windows:
            window(0, 0)
        for window, n in zip(windows, trips):
            lax.fori_loop(1, n, window, 0)
        return carry

    lax.fori_loop(0, nb, block, 0)


def _slots(offs_flat, pos, *, cap, blk, w):
    b, ne, s = pos.shape
    return pl.pallas_call(
        functools.partial(_slots_kernel, cap=cap, blk=blk, w=w, rows_per_expert=b * cap),
        grid_spec=pltpu.PrefetchScalarGridSpec(
            num_scalar_prefetch=1,
            grid=(b,),
            in_specs=[pl.BlockSpec((1, ne, s), lambda bi, o: (bi, 0, 0))],
            out_specs=pl.BlockSpec((1, cap, ne), lambda bi, o: (bi, 0, 0))),
        out_shape=jax.ShapeDtypeStruct((b, cap, ne), I32),
        compiler_params=_params(("arbitrary",)),
        name="slots",
    )(offs_flat, pos)


def _ffn_kernel(idx_ref, tab_hbm, wg_hbm, wu_hbm, wd_hbm, y_ref, buf_a, buf_b, wg_ref, wu_ref, wd_ref,
                sg_ref, su_ref, sd_ref, sem, wsem, *, t, d):
    e = pl.program_id(0)
    j = pl.program_id(1)
    ne, nt = pl.num_programs(0), pl.num_programs(1)
    step = e * nt + j
    nsteps = ne * nt
    rc, fc = sg_ref.shape[0], sd_ref.shape[0]

    def row_copy(base, i, buf, sl):
        return pltpu.make_async_copy(tab_hbm.at[pl.ds(idx_ref[base + i], 1)], buf.at[pl.ds(i, 1)], sem.at[sl])

    def wait_rows(buf, sl):
        pltpu.make_async_copy(tab_hbm.at[pl.ds(0, t)], buf, sem.at[sl]).wait()

    def piece_copies(ex, pc):
        return [pltpu.make_async_copy(wg_hbm.at[ex, pl.ds(pl.multiple_of(pc * rc, rc), rc)], sg_ref, wsem.at[0]),
                pltpu.make_async_copy(wu_hbm.at[ex, pl.ds(pl.multiple_of(pc * rc, rc), rc)], su_ref, wsem.at[1]),
                pltpu.make_async_copy(wd_hbm.at[ex, pl.ds(pl.multiple_of(pc * fc, fc), fc)], sd_ref, wsem.at[2])]

    def cast_piece(st, pc):
        wg_ref[st, pl.ds(pl.multiple_of(pc * rc, rc), rc), :] = sg_ref[...].astype(BF16)
        wu_ref[st, pl.ds(pl.multiple_of(pc * rc, rc), rc), :] = su_ref[...].astype(BF16)
        wd_ref[st, pl.ds(pl.multiple_of(pc * fc, fc), fc), :] = sd_ref[...].astype(BF16)

    @pl.when(step == 0)
    def _():
        def issue(i, carry):
            row_copy(0, i, buf_a, 0).start()
            return carry
        lax.fori_loop(0, t, issue, 0)

        def load(pc, carry):
            cps = piece_copies(0, pc)
            for cp in cps:
                cp.start()
            for cp in cps:
                cp.wait()
            cast_piece(0, pc)
            return carry
        lax.fori_loop(0, nt, load, 0)

    nxt_e = jnp.minimum(e + 1, ne - 1)
    nxt_set = (e + 1) % 2
    ws = e % 2

    def take_piece(pc):
        for cp in piece_copies(nxt_e, pc):
            cp.wait()
        cast_piece(nxt_set, pc)

    def run(cur, cur_sl, nxt, nxt_sl):
        wait_rows(cur, cur_sl)
        base = jnp.minimum(step + 1, nsteps - 1) * t
        for i in range(t):
            row_copy(base, i, nxt, nxt_sl).start()

        @pl.when(idx_ref[base] < 0)
        def _():
            y_ref[0:16, 0:LANES] = jnp.zeros((16, LANES), BF16)

        xb = cur[:, 0:d].astype(BF16)
        aff = cur[:, d:d + LANES]
        lane = lax.broadcasted_iota(I32, aff.shape, 1)
        gate_w = jnp.sum(jnp.where(lane == e, aff, 0.0), axis=-1, keepdims=True)
        g = _dot(xb, wg_ref[ws])
        up = _dot(xb, wu_ref[ws])
        hid = ((g / (1.0 + jnp.exp(-g))) * up).astype(BF16)

        @pl.when(j > 0)
        def _():
            take_piece(j - 1)

        for cp in piece_copies(nxt_e, j):
            cp.start()
        y = _dot(hid, wd_ref[ws])
        y_ref[...] = (y * gate_w).astype(BF16)

        @pl.when(j == nt - 1)
        def _():
            take_piece(nt - 1)

        @pl.when(step == nsteps - 1)
        def _():
            wait_rows(nxt, nxt_sl)

    @pl.when(step % 2 == 0)
    def _():
        run(buf_a, 0, buf_b, 1)

    @pl.when(step % 2 == 1)
    def _():
        run(buf_b, 1, buf_a, 0)


def _expert_ffn(idx_flat, tab, w_gate, w_up, w_down, *, t):
    ne, d, ff = w_gate.shape
    rows = idx_flat.shape[0]
    nt = rows // ne // t
    dw = tab.shape[1]
    bf16_rows = 16
    assert d % (nt * bf16_rows) == 0 and ff % (nt * bf16_rows) == 0
    return pl.pallas_call(
        functools.partial(_ffn_kernel, t=t, d=d),
        grid_spec=pltpu.PrefetchScalarGridSpec(
            num_scalar_prefetch=1,
            grid=(ne, nt),
            in_specs=[pl.BlockSpec(memory_space=pl.ANY)] * 4,
            out_specs=pl.BlockSpec((t, d), lambda e, j, idx: (e * nt + j, 0)),
            scratch_shapes=[pltpu.VMEM((t, dw), F32), pltpu.VMEM((t, dw), F32),
                            pltpu.VMEM((2, d, ff), BF16), pltpu.VMEM((2, d, ff), BF16), pltpu.VMEM((2, ff, d), BF16),
                            pltpu.VMEM((d // nt, ff), F32), pltpu.VMEM((d // nt, ff), F32),
                            pltpu.VMEM((ff // nt, d), F32),
                            pltpu.SemaphoreType.DMA((2,)), pltpu.SemaphoreType.DMA((3,))]),
        out_shape=jax.ShapeDtypeStruct((rows, d), BF16),
        compiler_params=_params(("arbitrary", "arbitrary"), vmem=FFN_VMEM_LIMIT),
        name="expert_ffn",
    )(idx_flat, tab, w_gate, w_up, w_down)


def _combine_kernel(offs_ref, pos_ref, y_hbm, x1_ref, g2_ref, fg_ref, o_ref, ybuf, acc_ref, sem, *, tt, w, total_rows):
    ne = pos_ref.shape[1]
    nk = pl.num_programs(1)
    step = pl.program_id(0) * nk + pl.program_id(1)
    nsteps = pl.num_programs(0) * nk
    slot = step % 2

    def bounds(st):
        o0 = (st // nk * (nk + 1) + st % nk) * ne
        lo8 = [(offs_ref[o0 + e] // 8) * 8 for e in range(ne)]
        hi = [offs_ref[o0 + ne + e] for e in range(ne)]
        return lo8, hi

    def window_copies(lo8, r, sl):
        starts = [pl.multiple_of(jnp.minimum(lo8[e] + r * w, total_rows - w), 8) for e in range(ne)]
        copies = [pltpu.make_async_copy(y_hbm.at[pl.ds(starts[e], w)], ybuf.at[sl, pl.ds(e * w, w)], sem.at[sl, e])
                  for e in range(ne)]
        return starts, copies

    @pl.when(step == 0)
    def _():
        for cp in window_copies(bounds(0)[0], 0, 0)[1]:
            cp.start()

    lo8, hi = bounds(step)
    starts0, copies0 = window_copies(lo8, 0, slot)
    for cp in copies0:
        cp.wait()

    @pl.when(step + 1 < nsteps)
    def _():
        for cp in window_copies(bounds(step + 1)[0], 0, 1 - slot)[1]:
            cp.start()

    pos = pos_ref[0]

    def contribution(r, starts):
        sub = lax.broadcasted_iota(I32, (w, tt), 0)
        want = [jnp.where(pos[e:e + 1, :] >= lo8[e] + r * w, pos[e:e + 1, :], -1) for e in range(ne)]
        blocks = [jnp.where(sub + starts[e] == want[e], 1.0, 0.0).astype(BF16) for e in range(ne)]
        onehot_t = jnp.concatenate(blocks, axis=0)
        return lax.dot_general(onehot_t, ybuf[slot], (((0,), (0,)), ((), ())), preferred_element_type=F32)

    acc_ref[...] = contribution(0, starts0)

    rounds = jnp.int32(1)
    for e in range(ne):
        rounds = jnp.maximum(rounds, (hi[e] - lo8[e] + w - 1) // w)

    def extra_round(r, carry):
        starts, copies = window_copies(lo8, r, slot)
        for cp in copies:
            cp.start()
        for cp in copies:
            cp.wait()
        acc_ref[...] += contribution(r, starts)
        return carry

    lax.fori_loop(1, rounds, extra_round, 0)
    x2 = x1_ref[0] + g2_ref[0] * acc_ref[...]
    o_ref[0] = _rms(x2) * fg_ref[...]


def _combine(offs_flat, pos, y, x1, g2, final_g, *, tt, w):
    b, s, d = x1.shape
    ne = pos.shape[1]
    total_rows = y.shape[0]
    return pl.pallas_call(
        functools.partial(_combine_kernel, tt=tt, w=w, total_rows=total_rows),
        grid_spec=pltpu.PrefetchScalarGridSpec(
            num_scalar_prefetch=1,
            grid=(b, s // tt),
            in_specs=[pl.BlockSpec((1, ne, tt), lambda bi, k, o: (bi, 0, k)),
                      pl.BlockSpec(memory_space=pl.ANY),
                      pl.BlockSpec((1, tt, d), lambda bi, k, o: (bi, k, 0)),
                      pl.BlockSpec((1, 1, d), lambda bi, k, o: (bi, 0, 0)),
                      pl.BlockSpec((1, d), lambda bi, k, o: (0, 0))],
            out_specs=pl.BlockSpec((1, tt, d), lambda bi, k, o: (bi, k, 0)),
            scratch_shapes=[pltpu.VMEM((2, ne * w, d), BF16), pltpu.VMEM((tt, d), F32),
                            pltpu.SemaphoreType.DMA((2, ne))]),
        out_shape=jax.ShapeDtypeStruct((b, s, d), F32),
        compiler_params=_params(("arbitrary", "arbitrary")),
        name="combine_final",
    )(offs_flat, pos, y, x1, g2, final_g)


def _rope_tables(s):
    n_rows = s // GRID_W
    row = np.repeat(np.arange(n_rows), GRID_W).astype(np.float64)
    col = np.tile(np.arange(GRID_W), n_rows).astype(np.float64)
    inv_freq = ROPE_THETA ** (-np.arange(ROPE_PAIRS, dtype=np.float64) / ROPE_PAIRS)
    ang_r = row[:, None] * inv_freq[None, :]
    ang_c = col[:, None] * inv_freq[None, :]
    cos64 = np.concatenate([np.cos(ang_r)] * 2 + [np.cos(ang_c)] * 2, axis=-1).astype(np.float32)
    sin64 = np.concatenate([-np.sin(ang_r), np.sin(ang_r), -np.sin(ang_c), np.sin(ang_c)], axis=-1).astype(np.float32)
    return cos64, sin64


ROPE_SWAP = np.concatenate([np.arange(16, 32), np.arange(0, 16), np.arange(48, 64), np.arange(32, 48)])


def kernel(x, c, ctx, c_ctx, w_mod, b_mod, norm1_g, norm2_g, w_in, conv_w, q_norm_g, w_uq, kv_norm_g, w_ukv, w_out,
           w_router, w_gate, w_up, w_down, final_g):
    b, s, d = x.shape
    lc = ctx.shape[1]
    assert w_in.shape[0] == 1, "single-layer stack"
    assert b <= 7 and s % max(ROW_TM, ATT_TQ, ATT_TK, CMB_TT) == 0
    cap = EC_FACTOR * s // N_EXPERTS
    assert (b * cap) % FFN_T == 0 and cap % IDX_PC == 0

    cvec = jnp.zeros((8, d), F32).at[0:b].set(c).at[b].set(c_ctx)
    mod = _modulation(cvec, w_mod[0], b_mod[0][None, :])
    sh1, sc1, g1, sh2, sc2, g2 = [mod[0:b, None, i * d:(i + 1) * d] for i in range(6)]
    shc1, scc1 = mod[b:b + 1, None, 0:d], mod[b:b + 1, None, d:2 * d]

    n_main = 3 * CONV_WIDTH + Q_LORA + KV_LORA
    w_main = w_in[0, :, 0:n_main].astype(BF16)
    w_kr = w_in[0, :, n_main:]
    w_kr2 = jnp.concatenate([w_kr, w_kr[:, ROPE_SWAP]], axis=-1).astype(BF16)
    w_qn = w_uq[0, :, :, 0:QK_NOPE].reshape(Q_LORA, MLA_HEADS * QK_NOPE).astype(BF16)
    w_qr3 = w_uq[0, :, :, QK_NOPE:]
    w_qr = w_qr3.reshape(Q_LORA, MLA_HEADS * QK_ROPE).astype(BF16)
    w_qrs = w_qr3[:, :, ROPE_SWAP].reshape(Q_LORA, MLA_HEADS * QK_ROPE).astype(BF16)
    w_kn = w_ukv[0, :, :, 0:QK_NOPE].reshape(KV_LORA, MLA_HEADS * QK_NOPE).astype(BF16)
    w_v = w_ukv[0, :, :, QK_NOPE:].reshape(KV_LORA, MLA_HEADS * V_DIM).astype(BF16)
    w_o = w_out[0].astype(BF16)
    wr = jnp.zeros((d, LANES), F32).at[:, 0:N_EXPERTS].set(w_router[0])
    wr_hi = wr.astype(BF16)
    wr2 = jnp.concatenate([wr_hi, (wr - wr_hi.astype(F32)).astype(BF16)], axis=-1)

    cos64, sin64 = _rope_tables(s)
    cos2 = jnp.asarray(np.concatenate([cos64, cos64], axis=-1))
    sin2 = jnp.asarray(np.concatenate([sin64, sin64], axis=-1))
    ktab = jnp.asarray(np.concatenate([cos64, sin64], axis=-1))

    n1 = norm1_g[0][None, :]
    qg, kvg = q_norm_g[0][None, :], kv_norm_g[0][None, :]
    u, bg, cqn, ckvn, kr2 = _inproj(x, sc1, sh1, n1, w_main, w_kr2, qg, kvg, with_conv=True, tm=ROW_TM)
    _, ckvn_c, kr2_c = _inproj(ctx, scc1, shc1, n1, w_main, w_kr2, qg, kvg, with_conv=False, tm=lc)

    q = _qproj(cqn, cos2, sin2, w_qn, w_qr, w_qrs, tm=ROW_TM)
    k, v = _kvproj(ckvn, kr2, ktab, w_kn, w_v, tm=ROW_TM)
    kc, vc = _kvproj(ckvn_c, kr2_c, None, w_kn, w_v, tm=lc)
    y_attn = _attention(q, k, v, kc, vc, tq=ATT_TQ, tk=ATT_TK)

    x1, tab, aff_t = _outproj(u, bg, conv_w[0], y_attn, w_o, x, g1, norm2_g[0][None, :], sc2, sh2, wr2, tm=ROW_TM)

    pos, offs = _routing(aff_t, cap=cap, blk=CMB_TT)
    nk1 = s // CMB_TT + 1
    offs_flat = jnp.swapaxes(offs[:, :, 0:nk1], 1, 2).reshape(-1)
    idx_t = _slots(offs_flat, pos, cap=cap, blk=CMB_TT, w=CMB_W)
    idx_flat = jnp.transpose(idx_t, (2, 0, 1)).reshape(-1)
    y = _expert_ffn(idx_flat, tab.reshape(b * s, d + LANES), w_gate[0], w_up[0], w_down[0], t=FFN_T)
    return _combine(offs_flat, pos, y, x1, g2, final_g[None, :], tt=CMB_TT, w=CMB_W)
```

```python
import functools
import math

import jax
import jax.numpy as jnp
import numpy as np
from jax import lax
from jax.experimental import pallas as pl
from jax.experimental.pallas import tpu as pltpu

F32 = jnp.float32
BF16 = jnp.bfloat16
I32 = jnp.int32

GRID_W = 64
CONV_WIDTH = 1024
MLA_HEADS = 8
QK_NOPE = 128
QK_ROPE = 64
V_DIM = 128
Q_LORA = 512
KV_LORA = 512
QK_DIM = QK_NOPE + QK_ROPE
MLA_WIDTH = MLA_HEADS * V_DIM
V_EXT = 2 * V_DIM
N_EXPERTS = 16
EC_FACTOR = 2
ROPE_THETA = 10000.0
ROPE_PAIRS = QK_ROPE // 4
ATTN_SCALE = 1.0 / math.sqrt(QK_DIM)
LOG2E = math.log2(math.e)
EPS = 1e-6

LANES = 128
VMEM_LIMIT = 56 * 1024 * 1024
FFN_VMEM_LIMIT = 60 * 1024 * 1024

MOD_TN = 1024
ROW_TM = 512
ATT_TQ = 1024
ATT_TK = 512
FFN_T = 256
CMB_TT = 256
CMB_W = 64
IDX_PC = 64


def _dot(a, b):
    return jnp.dot(a, b, preferred_element_type=F32)


def _params(sem, vmem=VMEM_LIMIT):
    return pltpu.CompilerParams(dimension_semantics=sem, vmem_limit_bytes=vmem)


def _resident(shape):
    nd = len(shape)
    return pl.BlockSpec(shape, lambda *_: (0,) * nd, pipeline_mode=pl.Buffered(1))


def _mod_kernel(c_ref, w_ref, b_ref, o_ref):
    cv = c_ref[...]
    a = cv / (1.0 + jnp.exp(-cv))
    a_hi = a.astype(BF16)
    a_lo = (a - a_hi.astype(F32)).astype(BF16)
    w = w_ref[...]
    w_hi = w.astype(BF16)
    w_lo = (w - w_hi.astype(F32)).astype(BF16)
    o_ref[...] = _dot(a_hi, w_hi) + _dot(a_lo, w_hi) + _dot(a_hi, w_lo) + b_ref[...]


def _modulation(cvec, w_mod, b_mod):
    d, n = w_mod.shape
    return pl.pallas_call(
        _mod_kernel,
        grid=(n // MOD_TN,),
        in_specs=[pl.BlockSpec((8, d), lambda j: (0, 0)),
                  pl.BlockSpec((d, MOD_TN), lambda j: (0, j)),
                  pl.BlockSpec((1, MOD_TN), lambda j: (0, j))],
        out_specs=pl.BlockSpec((8, MOD_TN), lambda j: (0, j)),
        out_shape=jax.ShapeDtypeStruct((8, n), F32),
        compiler_params=_params(("arbitrary",)),
        name="modulation",
    )(cvec, w_mod, b_mod)


def _rms(v):
    return v * lax.rsqrt(jnp.mean(v * v, axis=-1, keepdims=True) + EPS)


def _inproj_kernel(x_ref, sc_ref, sh_ref, g_ref, w_ref, wkr_ref, qg_ref, kvg_ref, *outs, with_conv):
    h = _rms(x_ref[0]) * g_ref[...]
    h = h * (1.0 + sc_ref[0]) + sh_ref[0]
    hb = h.astype(BF16)
    c = CONV_WIDTH
    if with_conv:
        u_ref, bg_ref, cq_ref, ckv_ref, kr_ref = outs
        xin = _dot(hb, w_ref[:, 0:c])
        cg = _dot(hb, w_ref[:, 2 * c:3 * c])
        u_ref[0] = (cg * xin).astype(BF16)
        bg_ref[0] = _dot(hb, w_ref[:, c:2 * c]).astype(BF16)
    else:
        cq_ref, ckv_ref, kr_ref = outs
    o = 3 * c
    cq = _dot(hb, w_ref[:, o:o + Q_LORA])
    cq_ref[0] = (_rms(cq) * qg_ref[...]).astype(BF16)
    ckv = _dot(hb, w_ref[:, o + Q_LORA:o + Q_LORA + KV_LORA])
    ckv_ref[0] = (_rms(ckv) * kvg_ref[...]).astype(BF16)
    kr_ref[0] = _dot(hb, wkr_ref[...])


def _inproj(x, scale, shift, gain, w_main, w_kr2, q_g, kv_g, *, with_conv, tm):
    b, s, d = x.shape
    per_batch = scale.shape[0] > 1
    vec = pl.BlockSpec((1, 1, d), (lambda bi, i: (bi, 0, 0)) if per_batch else (lambda bi, i: (0, 0, 0)))
    row = lambda n: pl.BlockSpec((1, tm, n), lambda bi, i: (bi, i, 0))
    outs, specs = [], []
    if with_conv:
        outs += [jax.ShapeDtypeStruct((b, s, CONV_WIDTH), BF16)] * 2
        specs += [row(CONV_WIDTH)] * 2
    outs += [jax.ShapeDtypeStruct((b, s, Q_LORA), BF16), jax.ShapeDtypeStruct((b, s, KV_LORA), BF16),
             jax.ShapeDtypeStruct((b, s, 2 * QK_ROPE), F32)]
    specs += [row(Q_LORA), row(KV_LORA), row(2 * QK_ROPE)]
    return pl.pallas_call(
        functools.partial(_inproj_kernel, with_conv=with_conv),
        grid=(b, s // tm),
        in_specs=[row(d), vec, vec, _resident((1, d)), _resident(w_main.shape), _resident(w_kr2.shape),
                  _resident((1, Q_LORA)), _resident((1, KV_LORA))],
        out_specs=specs,
        out_shape=outs,
        compiler_params=_params(("arbitrary", "arbitrary")),
        name="inproj_conv" if with_conv else "inproj_ctx",
    )(x, scale, shift, gain, w_main, w_kr2, q_g, kv_g)


def _qproj_kernel(cq_ref, cos_ref, sin_ref, wn_ref, wr_ref, wrs_ref, q_ref):
    cq = cq_ref[0]
    qn = _dot(cq, wn_ref[...])
    qr = _dot(cq, wr_ref[...])
    qrs = _dot(cq, wrs_ref[...])
    reps = MLA_HEADS * QK_ROPE // LANES
    cos = jnp.concatenate([cos_ref[...]] * reps, axis=-1)
    sin = jnp.concatenate([sin_ref[...]] * reps, axis=-1)
    qrot = qr * cos + qrs * sin
    sc = ATTN_SCALE * LOG2E
    for h in range(MLA_HEADS):
        q_ref[0, h, :, 0:QK_NOPE] = (qn[:, h * QK_NOPE:(h + 1) * QK_NOPE] * sc).astype(BF16)
        q_ref[0, h, :, QK_NOPE:QK_DIM] = (qrot[:, h * QK_ROPE:(h + 1) * QK_ROPE] * sc).astype(BF16)


def _qproj(cqn, cos2, sin2, w_qn, w_qr, w_qrs, *, tm):
    b, s, r = cqn.shape
    return pl.pallas_call(
        _qproj_kernel,
        grid=(b, s // tm),
        in_specs=[pl.BlockSpec((1, tm, r), lambda bi, i: (bi, i, 0)),
                  pl.BlockSpec((tm, LANES), lambda bi, i: (i, 0)),
                  pl.BlockSpec((tm, LANES), lambda bi, i: (i, 0)),
                  _resident(w_qn.shape), _resident(w_qr.shape), _resident(w_qrs.shape)],
        out_specs=pl.BlockSpec((1, MLA_HEADS, tm, QK_DIM), lambda bi, i: (bi, 0, i, 0)),
        out_shape=jax.ShapeDtypeStruct((b, MLA_HEADS, s, QK_DIM), BF16),
        compiler_params=_params(("arbitrary", "arbitrary")),
        name="qproj",
    )(cqn, cos2, sin2, w_qn, w_qr, w_qrs)


def _kvproj_kernel(ckv_ref, kr_ref, *rest, rope):
    if rope:
        tab_ref, wk_ref, wv_ref, k_ref, v_ref = rest
        prod = kr_ref[0] * tab_ref[...]
        krot = prod[:, 0:QK_ROPE] + prod[:, QK_ROPE:2 * QK_ROPE]
    else:
        wk_ref, wv_ref, k_ref, v_ref = rest
        krot = kr_ref[0][:, 0:QK_ROPE]
    ckv = ckv_ref[0]
    kn = _dot(ckv, wk_ref[...])
    vv = _dot(ckv, wv_ref[...])
    krot = krot.astype(BF16)
    for h in range(MLA_HEADS):
        k_ref[0, h, :, 0:QK_NOPE] = kn[:, h * QK_NOPE:(h + 1) * QK_NOPE].astype(BF16)
        k_ref[0, h, :, QK_NOPE:QK_DIM] = krot
        v_ref[0, h, :, 0:V_DIM] = vv[:, h * V_DIM:(h + 1) * V_DIM].astype(BF16)
        v_ref[0, h, :, V_DIM:V_EXT] = jnp.ones((vv.shape[0], V_EXT - V_DIM), BF16)


def _kvproj(ckvn, kr2, tab, w_kn, w_v, *, tm):
    b, s, r = ckvn.shape
    rope = tab is not None
    ins = [ckvn, kr2]
    specs = [pl.BlockSpec((1, tm, r), lambda bi, i: (bi, i, 0)),
             pl.BlockSpec((1, tm, 2 * QK_ROPE), lambda bi, i: (bi, i, 0))]
    if rope:
        ins.append(tab)
        specs.append(pl.BlockSpec((tm, 2 * QK_ROPE), lambda bi, i: (i, 0)))
    ins += [w_kn, w_v]
    specs += [_resident(w_kn.shape), _resident(w_v.shape)]
    return pl.pallas_call(
        functools.partial(_kvproj_kernel, rope=rope),
        grid=(b, s // tm),
        in_specs=specs,
        out_specs=[pl.BlockSpec((1, MLA_HEADS, tm, QK_DIM), lambda bi, i: (bi, 0, i, 0)),
                   pl.BlockSpec((1, MLA_HEADS, tm, V_EXT), lambda bi, i: (bi, 0, i, 0))],
        out_shape=[jax.ShapeDtypeStruct((b, MLA_HEADS, s, QK_DIM), BF16),
                   jax.ShapeDtypeStruct((b, MLA_HEADS, s, V_EXT), BF16)],
        compiler_params=_params(("arbitrary", "arbitrary")),
        name="kvproj_rope" if rope else "kvproj_ctx",
    )(*ins)


def _attn_kernel(q_ref, k_ref, v_ref, kc_ref, vc_ref, o_ref, m_ref, acc_ref, sa_ref, sb_ref, *, tk):
    q = q_ref[0, 0]
    m_ref[...] = jnp.full(m_ref.shape, -jnp.inf, F32)
    acc_ref[...] = jnp.zeros(acc_ref.shape, F32)

    def lanes(a, n):
        return jnp.concatenate([a] * (n // LANES), axis=1)

    def scores(k):
        return lax.dot_general(q, k, (((1,), (1,)), ((), ())), preferred_element_type=F32)

    def update(s, v):
        m_old = m_ref[...]
        m_new = jnp.maximum(m_old, jnp.max(s, axis=-1, keepdims=True))
        alpha = jnp.exp2(m_old - m_new)
        p = jnp.exp2(s - lanes(m_new, s.shape[1]))
        acc_ref[...] = lanes(alpha, acc_ref.shape[1]) * acc_ref[...] + _dot(p.astype(BF16), v)
        m_ref[...] = m_new

    def chunk(ref, j):
        return ref[0, 0, pl.ds(pl.multiple_of(j * tk, tk), tk), :]

    n = k_ref.shape[2] // tk
    bufs = (sa_ref, sb_ref)
    sa_ref[...] = scores(chunk(k_ref, 0))
    for j in range(n):
        nxt = bufs[(j + 1) % 2]
        if j + 1 < n:
            nxt[...] = scores(chunk(k_ref, j + 1))
        else:
            s_ctx = scores(kc_ref[0, 0])
        update(bufs[j % 2][...], chunk(v_ref, j))
    update(s_ctx, vc_ref[0, 0])
    o_ref[0] = (acc_ref[:, 0:V_DIM] / acc_ref[:, V_DIM:V_EXT]).astype(BF16)


def _attention(q, k, v, kc, vc, *, tq, tk):
    b, h, s, dq = q.shape
    lc = kc.shape[2]
    return pl.pallas_call(
        functools.partial(_attn_kernel, tk=tk),
        grid=(b, h, s // tq),
        in_specs=[pl.BlockSpec((1, 1, tq, dq), lambda bi, hi, i: (bi, hi, i, 0)),
                  pl.BlockSpec((1, 1, s, dq), lambda bi, hi, i: (bi, hi, 0, 0)),
                  pl.BlockSpec((1, 1, s, V_EXT), lambda bi, hi, i: (bi, hi, 0, 0)),
                  pl.BlockSpec((1, 1, lc, dq), lambda bi, hi, i: (bi, hi, 0, 0)),
                  pl.BlockSpec((1, 1, lc, V_EXT), lambda bi, hi, i: (bi, hi, 0, 0))],
        out_specs=pl.BlockSpec((1, tq, V_DIM), lambda bi, hi, i: (bi, i, hi)),
        out_shape=jax.ShapeDtypeStruct((b, s, h * V_DIM), BF16),
        scratch_shapes=[pltpu.VMEM((tq, LANES), F32), pltpu.VMEM((tq, V_EXT), F32),
                        pltpu.VMEM((tq, tk), F32), pltpu.VMEM((tq, tk), F32)],
        compiler_params=_params(("arbitrary", "arbitrary", "arbitrary")),
        name="attention",
    )(q, k, v, kc, vc)


HALO = 16


def _outproj_kernel(u_ref, up_ref, un_ref, bg_ref, cw_ref, ya_ref, wo_ref, x_ref, g1_ref, n2_ref, sc_ref, sh_ref,
                    wr2_ref, x1_ref, tab_ref, afft_ref, *, tm, d):
    i = pl.program_id(1)
    last = pl.num_programs(1) - 1
    u = u_ref[0].astype(F32)
    rows = lax.broadcasted_iota(I32, (tm, 1), 0)
    prev_row = jnp.where(i > 0, up_ref[0][HALO - 1:HALO, :].astype(F32), 0.0)
    next_row = jnp.where(i < last, un_ref[0][0:1, :].astype(F32), 0.0)
    um1 = jnp.where(rows == 0, prev_row, pltpu.roll(u, 1, 0))
    up1 = jnp.where(rows == tm - 1, next_row, pltpu.roll(u, tm - 1, 0))
    cw = cw_ref[...]
    yc = (bg_ref[0].astype(F32) * (um1 * cw[0:1] + u * cw[1:2] + up1 * cw[2:3])).astype(BF16)
    nsub = 2
    tr = tm // nsub
    for r in range(nsub):
        rs = slice(r * tr, (r + 1) * tr)
        y = _dot(yc[rs], wo_ref[0:CONV_WIDTH, :]) + _dot(ya_ref[0, rs, :], wo_ref[CONV_WIDTH:, :])
        x1 = x_ref[0, rs, :] + g1_ref[0] * y
        x1_ref[0, rs, :] = x1
        h2 = _rms(x1) * n2_ref[...]
        h2 = h2 * (1.0 + sc_ref[0]) + sh_ref[0]
        tab_ref[0, rs, 0:d] = h2
        h_hi = h2.astype(BF16)
        h_lo = (h2 - h_hi.astype(F32)).astype(BF16)
        l2 = _dot(h_hi, wr2_ref[...])
        logits = l2[:, 0:LANES] + l2[:, LANES:2 * LANES] + _dot(h_lo, wr2_ref[:, 0:LANES])
        lane = lax.broadcasted_iota(I32, logits.shape, 1)
        logits = jnp.where(lane < N_EXPERTS, logits, -jnp.inf)
        e = jnp.exp(logits - jnp.max(logits, axis=-1, keepdims=True))
        aff = e / jnp.sum(e, axis=-1, keepdims=True)
        tab_ref[0, rs, d:d + LANES] = aff
        afft_ref[0, :, rs] = aff.T[0:N_EXPERTS, :]


def _outproj(u, bg, conv_w, y_attn, w_out, x, g1, n2, sc2, sh2, wr2, *, tm):
    b, s, d = x.shape
    nh = tm // HALO
    nhb = s // HALO
    row = lambda n: pl.BlockSpec((1, tm, n), lambda bi, i: (bi, i, 0))
    vec = pl.BlockSpec((1, 1, d), lambda bi, i: (bi, 0, 0))
    return pl.pallas_call(
        functools.partial(_outproj_kernel, tm=tm, d=d),
        grid=(b, s // tm),
        in_specs=[row(CONV_WIDTH),
                  pl.BlockSpec((1, HALO, CONV_WIDTH), lambda bi, i: (bi, jnp.maximum(i * nh - 1, 0), 0)),
                  pl.BlockSpec((1, HALO, CONV_WIDTH), lambda bi, i: (bi, jnp.minimum((i + 1) * nh, nhb - 1), 0)),
                  row(CONV_WIDTH), _resident(conv_w.shape), row(MLA_WIDTH), _resident(w_out.shape), row(d),
                  vec, _resident((1, d)), vec, vec, _resident(wr2.shape)],
        out_specs=[row(d), row(d + LANES), pl.BlockSpec((1, N_EXPERTS, tm), lambda bi, i: (bi, 0, i))],
        out_shape=[jax.ShapeDtypeStruct((b, s, d), F32), jax.ShapeDtypeStruct((b, s, d + LANES), F32),
                   jax.ShapeDtypeStruct((b, N_EXPERTS, s), F32)],
        compiler_params=_params(("arbitrary", "arbitrary")),
        name="outproj_router",
    )(u, u, u, bg, conv_w, y_attn, w_out, x, g1, n2, sc2, sh2, wr2)


def _routing_kernel(aff_ref, tri_ref, pos_ref, offs_ref, *, cap, blk, rows_per_expert):
    bi = pl.program_id(0)
    aff = aff_ref[0]
    ne, s = aff.shape
    capf = jnp.float32(cap)

    def bisect(t, prefix):
        cand = prefix | jnp.left_shift(jnp.int32(1), 30 - t)
        cnt = jnp.sum(jnp.where(aff >= lax.bitcast_convert_type(cand, F32), 1.0, 0.0), axis=1, keepdims=True)
        return jnp.where(cnt >= capf, cand, prefix)

    floor_bits = lax.fori_loop(0, 31, bisect, jnp.zeros((ne, 1), I32))
    thr = jnp.min(jnp.where(aff >= lax.bitcast_convert_type(floor_bits, F32), aff, jnp.inf), axis=1, keepdims=True)
    gt = aff > thr
    eq = aff == thr
    need = capf - jnp.sum(jnp.where(gt, 1.0, 0.0), axis=1, keepdims=True)
    tri = tri_ref[...]

    def cumsum_blocks(mask_f32):
        run = jnp.zeros((ne, 1), F32)
        parts, starts = [], []
        for kb in range(s // blk):
            c = _dot(mask_f32[:, kb * blk:(kb + 1) * blk].astype(BF16), tri)
            starts.append(run)
            parts.append(c + run)
            run = run + c[:, blk - 1:blk]
        return jnp.concatenate(parts, axis=1), starts, run

    eqf = jnp.where(eq, 1.0, 0.0)
    eq_incl, _, _ = cumsum_blocks(eqf)
    sel = jnp.where(gt, 1.0, jnp.where(eq & (eq_incl - eqf < need), 1.0, 0.0))
    incl, starts, _ = cumsum_blocks(sel)

    base = (lax.broadcasted_iota(I32, (ne, 1), 0) * rows_per_expert + bi * cap)
    pos_ref[0] = jnp.where(sel > 0.0, (incl - sel).astype(I32) + base, -1)
    lane = lax.broadcasted_iota(I32, (ne, LANES), 1)
    offs = jnp.full((ne, LANES), cap, I32) + base
    for kb, st in enumerate(starts):
        offs = jnp.where(lane == kb, st.astype(I32) + base, offs)
    offs_ref[0] = offs


def _routing(aff_t, *, cap, blk):
    b, ne, s = aff_t.shape
    assert s // blk + 1 <= LANES
    tri = jnp.asarray(np.triu(np.ones((blk, blk), np.float32)), BF16)
    return pl.pallas_call(
        functools.partial(_routing_kernel, cap=cap, blk=blk, rows_per_expert=b * cap),
        grid=(b,),
        in_specs=[pl.BlockSpec((1, ne, s), lambda bi: (bi, 0, 0)), _resident(tri.shape)],
        out_specs=[pl.BlockSpec((1, ne, s), lambda bi: (bi, 0, 0)),
                   pl.BlockSpec((1, ne, LANES), lambda bi: (bi, 0, 0))],
        out_shape=[jax.ShapeDtypeStruct((b, ne, s), I32), jax.ShapeDtypeStruct((b, ne, LANES), I32)],
        compiler_params=_params(("arbitrary",)),
        name="routing",
    )(aff_t, tri)


def _slots_kernel(offs_ref, pos_ref, idx_ref, *, cap, blk, w, rows_per_expert):
    bi = pl.program_id(0)
    ne, s = pos_ref.shape[1], pos_ref.shape[2]
    nb = s // blk
    idx_ref[...] = jnp.zeros(idx_ref.shape, I32)
    sub = lax.broadcasted_iota(I32, (w, blk), 0)
    tok = lax.broadcasted_iota(I32, (w, blk), 1)

    def block(kb, carry):
        tok1 = tok + (kb * blk + bi * s + 1)
        windows, trips = [], []
        for e in range(ne):
            base = e * rows_per_expert + bi * cap
            o = (bi * (nb + 1) + kb) * ne + e
            lo8 = (offs_ref[o] - base) // 8 * 8
            hi = offs_ref[o + ne] - base
            posrow = pos_ref[0, e:e + 1, pl.ds(pl.multiple_of(kb * blk, blk), blk)] - base

            def window(r, c, e=e, lo8=lo8, posrow=posrow):
                start = pl.multiple_of(jnp.minimum(lo8 + r * w, cap - w), 8)
                hit = jnp.where(sub + start == posrow, tok1, 0)
                val = jnp.sum(hit.astype(F32), axis=1, keepdims=True).astype(I32)
                old = idx_ref[0, pl.ds(start, w), e:e + 1]
                idx_ref[0, pl.ds(start, w), e:e + 1] = jnp.where(val > 0, val - 1, old)
                return c

            windows.append(window)
            trips.append((hi - lo8 + w - 1) // w)
        for window in windows:
            window(0, 0)
        for window, n in zip(windows, trips):
            lax.fori_loop(1, n, window, 0)
        return carry

    lax.fori_loop(0, nb, block, 0)


def _slots(offs_flat, pos, *, cap, blk, w):
    b, ne, s = pos.shape
    return pl.pallas_call(
        functools.partial(_slots_kernel, cap=cap, blk=blk, w=w, rows_per_expert=b * cap),
        grid_spec=pltpu.PrefetchScalarGridSpec(
            num_scalar_prefetch=1,
            grid=(b,),
            in_specs=[pl.BlockSpec((1, ne, s), lambda bi, o: (bi, 0, 0))],
            out_specs=pl.BlockSpec((1, cap, ne), lambda bi, o: (bi, 0, 0))),
        out_shape=jax.ShapeDtypeStruct((b, cap, ne), I32),
        compiler_params=_params(("arbitrary",)),
        name="slots",
    )(offs_flat, pos)


def _ffn_kernel(idx_ref, tab_hbm, wg_hbm, wu_hbm, wd_hbm, y_ref, buf_a, buf_b, wg_ref, wu_ref, wd_ref,
                sg_ref, su_ref, sd_ref, sem, wsem, *, t, d):
    e = pl.program_id(0)
    j = pl.program_id(1)
    ne, nt = pl.num_programs(0), pl.num_programs(1)
    step = e * nt + j
    nsteps = ne * nt
    rc, fc = sg_ref.shape[0], sd_ref.shape[0]

    def row_copy(base, i, buf, sl):
        return pltpu.make_async_copy(tab_hbm.at[pl.ds(idx_ref[base + i], 1)], buf.at[pl.ds(i, 1)], sem.at[sl])

    def wait_rows(buf, sl):
        pltpu.make_async_copy(tab_hbm.at[pl.ds(0, t)], buf, sem.at[sl]).wait()

    def piece_copies(ex, pc):
        return [pltpu.make_async_copy(wg_hbm.at[ex, pl.ds(pl.multiple_of(pc * rc, rc), rc)], sg_ref, wsem.at[0]),
                pltpu.make_async_copy(wu_hbm.at[ex, pl.ds(pl.multiple_of(pc * rc, rc), rc)], su_ref, wsem.at[1]),
                pltpu.make_async_copy(wd_hbm.at[ex, pl.ds(pl.multiple_of(pc * fc, fc), fc)], sd_ref, wsem.at[2])]

    def cast_piece(st, pc):
        wg_ref[st, pl.ds(pl.multiple_of(pc * rc, rc), rc), :] = sg_ref[...].astype(BF16)
        wu_ref[st, pl.ds(pl.multiple_of(pc * rc, rc), rc), :] = su_ref[...].astype(BF16)
        wd_ref[st, pl.ds(pl.multiple_of(pc * fc, fc), fc), :] = sd_ref[...].astype(BF16)

    @pl.when(step == 0)
    def _():
        def issue(i, carry):
            row_copy(0, i, buf_a, 0).start()
            return carry
        lax.fori_loop(0, t, issue, 0)

        def load(pc, carry):
            cps = piece_copies(0, pc)
            for cp in cps:
                cp.start()
            for cp in cps:
                cp.wait()
            cast_piece(0, pc)
            return carry
        lax.fori_loop(0, nt - 1, load, 0)
        for cp in piece_copies(0, nt - 1):
            cp.start()

    pe = jnp.where(j > 0, e, e - 1)
    pj = jnp.where(j > 0, j - 1, nt - 1)
    nxt_e = jnp.minimum(e + 1, ne - 1)
    ws = e % 2

    def run(cur, cur_sl, nxt, nxt_sl):
        wait_rows(cur, cur_sl)
        for cp in piece_copies(jnp.minimum(pe + 1, ne - 1), pj):
            cp.wait()
        base = jnp.minimum(step + 1, nsteps - 1) * t
        for i in range(t):
            row_copy(base, i, nxt, nxt_sl).start()
        cast_piece((pe + 1) % 2, pj)
        for cp in piece_copies(nxt_e, j):
            cp.start()
        xb = cur[:, 0:d].astype(BF16)
        aff = cur[:, d:d + LANES]
        lane = lax.broadcasted_iota(I32, aff.shape, 1)
        gate_w = jnp.sum(jnp.where(lane == e, aff, 0.0), axis=-1, keepdims=True)

        @pl.when(idx_ref[base] < 0)
        def _():
            y_ref[0:16, 0:LANES] = jnp.zeros((16, LANES), BF16)

        g = _dot(xb, wg_ref[ws])
        up = _dot(xb, wu_ref[ws])
        hid = ((g / (1.0 + jnp.exp(-g))) * up).astype(BF16)
        y = _dot(hid, wd_ref[ws])
        y_ref[...] = (y * gate_w).astype(BF16)

        @pl.when(step == nsteps - 1)
        def _():
            wait_rows(nxt, nxt_sl)
            for cp in piece_copies(nxt_e, j):
                cp.wait()

    @pl.when(step % 2 == 0)
    def _():
        run(buf_a, 0, buf_b, 1)

    @pl.when(step % 2 == 1)
    def _():
        run(buf_b, 1, buf_a, 0)


def _expert_ffn(idx_flat, tab, w_gate, w_up, w_down, *, t):
    ne, d, ff = w_gate.shape
    rows = idx_flat.shape[0]
    nt = rows // ne // t
    dw = tab.shape[1]
    bf16_rows = 16
    assert d % (nt * bf16_rows) == 0 and ff % (nt * bf16_rows) == 0
    return pl.pallas_call(
        functools.partial(_ffn_kernel, t=t, d=d),
        grid_spec=pltpu.PrefetchScalarGridSpec(
            num_scalar_prefetch=1,
            grid=(ne, nt),
            in_specs=[pl.BlockSpec(memory_space=pl.ANY)] * 4,
            out_specs=pl.BlockSpec((t, d), lambda e, j, idx: (e * nt + j, 0)),
            scratch_shapes=[pltpu.VMEM((t, dw), F32), pltpu.VMEM((t, dw), F32),
                            pltpu.VMEM((2, d, ff), BF16), pltpu.VMEM((2, d, ff), BF16), pltpu.VMEM((2, ff, d), BF16),
                            pltpu.VMEM((d // nt, ff), F32), pltpu.VMEM((d // nt, ff), F32),
                            pltpu.VMEM((ff // nt, d), F32),
                            pltpu.SemaphoreType.DMA((2,)), pltpu.SemaphoreType.DMA((3,))]),
        out_shape=jax.ShapeDtypeStruct((rows, d), BF16),
        compiler_params=_params(("arbitrary", "arbitrary"), vmem=FFN_VMEM_LIMIT),
        name="expert_ffn",
    )(idx_flat, tab, w_gate, w_up, w_down)


def _combine_kernel(offs_ref, pos_ref, y_hbm, x1_ref, g2_ref, fg_ref, o_ref, ybuf, acc_ref, sem, *, tt, w, total_rows):
    ne = pos_ref.shape[1]
    nk = pl.num_programs(1)
    step = pl.program_id(0) * nk + pl.program_id(1)
    nsteps = pl.num_programs(0) * nk
    slot = step % 2

    def bounds(st):
        o0 = (st // nk * (nk + 1) + st % nk) * ne
        lo8 = [(offs_ref[o0 + e] // 8) * 8 for e in range(ne)]
        hi = [offs_ref[o0 + ne + e] for e in range(ne)]
        return lo8, hi

    def window_copies(lo8, r, sl):
        starts = [pl.multiple_of(jnp.minimum(lo8[e] + r * w, total_rows - w), 8) for e in range(ne)]
        copies = [pltpu.make_async_copy(y_hbm.at[pl.ds(starts[e], w)], ybuf.at[sl, pl.ds(e * w, w)], sem.at[sl, e])
                  for e in range(ne)]
        return starts, copies

    @pl.when(step == 0)
    def _():
        for cp in window_copies(bounds(0)[0], 0, 0)[1]:
            cp.start()

    lo8, hi = bounds(step)
    starts0, copies0 = window_copies(lo8, 0, slot)
    for cp in copies0:
        cp.wait()

    @pl.when(step + 1 < nsteps)
    def _():
        for cp in window_copies(bounds(step + 1)[0], 0, 1 - slot)[1]:
            cp.start()

    pos = pos_ref[0]

    def contribution(r, starts):
        sub = lax.broadcasted_iota(I32, (w, tt), 0)
        want = [jnp.where(pos[e:e + 1, :] >= lo8[e] + r * w, pos[e:e + 1, :], -1) for e in range(ne)]
        blocks = [jnp.where(sub + starts[e] == want[e], 1.0, 0.0).astype(BF16) for e in range(ne)]
        onehot_t = jnp.concatenate(blocks, axis=0)
        return lax.dot_general(onehot_t, ybuf[slot], (((0,), (0,)), ((), ())), preferred_element_type=F32)

    acc_ref[...] = contribution(0, starts0)

    rounds = jnp.int32(1)
    for e in range(ne):
        rounds = jnp.maximum(rounds, (hi[e] - lo8[e] + w - 1) // w)

    def extra_round(r, carry):
        starts, copies = window_copies(lo8, r, slot)
        for cp in copies:
            cp.start()
        for cp in copies:
            cp.wait()
        acc_ref[...] += contribution(r, starts)
        return carry

    lax.fori_loop(1, rounds, extra_round, 0)
    x2 = x1_ref[0] + g2_ref[0] * acc_ref[...]
    o_ref[0] = _rms(x2) * fg_ref[...]


def _combine(offs_flat, pos, y, x1, g2, final_g, *, tt, w):
    b, s, d = x1.shape
    ne = pos.shape[1]
    total_rows = y.shape[0]
    return pl.pallas_call(
        functools.partial(_combine_kernel, tt=tt, w=w, total_rows=total_rows),
        grid_spec=pltpu.PrefetchScalarGridSpec(
            num_scalar_prefetch=1,
            grid=(b, s // tt),
            in_specs=[pl.BlockSpec((1, ne, tt), lambda bi, k, o: (bi, 0, k)),
                      pl.BlockSpec(memory_space=pl.ANY),
                      pl.BlockSpec((1, tt, d), lambda bi, k, o: (bi, k, 0)),
                      pl.BlockSpec((1, 1, d), lambda bi, k, o: (bi, 0, 0)),
                      pl.BlockSpec((1, d), lambda bi, k, o: (0, 0))],
            out_specs=pl.BlockSpec((1, tt, d), lambda bi, k, o: (bi, k, 0)),
            scratch_shapes=[pltpu.VMEM((2, ne * w, d), BF16), pltpu.VMEM((tt, d), F32),
                            pltpu.SemaphoreType.DMA((2, ne))]),
        out_shape=jax.ShapeDtypeStruct((b, s, d), F32),
        compiler_params=_params(("arbitrary", "arbitrary")),
        name="combine_final",
    )(offs_flat, pos, y, x1, g2, final_g)


def _rope_tables(s):
    n_rows = s // GRID_W
    row = np.repeat(np.arange(n_rows), GRID_W).astype(np.float64)
    col = np.tile(np.arange(GRID_W), n_rows).astype(np.float64)
    inv_freq = ROPE_THETA ** (-np.arange(ROPE_PAIRS, dtype=np.float64) / ROPE_PAIRS)
    ang_r = row[:, None] * inv_freq[None, :]
    ang_c = col[:, None] * inv_freq[None, :]
    cos64 = np.concatenate([np.cos(ang_r)] * 2 + [np.cos(ang_c)] * 2, axis=-1).astype(np.float32)
    sin64 = np.concatenate([-np.sin(ang_r), np.sin(ang_r), -np.sin(ang_c), np.sin(ang_c)], axis=-1).astype(np.float32)
    return cos64, sin64


ROPE_SWAP = np.concatenate([np.arange(16, 32), np.arange(0, 16), np.arange(48, 64), np.arange(32, 48)])


def kernel(x, c, ctx, c_ctx, w_mod, b_mod, norm1_g, norm2_g, w_in, conv_w, q_norm_g, w_uq, kv_norm_g, w_ukv, w_out,
           w_router, w_gate, w_up, w_down, final_g):
    b, s, d = x.shape
    lc = ctx.shape[1]
    assert w_in.shape[0] == 1, "single-layer stack"
    assert b <= 7 and s % max(ROW_TM, ATT_TQ, ATT_TK, CMB_TT) == 0
    cap = EC_FACTOR * s // N_EXPERTS
    assert (b * cap) % FFN_T == 0 and cap % IDX_PC == 0

    cvec = jnp.zeros((8, d), F32).at[0:b].set(c).at[b].set(c_ctx)
    mod = _modulation(cvec, w_mod[0], b_mod[0][None, :])
    sh1, sc1, g1, sh2, sc2, g2 = [mod[0:b, None, i * d:(i + 1) * d] for i in range(6)]
    shc1, scc1 = mod[b:b + 1, None, 0:d], mod[b:b + 1, None, d:2 * d]

    n_main = 3 * CONV_WIDTH + Q_LORA + KV_LORA
    w_main = w_in[0, :, 0:n_main].astype(BF16)
    w_kr = w_in[0, :, n_main:]
    w_kr2 = jnp.concatenate([w_kr, w_kr[:, ROPE_SWAP]], axis=-1).astype(BF16)
    w_qn = w_uq[0, :, :, 0:QK_NOPE].reshape(Q_LORA, MLA_HEADS * QK_NOPE).astype(BF16)
    w_qr3 = w_uq[0, :, :, QK_NOPE:]
    w_qr = w_qr3.reshape(Q_LORA, MLA_HEADS * QK_ROPE).astype(BF16)
    w_qrs = w_qr3[:, :, ROPE_SWAP].reshape(Q_LORA, MLA_HEADS * QK_ROPE).astype(BF16)
    w_kn = w_ukv[0, :, :, 0:QK_NOPE].reshape(KV_LORA, MLA_HEADS * QK_NOPE).astype(BF16)
    w_v = w_ukv[0, :, :, QK_NOPE:].reshape(KV_LORA, MLA_HEADS * V_DIM).astype(BF16)
    w_o = w_out[0].astype(BF16)
    wr = jnp.zeros((d, LANES), F32).at[:, 0:N_EXPERTS].set(w_router[0])
    wr_hi = wr.astype(BF16)
    wr2 = jnp.concatenate([wr_hi, (wr - wr_hi.astype(F32)).astype(BF16)], axis=-1)

    cos64, sin64 = _rope_tables(s)
    cos2 = jnp.asarray(np.concatenate([cos64, cos64], axis=-1))
    sin2 = jnp.asarray(np.concatenate([sin64, sin64], axis=-1))
    ktab = jnp.asarray(np.concatenate([cos64, sin64], axis=-1))

    n1 = norm1_g[0][None, :]
    qg, kvg = q_norm_g[0][None, :], kv_norm_g[0][None, :]
    u, bg, cqn, ckvn, kr2 = _inproj(x, sc1, sh1, n1, w_main, w_kr2, qg, kvg, with_conv=True, tm=ROW_TM)
    _, ckvn_c, kr2_c = _inproj(ctx, scc1, shc1, n1, w_main, w_kr2, qg, kvg, with_conv=False, tm=lc)

    q = _qproj(cqn, cos2, sin2, w_qn, w_qr, w_qrs, tm=ROW_TM)
    k, v = _kvproj(ckvn, kr2, ktab, w_kn, w_v, tm=ROW_TM)
    kc, vc = _kvproj(ckvn_c, kr2_c, None, w_kn, w_v, tm=lc)
    y_attn = _attention(q, k, v, kc, vc, tq=ATT_TQ, tk=ATT_TK)

    x1, tab, aff_t = _outproj(u, bg, conv_w[0], y_attn, w_o, x, g1, norm2_g[0][None, :], sc2, sh2, wr2, tm=ROW_TM)

    pos, offs = _routing(aff_t, cap=cap, blk=CMB_TT)
    nk1 = s // CMB_TT + 1
    offs_flat = jnp.swapaxes(offs[:, :, 0:nk1], 1, 2).reshape(-1)
    idx_t = _slots(offs_flat, pos, cap=cap, blk=CMB_TT, w=CMB_W)
    idx_flat = jnp.transpose(idx_t, (2, 0, 1)).reshape(-1)
    y = _expert_ffn(idx_flat, tab.reshape(b * s, d + LANES), w_gate[0], w_up[0], w_down[0], t=FFN_T)
    return _combine(offs_flat, pos, y, x1, g2, final_g[None, :], tt=CMB_TT, w=CMB_W)
```

```python
import functools
import math

import jax
import jax.numpy as jnp
import numpy as np
from jax import lax
from jax.experimental import pallas as pl
from jax.experimental.pallas import tpu as pltpu

F32 = jnp.float32
BF16 = jnp.bfloat16
I32 = jnp.int32

GRID_W = 64
CONV_WIDTH = 1024
MLA_HEADS = 8
QK_NOPE = 128
QK_ROPE = 64
V_DIM = 128
Q_LORA = 512
KV_LORA = 512
QK_DIM = QK_NOPE + QK_ROPE
MLA_WIDTH = MLA_HEADS * V_DIM
V_EXT = 2 * V_DIM
N_EXPERTS = 16
EC_FACTOR = 2
ROPE_THETA = 10000.0
ROPE_PAIRS = QK_ROPE // 4
ATTN_SCALE = 1.0 / math.sqrt(QK_DIM)
LOG2E = math.log2(math.e)
EPS = 1e-6

LANES = 128
VMEM_LIMIT = 56 * 1024 * 1024
FFN_VMEM_LIMIT = 60 * 1024 * 1024

MOD_TN = 1024
ROW_TM = 512
QKV_TM = 1024
ATT_TQ = 1024
ATT_TK = 512
FFN_T = 256
CMB_TT = 256
CMB_W = 64
IDX_PC = 64


def _dot(a, b):
    return jnp.dot(a, b, preferred_element_type=F32)


def _params(sem, vmem=VMEM_LIMIT):
    return pltpu.CompilerParams(dimension_semantics=sem, vmem_limit_bytes=vmem)


def _resident(shape):
    nd = len(shape)
    return pl.BlockSpec(shape, lambda *_: (0,) * nd, pipeline_mode=pl.Buffered(1))


def _mod_kernel(c_ref, w_ref, b_ref, o_ref):
    cv = c_ref[...]
    a = cv / (1.0 + jnp.exp(-cv))
    a_hi = a.astype(BF16)
    a_lo = (a - a_hi.astype(F32)).astype(BF16)
    w = w_ref[...]
    w_hi = w.astype(BF16)
    w_lo = (w - w_hi.astype(F32)).astype(BF16)
    o_ref[...] = _dot(a_hi, w_hi) + _dot(a_lo, w_hi) + _dot(a_hi, w_lo) + b_ref[...]


def _modulation(cvec, w_mod, b_mod):
    d, n = w_mod.shape
    return pl.pallas_call(
        _mod_kernel,
        grid=(n // MOD_TN,),
        in_specs=[pl.BlockSpec((8, d), lambda j: (0, 0)),
                  pl.BlockSpec((d, MOD_TN), lambda j: (0, j)),
                  pl.BlockSpec((1, MOD_TN), lambda j: (0, j))],
        out_specs=pl.BlockSpec((8, MOD_TN), lambda j: (0, j)),
        out_shape=jax.ShapeDtypeStruct((8, n), F32),
        compiler_params=_params(("arbitrary",)),
        name="modulation",
    )(cvec, w_mod, b_mod)


def _rms(v):
    return v * lax.rsqrt(jnp.mean(v * v, axis=-1, keepdims=True) + EPS)


def _inproj_kernel(x_ref, sc_ref, sh_ref, g_ref, w_ref, wkr_ref, qg_ref, kvg_ref, *outs, with_conv):
    h = _rms(x_ref[0]) * g_ref[...]
    h = h * (1.0 + sc_ref[0]) + sh_ref[0]
    hb = h.astype(BF16)
    c = CONV_WIDTH
    if with_conv:
        u_ref, bg_ref, cq_ref, ckv_ref, kr_ref = outs
        xin = _dot(hb, w_ref[:, 0:c])
        cg = _dot(hb, w_ref[:, 2 * c:3 * c])
        u_ref[0] = (cg * xin).astype(BF16)
        bg_ref[0] = _dot(hb, w_ref[:, c:2 * c]).astype(BF16)
    else:
        cq_ref, ckv_ref, kr_ref = outs
    o = 3 * c
    cq = _dot(hb, w_ref[:, o:o + Q_LORA])
    cq_ref[0] = (_rms(cq) * qg_ref[...]).astype(BF16)
    ckv = _dot(hb, w_ref[:, o + Q_LORA:o + Q_LORA + KV_LORA])
    ckv_ref[0] = (_rms(ckv) * kvg_ref[...]).astype(BF16)
    kr_ref[0] = _dot(hb, wkr_ref[...])


def _inproj(x, scale, shift, gain, w_main, w_kr2, q_g, kv_g, *, with_conv, tm):
    b, s, d = x.shape
    per_batch = scale.shape[0] > 1
    vec = pl.BlockSpec((1, 1, d), (lambda bi, i: (bi, 0, 0)) if per_batch else (lambda bi, i: (0, 0, 0)))
    row = lambda n: pl.BlockSpec((1, tm, n), lambda bi, i: (bi, i, 0))
    outs, specs = [], []
    if with_conv:
        outs += [jax.ShapeDtypeStruct((b, s, CONV_WIDTH), BF16)] * 2
        specs += [row(CONV_WIDTH)] * 2
    outs += [jax.ShapeDtypeStruct((b, s, Q_LORA), BF16), jax.ShapeDtypeStruct((b, s, KV_LORA), BF16),
             jax.ShapeDtypeStruct((b, s, 2 * QK_ROPE), F32)]
    specs += [row(Q_LORA), row(KV_LORA), row(2 * QK_ROPE)]
    return pl.pallas_call(
        functools.partial(_inproj_kernel, with_conv=with_conv),
        grid=(b, s // tm),
        in_specs=[row(d), vec, vec, _resident((1, d)), _resident(w_main.shape), _resident(w_kr2.shape),
                  _resident((1, Q_LORA)), _resident((1, KV_LORA))],
        out_specs=specs,
        out_shape=outs,
        compiler_params=_params(("arbitrary", "arbitrary")),
        name="inproj_conv" if with_conv else "inproj_ctx",
    )(x, scale, shift, gain, w_main, w_kr2, q_g, kv_g)


def _qproj_kernel(cq_ref, cos_ref, sin_ref, wn_ref, wr_ref, wrs_ref, q_ref):
    cq = cq_ref[0]
    qn = _dot(cq, wn_ref[...])
    qr = _dot(cq, wr_ref[...])
    qrs = _dot(cq, wrs_ref[...])
    reps = MLA_HEADS * QK_ROPE // LANES
    cos = jnp.concatenate([cos_ref[...]] * reps, axis=-1)
    sin = jnp.concatenate([sin_ref[...]] * reps, axis=-1)
    qrot = qr * cos + qrs * sin
    sc = ATTN_SCALE * LOG2E
    for h in range(MLA_HEADS):
        q_ref[0, h, :, 0:QK_NOPE] = (qn[:, h * QK_NOPE:(h + 1) * QK_NOPE] * sc).astype(BF16)
        q_ref[0, h, :, QK_NOPE:QK_DIM] = (qrot[:, h * QK_ROPE:(h + 1) * QK_ROPE] * sc).astype(BF16)


def _qproj(cqn, cos2, sin2, w_qn, w_qr, w_qrs, *, tm):
    b, s, r = cqn.shape
    return pl.pallas_call(
        _qproj_kernel,
        grid=(b, s // tm),
        in_specs=[pl.BlockSpec((1, tm, r), lambda bi, i: (bi, i, 0)),
                  pl.BlockSpec((tm, LANES), lambda bi, i: (i, 0)),
                  pl.BlockSpec((tm, LANES), lambda bi, i: (i, 0)),
                  _resident(w_qn.shape), _resident(w_qr.shape), _resident(w_qrs.shape)],
        out_specs=pl.BlockSpec((1, MLA_HEADS, tm, QK_DIM), lambda bi, i: (bi, 0, i, 0)),
        out_shape=jax.ShapeDtypeStruct((b, MLA_HEADS, s, QK_DIM), BF16),
        compiler_params=_params(("arbitrary", "arbitrary")),
        name="qproj",
    )(cqn, cos2, sin2, w_qn, w_qr, w_qrs)


def _kvproj_kernel(ckv_ref, kr_ref, *rest, rope):
    if rope:
        tab_ref, wk_ref, wv_ref, k_ref, v_ref = rest
        prod = kr_ref[0] * tab_ref[...]
        krot = prod[:, 0:QK_ROPE] + prod[:, QK_ROPE:2 * QK_ROPE]
    else:
        wk_ref, wv_ref, k_ref, v_ref = rest
        krot = kr_ref[0][:, 0:QK_ROPE]
    ckv = ckv_ref[0]
    kn = _dot(ckv, wk_ref[...])
    vv = _dot(ckv, wv_ref[...])
    krot = krot.astype(BF16)
    for h in range(MLA_HEADS):
        k_ref[0, h, :, 0:QK_NOPE] = kn[:, h * QK_NOPE:(h + 1) * QK_NOPE].astype(BF16)
        k_ref[0, h, :, QK_NOPE:QK_DIM] = krot
        v_ref[0, h, :, 0:V_DIM] = vv[:, h * V_DIM:(h + 1) * V_DIM].astype(BF16)
        v_ref[0, h, :, V_DIM:V_EXT] = jnp.ones((vv.shape[0], V_EXT - V_DIM), BF16)


def _kvproj(ckvn, kr2, tab, w_kn, w_v, *, tm):
    b, s, r = ckvn.shape
    rope = tab is not None
    ins = [ckvn, kr2]
    specs = [pl.BlockSpec((1, tm, r), lambda bi, i: (bi, i, 0)),
             pl.BlockSpec((1, tm, 2 * QK_ROPE), lambda bi, i: (bi, i, 0))]
    if rope:
        ins.append(tab)
        specs.append(pl.BlockSpec((tm, 2 * QK_ROPE), lambda bi, i: (i, 0)))
    ins += [w_kn, w_v]
    specs += [_resident(w_kn.shape), _resident(w_v.shape)]
    return pl.pallas_call(
        functools.partial(_kvproj_kernel, rope=rope),
        grid=(b, s // tm),
        in_specs=specs,
        out_specs=[pl.BlockSpec((1, MLA_HEADS, tm, QK_DIM), lambda bi, i: (bi, 0, i, 0)),
                   pl.BlockSpec((1, MLA_HEADS, tm, V_EXT), lambda bi, i: (bi, 0, i, 0))],
        out_shape=[jax.ShapeDtypeStruct((b, MLA_HEADS, s, QK_DIM), BF16),
                   jax.ShapeDtypeStruct((b, MLA_HEADS, s, V_EXT), BF16)],
        compiler_params=_params(("arbitrary", "arbitrary")),
        name="kvproj_rope" if rope else "kvproj_ctx",
    )(*ins)


def _attn_kernel(q_ref, k_ref, v_ref, kc_ref, vc_ref, o_ref, m_ref, acc_ref, sa_ref, sb_ref, *, tk):
    q = q_ref[0, 0]
    m_ref[...] = jnp.full(m_ref.shape, -jnp.inf, F32)
    acc_ref[...] = jnp.zeros(acc_ref.shape, F32)

    def lanes(a, n):
        return jnp.concatenate([a] * (n // LANES), axis=1)

    def scores(k):
        return lax.dot_general(q, k, (((1,), (1,)), ((), ())), preferred_element_type=F32)

    def update(s, v):
        m_old = m_ref[...]
        m_new = jnp.maximum(m_old, jnp.max(s, axis=-1, keepdims=True))
        alpha = jnp.exp2(m_old - m_new)
        p = jnp.exp2(s - lanes(m_new, s.shape[1]))
        acc_ref[...] = lanes(alpha, acc_ref.shape[1]) * acc_ref[...] + _dot(p.astype(BF16), v)
        m_ref[...] = m_new

    def chunk(ref, j):
        return ref[0, 0, pl.ds(pl.multiple_of(j * tk, tk), tk), :]

    n = k_ref.shape[2] // tk
    bufs = (sa_ref, sb_ref)
    sa_ref[...] = scores(chunk(k_ref, 0))
    for j in range(n):
        nxt = bufs[(j + 1) % 2]
        if j + 1 < n:
            nxt[...] = scores(chunk(k_ref, j + 1))
        else:
            s_ctx = scores(kc_ref[0, 0])
        update(bufs[j % 2][...], chunk(v_ref, j))
    update(s_ctx, vc_ref[0, 0])
    o_ref[0] = (acc_ref[:, 0:V_DIM] / acc_ref[:, V_DIM:V_EXT]).astype(BF16)


def _attention(q, k, v, kc, vc, *, tq, tk):
    b, h, s, dq = q.shape
    lc = kc.shape[2]
    return pl.pallas_call(
        functools.partial(_attn_kernel, tk=tk),
        grid=(b, h, s // tq),
        in_specs=[pl.BlockSpec((1, 1, tq, dq), lambda bi, hi, i: (bi, hi, i, 0)),
                  pl.BlockSpec((1, 1, s, dq), lambda bi, hi, i: (bi, hi, 0, 0)),
                  pl.BlockSpec((1, 1, s, V_EXT), lambda bi, hi, i: (bi, hi, 0, 0)),
                  pl.BlockSpec((1, 1, lc, dq), lambda bi, hi, i: (bi, hi, 0, 0)),
                  pl.BlockSpec((1, 1, lc, V_EXT), lambda bi, hi, i: (bi, hi, 0, 0))],
        out_specs=pl.BlockSpec((1, tq, V_DIM), lambda bi, hi, i: (bi, i, hi)),
        out_shape=jax.ShapeDtypeStruct((b, s, h * V_DIM), BF16),
        scratch_shapes=[pltpu.VMEM((tq, LANES), F32), pltpu.VMEM((tq, V_EXT), F32),
                        pltpu.VMEM((tq, tk), F32), pltpu.VMEM((tq, tk), F32)],
        compiler_params=_params(("arbitrary", "arbitrary", "arbitrary")),
        name="attention",
    )(q, k, v, kc, vc)


HALO = 16


def _outproj_kernel(u_ref, up_ref, un_ref, bg_ref, cw_ref, ya_ref, wo_ref, x_ref, g1_ref, n2_ref, sc_ref, sh_ref,
                    wr2_ref, x1_ref, tab_ref, afft_ref, *, tm, d):
    i = pl.program_id(1)
    last = pl.num_programs(1) - 1
    u = u_ref[0].astype(F32)
    rows = lax.broadcasted_iota(I32, (tm, 1), 0)
    prev_row = jnp.where(i > 0, up_ref[0][HALO - 1:HALO, :].astype(F32), 0.0)
    next_row = jnp.where(i < last, un_ref[0][0:1, :].astype(F32), 0.0)
    um1 = jnp.where(rows == 0, prev_row, pltpu.roll(u, 1, 0))
    up1 = jnp.where(rows == tm - 1, next_row, pltpu.roll(u, tm - 1, 0))
    cw = cw_ref[...]
    yc = (bg_ref[0].astype(F32) * (um1 * cw[0:1] + u * cw[1:2] + up1 * cw[2:3])).astype(BF16)
    nsub = 2
    tr = tm // nsub
    for r in range(nsub):
        rs = slice(r * tr, (r + 1) * tr)
        y = _dot(yc[rs], wo_ref[0:CONV_WIDTH, :]) + _dot(ya_ref[0, rs, :], wo_ref[CONV_WIDTH:, :])
        x1 = x_ref[0, rs, :] + g1_ref[0] * y
        x1_ref[0, rs, :] = x1
        h2 = _rms(x1) * n2_ref[...]
        h2 = h2 * (1.0 + sc_ref[0]) + sh_ref[0]
        tab_ref[0, rs, 0:d] = h2
        h_hi = h2.astype(BF16)
        h_lo = (h2 - h_hi.astype(F32)).astype(BF16)
        l2 = _dot(h_hi, wr2_ref[...])
        logits = l2[:, 0:LANES] + l2[:, LANES:2 * LANES] + _dot(h_lo, wr2_ref[:, 0:LANES])
        lane = lax.broadcasted_iota(I32, logits.shape, 1)
        logits = jnp.where(lane < N_EXPERTS, logits, -jnp.inf)
        e = jnp.exp(logits - jnp.max(logits, axis=-1, keepdims=True))
        aff = e / jnp.sum(e, axis=-1, keepdims=True)
        tab_ref[0, rs, d:d + LANES] = aff
        afft_ref[0, :, rs] = aff.T[0:N_EXPERTS, :]


def _outproj(u, bg, conv_w, y_attn, w_out, x, g1, n2, sc2, sh2, wr2, *, tm):
    b, s, d = x.shape
    nh = tm // HALO
    nhb = s // HALO
    row = lambda n: pl.BlockSpec((1, tm, n), lambda bi, i: (bi, i, 0))
    vec = pl.BlockSpec((1, 1, d), lambda bi, i: (bi, 0, 0))
    return pl.pallas_call(
        functools.partial(_outproj_kernel, tm=tm, d=d),
        grid=(b, s // tm),
        in_specs=[row(CONV_WIDTH),
                  pl.BlockSpec((1, HALO, CONV_WIDTH), lambda bi, i: (bi, jnp.maximum(i * nh - 1, 0), 0)),
                  pl.BlockSpec((1, HALO, CONV_WIDTH), lambda bi, i: (bi, jnp.minimum((i + 1) * nh, nhb - 1), 0)),
                  row(CONV_WIDTH), _resident(conv_w.shape), row(MLA_WIDTH), _resident(w_out.shape), row(d),
                  vec, _resident((1, d)), vec, vec, _resident(wr2.shape)],
        out_specs=[row(d), row(d + LANES), pl.BlockSpec((1, N_EXPERTS, tm), lambda bi, i: (bi, 0, i))],
        out_shape=[jax.ShapeDtypeStruct((b, s, d), F32), jax.ShapeDtypeStruct((b, s, d + LANES), F32),
                   jax.ShapeDtypeStruct((b, N_EXPERTS, s), F32)],
        compiler_params=_params(("arbitrary", "arbitrary")),
        name="outproj_router",
    )(u, u, u, bg, conv_w, y_attn, w_out, x, g1, n2, sc2, sh2, wr2)


def _routing_kernel(aff_ref, tri_ref, pos_ref, offs_ref, *, cap, blk, rows_per_expert):
    bi = pl.program_id(0)
    aff = aff_ref[0]
    ne, s = aff.shape
    capf = jnp.float32(cap)

    def bisect(t, prefix):
        cand = prefix | jnp.left_shift(jnp.int32(1), 30 - t)
        cnt = jnp.sum(jnp.where(aff >= lax.bitcast_convert_type(cand, F32), 1.0, 0.0), axis=1, keepdims=True)
        return jnp.where(cnt >= capf, cand, prefix)

    floor_bits = lax.fori_loop(0, 31, bisect, jnp.zeros((ne, 1), I32))
    thr = jnp.min(jnp.where(aff >= lax.bitcast_convert_type(floor_bits, F32), aff, jnp.inf), axis=1, keepdims=True)
    gt = aff > thr
    eq = aff == thr
    need = capf - jnp.sum(jnp.where(gt, 1.0, 0.0), axis=1, keepdims=True)
    tri = tri_ref[...]

    def cumsum_blocks(mask_f32):
        run = jnp.zeros((ne, 1), F32)
        parts, starts = [], []
        for kb in range(s // blk):
            c = _dot(mask_f32[:, kb * blk:(kb + 1) * blk].astype(BF16), tri)
            starts.append(run)
            parts.append(c + run)
            run = run + c[:, blk - 1:blk]
        return jnp.concatenate(parts, axis=1), starts, run

    eqf = jnp.where(eq, 1.0, 0.0)
    eq_incl, _, _ = cumsum_blocks(eqf)
    sel = jnp.where(gt, 1.0, jnp.where(eq & (eq_incl - eqf < need), 1.0, 0.0))
    incl, starts, _ = cumsum_blocks(sel)

    base = (lax.broadcasted_iota(I32, (ne, 1), 0) * rows_per_expert + bi * cap)
    pos_ref[0] = jnp.where(sel > 0.0, (incl - sel).astype(I32) + base, -1)
    lane = lax.broadcasted_iota(I32, (ne, LANES), 1)
    offs = jnp.full((ne, LANES), cap, I32) + base
    for kb, st in enumerate(starts):
        offs = jnp.where(lane == kb, st.astype(I32) + base, offs)
    offs_ref[0] = offs


def _routing(aff_t, *, cap, blk):
    b, ne, s = aff_t.shape
    assert s // blk + 1 <= LANES
    tri = jnp.asarray(np.triu(np.ones((blk, blk), np.float32)), BF16)
    return pl.pallas_call(
        functools.partial(_routing_kernel, cap=cap, blk=blk, rows_per_expert=b * cap),
        grid=(b,),
        in_specs=[pl.BlockSpec((1, ne, s), lambda bi: (bi, 0, 0)), _resident(tri.shape)],
        out_specs=[pl.BlockSpec((1, ne, s), lambda bi: (bi, 0, 0)),
                   pl.BlockSpec((1, ne, LANES), lambda bi: (bi, 0, 0))],
        out_shape=[jax.ShapeDtypeStruct((b, ne, s), I32), jax.ShapeDtypeStruct((b, ne, LANES), I32)],
        compiler_params=_params(("arbitrary",)),
        name="routing",
    )(aff_t, tri)


def _slots_kernel(offs_ref, pos_ref, idx_ref, *, cap, blk, w, rows_per_expert):
    bi = pl.program_id(0)
    ne, s = pos_ref.shape[1], pos_ref.shape[2]
    nb = s // blk
    idx_ref[...] = jnp.zeros(idx_ref.shape, I32)
    sub = lax.broadcasted_iota(I32, (w, blk), 0)
    tok = lax.broadcasted_iota(I32, (w, blk), 1)

    def block(kb, carry):
        tok1 = tok + (kb * blk + bi * s + 1)
        windows, trips = [], []
        for e in range(ne):
            base = e * rows_per_expert + bi * cap
            o = (bi * (nb + 1) + kb) * ne + e
            lo8 = (offs_ref[o] - base) // 8 * 8
            hi = offs_ref[o + ne] - base
            posrow = pos_ref[0, e:e + 1, pl.ds(pl.multiple_of(kb * blk, blk), blk)] - base

            def window(r, c, e=e, lo8=lo8, posrow=posrow):
                start = pl.multiple_of(jnp.minimum(lo8 + r * w, cap - w), 8)
                hit = jnp.where(sub + start == posrow, tok1, 0)
                val = jnp.sum(hit.astype(F32), axis=1, keepdims=True).astype(I32)
                old = idx_ref[0, pl.ds(start, w), e:e + 1]
                idx_ref[0, pl.ds(start, w), e:e + 1] = jnp.where(val > 0, val - 1, old)
                return c

            windows.append(window)
            trips.append((hi - lo8 + w - 1) // w)
        for window in windows:
            window(0, 0)
        for window, n in zip(windows, trips):
            lax.fori_loop(1, n, window, 0)
        return carry

    lax.fori_loop(0, nb, block, 0)


def _slots(offs_flat, pos, *, cap, blk, w):
    b, ne, s = pos.shape
    return pl.pallas_call(
        functools.partial(_slots_kernel, cap=cap, blk=blk, w=w, rows_per_expert=b * cap),
        grid_spec=pltpu.PrefetchScalarGridSpec(
            num_scalar_prefetch=1,
            grid=(b,),
            in_specs=[pl.BlockSpec((1, ne, s), lambda bi, o: (bi, 0, 0))],
            out_specs=pl.BlockSpec((1, cap, ne), lambda bi, o: (bi, 0, 0))),
        out_shape=jax.ShapeDtypeStruct((b, cap, ne), I32),
        compiler_params=_params(("arbitrary",)),
        name="slots",
    )(offs_flat, pos)


def _ffn_kernel(idx_ref, tab_hbm, wg_hbm, wu_hbm, wd_hbm, y_ref, buf_a, buf_b, buf_c, wg_ref, wu_ref, wd_ref,
                sg_ref, su_ref, sd_ref, sem, wsem, *, t, d):
    e = pl.program_id(0)
    j = pl.program_id(1)
    ne, nt = pl.num_programs(0), pl.num_programs(1)
    step = e * nt + j
    nsteps = ne * nt
    rc, fc = sg_ref.shape[0], sd_ref.shape[0]

    def row_copy(base, i, buf, sl):
        return pltpu.make_async_copy(tab_hbm.at[pl.ds(idx_ref[base + i], 1)], buf.at[pl.ds(i, 1)], sem.at[sl])

    def wait_rows(buf, sl):
        pltpu.make_async_copy(tab_hbm.at[pl.ds(0, t)], buf, sem.at[sl]).wait()

    def piece_copies(ex, pc):
        return [pltpu.make_async_copy(wg_hbm.at[ex, pl.ds(pl.multiple_of(pc * rc, rc), rc)], sg_ref, wsem.at[0]),
                pltpu.make_async_copy(wu_hbm.at[ex, pl.ds(pl.multiple_of(pc * rc, rc), rc)], su_ref, wsem.at[1]),
                pltpu.make_async_copy(wd_hbm.at[ex, pl.ds(pl.multiple_of(pc * fc, fc), fc)], sd_ref, wsem.at[2])]

    def cast_piece(st, pc):
        wg_ref[st, pl.ds(pl.multiple_of(pc * rc, rc), rc), :] = sg_ref[...].astype(BF16)
        wu_ref[st, pl.ds(pl.multiple_of(pc * rc, rc), rc), :] = su_ref[...].astype(BF16)
        wd_ref[st, pl.ds(pl.multiple_of(pc * fc, fc), fc), :] = sd_ref[...].astype(BF16)

    @pl.when(step == 0)
    def _():
        def issue(i, carry):
            row_copy(0, i, buf_a, 0).start()
            row_copy(jnp.minimum(1, nsteps - 1) * t, i, buf_b, 1).start()
            return carry
        lax.fori_loop(0, t, issue, 0)

        def load(pc, carry):
            cps = piece_copies(0, pc)
            for cp in cps:
                cp.start()
            for cp in cps:
                cp.wait()
            cast_piece(0, pc)
            return carry
        lax.fori_loop(0, nt - 1, load, 0)
        for cp in piece_copies(0, nt - 1):
            cp.start()

    pe = jnp.where(j > 0, e, e - 1)
    pj = jnp.where(j > 0, j - 1, nt - 1)
    nxt_e = jnp.minimum(e + 1, ne - 1)
    ws = e % 2

    def run(cur, cur_sl, mid, mid_sl, far, far_sl):
        wait_rows(cur, cur_sl)
        for cp in piece_copies(jnp.minimum(pe + 1, ne - 1), pj):
            cp.wait()
        base = jnp.minimum(step + 2, nsteps - 1) * t
        for i in range(t):
            row_copy(base, i, far, far_sl).start()
        cast_piece((pe + 1) % 2, pj)
        for cp in piece_copies(nxt_e, j):
            cp.start()
        xb = cur[:, 0:d].astype(BF16)
        aff = cur[:, d:d + LANES]
        lane = lax.broadcasted_iota(I32, aff.shape, 1)
        gate_w = jnp.sum(jnp.where(lane == e, aff, 0.0), axis=-1, keepdims=True)
        g = _dot(xb, wg_ref[ws])
        up = _dot(xb, wu_ref[ws])
        hid =((g / (1.0 + jnp.exp(-g))) * up).astype(BF16)
        y = _dot(hid, wd_ref[ws])
        y_ref[...] = (y * gate_w).astype(BF16)

        @pl.when(step == nsteps - 1)
        def _():
            wait_rows(mid, mid_sl)
            wait_rows(far, far_sl)
            for cp in piece_copies(nxt_e, j):
                cp.wait()

    bufs = (buf_a, buf_b, buf_c)
    for r in range(3):
        @pl.when(step % 3 == r)
        def _(r=r):
            run(bufs[r], r, bufs[(r + 1) % 3], (r + 1) % 3, bufs[(r + 2) % 3], (r + 2) % 3)


def _expert_ffn(idx_flat, tab, w_gate, w_up, w_down, *, t):
    ne, d, ff = w_gate.shape
    rows = idx_flat.shape[0]
    nt = rows // ne // t
    dw = tab.shape[1]
    bf16_rows = 16
    assert d % (nt * bf16_rows) == 0 and ff % (nt * bf16_rows) == 0
    return pl.pallas_call(
        functools.partial(_ffn_kernel, t=t, d=d),
        grid_spec=pltpu.PrefetchScalarGridSpec(
            num_scalar_prefetch=1,
            grid=(ne, nt),
            in_specs=[pl.BlockSpec(memory_space=pl.ANY)] * 4,
            out_specs=pl.BlockSpec((t, d), lambda e, j, idx: (e * nt + j, 0)),
            scratch_shapes=[pltpu.VMEM((t, dw), F32), pltpu.VMEM((t, dw), F32), pltpu.VMEM((t, dw), F32),
                            pltpu.VMEM((2, d, ff), BF16), pltpu.VMEM((2, d, ff), BF16), pltpu.VMEM((2, ff, d), BF16),
                            pltpu.VMEM((d // nt, ff), F32), pltpu.VMEM((d // nt, ff), F32),
                            pltpu.VMEM((ff // nt, d), F32),
                            pltpu.SemaphoreType.DMA((3,)), pltpu.SemaphoreType.DMA((3,))]),
        out_shape=jax.ShapeDtypeStruct((rows, d), BF16),
        compiler_params=_params(("arbitrary", "arbitrary"), vmem=FFN_VMEM_LIMIT),
        name="expert_ffn",
    )(idx_flat, tab, w_gate, w_up, w_down)


def _combine_kernel(offs_ref, pos_ref, y_hbm, x1_ref, g2_ref, fg_ref, o_ref, ybuf, acc_ref, sem, *, tt, w, total_rows):
    ne = pos_ref.shape[1]
    nk = pl.num_programs(1)
    step = pl.program_id(0) * nk + pl.program_id(1)
    nsteps = pl.num_programs(0) * nk
    slot = step % 2

    def bounds(st):
        o0 = (st // nk * (nk + 1) + st % nk) * ne
        lo8 = [(offs_ref[o0 + e] // 8) * 8 for e in range(ne)]
        hi = [offs_ref[o0 + ne + e] for e in range(ne)]
        return lo8, hi

    def window_copies(lo8, r, sl):
        starts = [pl.multiple_of(jnp.minimum(lo8[e] + r * w, total_rows - w), 8) for e in range(ne)]
        copies = [pltpu.make_async_copy(y_hbm.at[pl.ds(starts[e], w)], ybuf.at[sl, pl.ds(e * w, w)], sem.at[sl, e])
                  for e in range(ne)]
        return starts, copies

    @pl.when(step == 0)
    def _():
        for cp in window_copies(bounds(0)[0], 0, 0)[1]:
            cp.start()

    lo8, hi = bounds(step)
    starts0, copies0 = window_copies(lo8, 0, slot)
    for cp in copies0:
        cp.wait()

    @pl.when(step + 1 < nsteps)
    def _():
        for cp in window_copies(bounds(step + 1)[0], 0, 1 - slot)[1]:
            cp.start()

    pos = pos_ref[0]

    def contribution(r, starts):
        sub = lax.broadcasted_iota(I32, (w, tt), 0)
        want = [jnp.where(pos[e:e + 1, :] >= lo8[e] + r * w, pos[e:e + 1, :], -1) for e in range(ne)]
        blocks = [jnp.where(sub + starts[e] == want[e], 1.0, 0.0).astype(BF16) for e in range(ne)]
        onehot_t = jnp.concatenate(blocks, axis=0)
        return lax.dot_general(onehot_t, ybuf[slot], (((0,), (0,)), ((), ())), preferred_element_type=F32)

    acc_ref[...] = contribution(0, starts0)

    rounds = jnp.int32(1)
    for e in range(ne):
        rounds = jnp.maximum(rounds, (hi[e] - lo8[e] + w - 1) // w)

    def extra_round(r, carry):
        starts, copies = window_copies(lo8, r, slot)
        for cp in copies:
            cp.start()
        for cp in copies:
            cp.wait()
        acc_ref[...] += contribution(r, starts)
        return carry

    lax.fori_loop(1, rounds, extra_round, 0)
    x2 = x1_ref[0] + g2_ref[0] * acc_ref[...]
    o_ref[0] = _rms(x2) * fg_ref[...]


def _combine(offs_flat, pos, y, x1, g2, final_g, *, tt, w):
    b, s, d = x1.shape
    ne = pos.shape[1]
    total_rows = y.shape[0]
    return pl.pallas_call(
        functools.partial(_combine_kernel, tt=tt, w=w, total_rows=total_rows),
        grid_spec=pltpu.PrefetchScalarGridSpec(
            num_scalar_prefetch=1,
            grid=(b, s // tt),
            in_specs=[pl.BlockSpec((1, ne, tt), lambda bi, k, o: (bi, 0, k)),
                      pl.BlockSpec(memory_space=pl.ANY),
                      pl.BlockSpec((1, tt, d), lambda bi, k, o: (bi, k, 0)),
                      pl.BlockSpec((1, 1, d), lambda bi, k, o: (bi, 0, 0)),
                      pl.BlockSpec((1, d), lambda bi, k, o: (0, 0))],
            out_specs=pl.BlockSpec((1, tt, d), lambda bi, k, o: (bi, k, 0)),
            scratch_shapes=[pltpu.VMEM((2, ne * w, d), BF16), pltpu.VMEM((tt, d), F32),
                            pltpu.SemaphoreType.DMA((2, ne))]),
        out_shape=jax.ShapeDtypeStruct((b, s, d), F32),
        compiler_params=_params(("arbitrary", "arbitrary")),
        name="combine_final",
    )(offs_flat, pos, y, x1, g2, final_g)


def _rope_tables(s):
    n_rows = s // GRID_W
    row = np.repeat(np.arange(n_rows), GRID_W).astype(np.float64)
    col = np.tile(np.arange(GRID_W), n_rows).astype(np.float64)
    inv_freq = ROPE_THETA ** (-np.arange(ROPE_PAIRS, dtype=np.float64) / ROPE_PAIRS)
    ang_r = row[:, None] * inv_freq[None, :]
    ang_c = col[:, None] * inv_freq[None, :]
    cos64 = np.concatenate([np.cos(ang_r)] * 2 + [np.cos(ang_c)] * 2, axis=-1).astype(np.float32)
    sin64 = np.concatenate([-np.sin(ang_r), np.sin(ang_r), -np.sin(ang_c), np.sin(ang_c)], axis=-1).astype(np.float32)
    return cos64, sin64


ROPE_SWAP = np.concatenate([np.arange(16, 32), np.arange(0, 16), np.arange(48, 64), np.arange(32, 48)])


def kernel(x, c, ctx, c_ctx, w_mod, b_mod, norm1_g, norm2_g, w_in, conv_w, q_norm_g, w_uq, kv_norm_g, w_ukv, w_out,
           w_router, w_gate, w_up, w_down, final_g):
    b, s, d = x.shape
    lc = ctx.shape[1]
    assert w_in.shape[0] == 1, "single-layer stack"
    assert b <= 7 and s % max(ROW_TM, QKV_TM, ATT_TQ, ATT_TK, CMB_TT) == 0
    cap = EC_FACTOR * s // N_EXPERTS
    assert (b * cap) % FFN_T == 0 and cap % IDX_PC == 0

    cvec = jnp.zeros((8, d), F32).at[0:b].set(c).at[b].set(c_ctx)
    mod = _modulation(cvec, w_mod[0], b_mod[0][None, :])
    sh1, sc1, g1, sh2, sc2, g2 = [mod[0:b, None, i * d:(i + 1) * d] for i in range(6)]
    shc1, scc1 = mod[b:b + 1, None, 0:d], mod[b:b + 1, None, d:2 * d]

    n_main = 3 * CONV_WIDTH + Q_LORA + KV_LORA
    w_main = w_in[0, :, 0:n_main].astype(BF16)
    w_kr = w_in[0, :, n_main:]
    w_kr2 = jnp.concatenate([w_kr, w_kr[:, ROPE_SWAP]], axis=-1).astype(BF16)
    w_qn = w_uq[0, :, :, 0:QK_NOPE].reshape(Q_LORA, MLA_HEADS * QK_NOPE).astype(BF16)
    w_qr3 = w_uq[0, :, :, QK_NOPE:]
    w_qr = w_qr3.reshape(Q_LORA, MLA_HEADS * QK_ROPE).astype(BF16)
    w_qrs = w_qr3[:, :, ROPE_SWAP].reshape(Q_LORA, MLA_HEADS * QK_ROPE).astype(BF16)
    w_kn = w_ukv[0, :, :, 0:QK_NOPE].reshape(KV_LORA, MLA_HEADS * QK_NOPE).astype(BF16)
    w_v = w_ukv[0, :, :, QK_NOPE:].reshape(KV_LORA, MLA_HEADS * V_DIM).astype(BF16)
    w_o = w_out[0].astype(BF16)
    wr = jnp.zeros((d, LANES), F32).at[:, 0:N_EXPERTS].set(w_router[0])
    wr_hi = wr.astype(BF16)
    wr2 = jnp.concatenate([wr_hi, (wr - wr_hi.astype(F32)).astype(BF16)], axis=-1)

    cos64, sin64 = _rope_tables(s)
    cos2 = jnp.asarray(np.concatenate([cos64, cos64], axis=-1))
    sin2 = jnp.asarray(np.concatenate([sin64, sin64], axis=-1))
    ktab = jnp.asarray(np.concatenate([cos64, sin64], axis=-1))

    n1 = norm1_g[0][None, :]
    qg, kvg = q_norm_g[0][None, :], kv_norm_g[0][None, :]
    u, bg, cqn, ckvn, kr2 = _inproj(x, sc1, sh1, n1, w_main, w_kr2, qg, kvg, with_conv=True, tm=ROW_TM)
    _, ckvn_c, kr2_c = _inproj(ctx, scc1, shc1, n1, w_main, w_kr2, qg, kvg, with_conv=False, tm=lc)

    q = _qproj(cqn, cos2, sin2, w_qn, w_qr, w_qrs, tm=QKV_TM)
    k, v = _kvproj(ckvn, kr2, ktab, w_kn, w_v, tm=QKV_TM)
    kc, vc = _kvproj(ckvn_c, kr2_c, None, w_kn, w_v, tm=lc)
    y_attn = _attention(q, k, v, kc, vc, tq=ATT_TQ, tk=ATT_TK)

    x1, tab, aff_t = _outproj(u, bg, conv_w[0], y_attn, w_o, x, g1, norm2_g[0][None, :], sc2, sh2, wr2, tm=ROW_TM)

    pos, offs = _routing(aff_t, cap=cap, blk=CMB_TT)
    nk1 = s // CMB_TT + 1
    offs_flat = jnp.swapaxes(offs[:, :, 0:nk1], 1, 2).reshape(-1)
    idx_t = _slots(offs_flat, pos, cap=cap, blk=CMB_TT, w=CMB_W)
    idx_flat = jnp.transpose(idx_t, (2, 0, 1)).reshape(-1)
    y = _expert_ffn(idx_flat, tab.reshape(b * s, d + LANES), w_gate[0], w_up[0], w_down[0], t=FFN_T)
    return _combine(offs_flat, pos, y, x1, g2, final_g[None, :], tt=CMB_TT, w=CMB_W)
```

```python
import functools
import math

import jax
import jax.numpy as jnp
import numpy as np
from jax import lax
from jax.experimental import pallas as pl
from jax.experimental.pallas import tpu as pltpu

F32 = jnp.float32
BF16 = jnp.bfloat16
I32 = jnp.int32

GRID_W = 64
CONV_WIDTH = 1024
MLA_HEADS = 8
QK_NOPE = 128
QK_ROPE = 64
V_DIM = 128
Q_LORA = 512
KV_LORA = 512
QK_DIM = QK_NOPE + QK_ROPE
MLA_WIDTH = MLA_HEADS * V_DIM
V_EXT = 2 * V_DIM
N_EXPERTS = 16
EC_FACTOR = 2
ROPE_THETA = 10000.0
ROPE_PAIRS = QK_ROPE // 4
ATTN_SCALE = 1.0 / math.sqrt(QK_DIM)
LOG2E = math.log2(math.e)
EPS = 1e-6

LANES = 128
VMEM_LIMIT = 56 * 1024 * 1024
FFN_VMEM_LIMIT = 60 * 1024 * 1024

MOD_TN = 1024
ROW_TM = 512
QKV_TM = 1024
ATT_TQ = 1024
ATT_TK = 512
FFN_T = 256
CMB_TT = 256
CMB_W = 64
IDX_PC = 64


def _dot(a, b):
    return jnp.dot(a, b, preferred_element_type=F32)


def _params(sem, vmem=VMEM_LIMIT):
    return pltpu.CompilerParams(dimension_semantics=sem, vmem_limit_bytes=vmem)


def _resident(shape):
    nd = len(shape)
    return pl.BlockSpec(shape, lambda *_: (0,) * nd, pipeline_mode=pl.Buffered(1))


def _mod_kernel(c_ref, w_ref, b_ref, o_ref):
    cv = c_ref[...]
    a = cv / (1.0 + jnp.exp(-cv))
    a_hi = a.astype(BF16)
    a_lo = (a - a_hi.astype(F32)).astype(BF16)
    w = w_ref[...]
    w_hi = w.astype(BF16)
    w_lo = (w - w_hi.astype(F32)).astype(BF16)
    o_ref[...] = _dot(a_hi, w_hi) + _dot(a_lo, w_hi) + _dot(a_hi, w_lo) + b_ref[...]


def _modulation(cvec, w_mod, b_mod):
    d, n = w_mod.shape
    return pl.pallas_call(
        _mod_kernel,
        grid=(n // MOD_TN,),
        in_specs=[pl.BlockSpec((8, d), lambda j: (0, 0)),
                  pl.BlockSpec((d, MOD_TN), lambda j: (0, j)),
                  pl.BlockSpec((1, MOD_TN), lambda j: (0, j))],
        out_specs=pl.BlockSpec((8, MOD_TN), lambda j: (0, j)),
        out_shape=jax.ShapeDtypeStruct((8, n), F32),
        compiler_params=_params(("arbitrary",)),
        name="modulation",
    )(cvec, w_mod, b_mod)


def _rms(v):
    return v * lax.rsqrt(jnp.mean(v * v, axis=-1, keepdims=True) + EPS)


def _inproj_kernel(x_ref, sc_ref, sh_ref, g_ref, w_ref, wkr_ref, qg_ref, kvg_ref, *outs, with_conv):
    h = _rms(x_ref[0]) * g_ref[...]
    h = h * (1.0 + sc_ref[0]) + sh_ref[0]
    hb = h.astype(BF16)
    c = CONV_WIDTH
    if with_conv:
        u_ref, bg_ref, cq_ref, ckv_ref, kr_ref = outs
        xin = _dot(hb, w_ref[:, 0:c])
        cg = _dot(hb, w_ref[:, 2 * c:3 * c])
        u_ref[0] = (cg * xin).astype(BF16)
        bg_ref[0] = _dot(hb, w_ref[:, c:2 * c]).astype(BF16)
    else:
        cq_ref, ckv_ref, kr_ref = outs
    o = 3 * c
    cq = _dot(hb, w_ref[:, o:o + Q_LORA])
    cq_ref[0] = (_rms(cq) * qg_ref[...]).astype(BF16)
    ckv = _dot(hb, w_ref[:, o + Q_LORA:o + Q_LORA + KV_LORA])
    ckv_ref[0] = (_rms(ckv) * kvg_ref[...]).astype(BF16)
    kr_ref[0] = _dot(hb, wkr_ref[...])


def _inproj(x, scale, shift, gain, w_main, w_kr2, q_g, kv_g, *, with_conv, tm):
    b, s, d = x.shape
    per_batch = scale.shape[0] > 1
    vec = pl.BlockSpec((1, 1, d), (lambda bi, i: (bi, 0, 0)) if per_batch else (lambda bi, i: (0, 0, 0)))
    row = lambda n: pl.BlockSpec((1, tm, n), lambda bi, i: (bi, i, 0))
    outs, specs = [], []
    if with_conv:
        outs += [jax.ShapeDtypeStruct((b, s, CONV_WIDTH), BF16)] * 2
        specs += [row(CONV_WIDTH)] * 2
    outs += [jax.ShapeDtypeStruct((b, s, Q_LORA), BF16), jax.ShapeDtypeStruct((b, s, KV_LORA), BF16),
             jax.ShapeDtypeStruct((b, s, 2 * QK_ROPE), F32)]
    specs += [row(Q_LORA), row(KV_LORA), row(2 * QK_ROPE)]
    return pl.pallas_call(
        functools.partial(_inproj_kernel, with_conv=with_conv),
        grid=(b, s // tm),
        in_specs=[row(d), vec, vec, _resident((1, d)), _resident(w_main.shape), _resident(w_kr2.shape),
                  _resident((1, Q_LORA)), _resident((1, KV_LORA))],
        out_specs=specs,
        out_shape=outs,
        compiler_params=_params(("arbitrary", "arbitrary")),
        name="inproj_conv" if with_conv else "inproj_ctx",
    )(x, scale, shift, gain, w_main, w_kr2, q_g, kv_g)


def _qproj_kernel(cq_ref, cos_ref, sin_ref, wn_ref, wr_ref, wrs_ref, q_ref):
    cq = cq_ref[0]
    qn = _dot(cq, wn_ref[...])
    qr = _dot(cq, wr_ref[...])
    qrs = _dot(cq, wrs_ref[...])
    reps = MLA_HEADS * QK_ROPE // LANES
    cos = jnp.concatenate([cos_ref[...]] * reps, axis=-1)
    sin = jnp.concatenate([sin_ref[...]] * reps, axis=-1)
    qrot = qr * cos + qrs * sin
    sc = ATTN_SCALE * LOG2E
    for h in range(MLA_HEADS):
        q_ref[0, h, :, 0:QK_NOPE] = (qn[:, h * QK_NOPE:(h + 1) * QK_NOPE] * sc).astype(BF16)
        q_ref[0, h, :, QK_NOPE:QK_DIM] = (qrot[:, h * QK_ROPE:(h + 1) * QK_ROPE] * sc).astype(BF16)


def _qproj(cqn, cos2, sin2, w_qn, w_qr, w_qrs, *, tm):
    b, s, r = cqn.shape
    return pl.pallas_call(
        _qproj_kernel,
        grid=(b, s // tm),
        in_specs=[pl.BlockSpec((1, tm, r), lambda bi, i: (bi, i, 0)),
                  pl.BlockSpec((tm, LANES), lambda bi, i: (i, 0)),
                  pl.BlockSpec((tm, LANES), lambda bi, i: (i, 0)),
                  _resident(w_qn.shape), _resident(w_qr.shape), _resident(w_qrs.shape)],
        out_specs=pl.BlockSpec((1, MLA_HEADS, tm, QK_DIM), lambda bi, i: (bi, 0, i, 0)),
        out_shape=jax.ShapeDtypeStruct((b, MLA_HEADS, s, QK_DIM), BF16),
        compiler_params=_params(("arbitrary", "arbitrary")),
        name="qproj",
    )(cqn, cos2, sin2, w_qn, w_qr, w_qrs)


def _kvproj_kernel(ckv_ref, kr_ref, *rest, rope):
    if rope:
        tab_ref, wk_ref, wv_ref, k_ref, v_ref = rest
        prod = kr_ref[0] * tab_ref[...]
        krot = prod[:, 0:QK_ROPE] + prod[:, QK_ROPE:2 * QK_ROPE]
    else:
        wk_ref, wv_ref, k_ref, v_ref = rest
        krot = kr_ref[0][:, 0:QK_ROPE]
    ckv = ckv_ref[0]
    kn = _dot(ckv, wk_ref[...])
    vv = _dot(ckv, wv_ref[...])
    krot = krot.astype(BF16)
    for h in range(MLA_HEADS):
        k_ref[0, h, :, 0:QK_NOPE] = kn[:, h * QK_NOPE:(h + 1) * QK_NOPE].astype(BF16)
        k_ref[0, h, :, QK_NOPE:QK_DIM] = krot
        v_ref[0, h, :, 0:V_DIM] = vv[:, h * V_DIM:(h + 1) * V_DIM].astype(BF16)
        v_ref[0, h, :, V_DIM:V_EXT] = jnp.ones((vv.shape[0], V_EXT - V_DIM), BF16)


def _kvproj(ckvn, kr2, tab, w_kn, w_v, *, tm):
    b, s, r = ckvn.shape
    rope = tab is not None
    ins = [ckvn, kr2]
    specs = [pl.BlockSpec((1, tm, r), lambda bi, i: (bi, i, 0)),
             pl.BlockSpec((1, tm, 2 * QK_ROPE), lambda bi, i: (bi, i, 0))]
    if rope:
        ins.append(tab)
        specs.append(pl.BlockSpec((tm, 2 * QK_ROPE), lambda bi, i: (i, 0)))
    ins += [w_kn, w_v]
    specs += [_resident(w_kn.shape), _resident(w_v.shape)]
    return pl.pallas_call(
        functools.partial(_kvproj_kernel, rope=rope),
        grid=(b, s // tm),
        in_specs=specs,
        out_specs=[pl.BlockSpec((1, MLA_HEADS, tm, QK_DIM), lambda bi, i: (bi, 0, i, 0)),
                   pl.BlockSpec((1, MLA_HEADS, tm, V_EXT), lambda bi, i: (bi, 0, i, 0))],
        out_shape=[jax.ShapeDtypeStruct((b, MLA_HEADS, s, QK_DIM), BF16),
                   jax.ShapeDtypeStruct((b, MLA_HEADS, s, V_EXT), BF16)],
        compiler_params=_params(("arbitrary", "arbitrary")),
        name="kvproj_rope" if rope else "kvproj_ctx",
    )(*ins)


def _attn_kernel(q_ref, k_ref, v_ref, kc_ref, vc_ref, o_ref, m_ref, acc_ref, sa_ref, sb_ref, *, tk):
    q = q_ref[0, 0]
    m_ref[...] = jnp.full(m_ref.shape, -jnp.inf, F32)
    acc_ref[...] = jnp.zeros(acc_ref.shape, F32)

    def lanes(a, n):
        return jnp.concatenate([a] * (n // LANES), axis=1)

    def scores(k):
        return lax.dot_general(q, k, (((1,), (1,)), ((), ())), preferred_element_type=F32)

    def update(s, v):
        m_old = m_ref[...]
        m_new = jnp.maximum(m_old, jnp.max(s, axis=-1, keepdims=True))
        alpha = jnp.exp2(m_old - m_new)
        p = jnp.exp2(s - lanes(m_new, s.shape[1]))
        acc_ref[...] = lanes(alpha, acc_ref.shape[1]) * acc_ref[...] + _dot(p.astype(BF16), v)
        m_ref[...] = m_new

    def chunk(ref, j):
        return ref[0, 0, pl.ds(pl.multiple_of(j * tk, tk), tk), :]

    n = k_ref.shape[2] // tk
    bufs = (sa_ref, sb_ref)
    sa_ref[...] = scores(chunk(k_ref, 0))
    for j in range(n):
        nxt = bufs[(j + 1) % 2]
        if j + 1 < n:
            nxt[...] = scores(chunk(k_ref, j + 1))
        else:
            s_ctx = scores(kc_ref[0, 0])
        update(bufs[j % 2][...], chunk(v_ref, j))
    update(s_ctx, vc_ref[0, 0])
    o_ref[0] = (acc_ref[:, 0:V_DIM] / acc_ref[:, V_DIM:V_EXT]).astype(BF16)


def _attention(q, k, v, kc, vc, *, tq, tk):
    b, h, s, dq = q.shape
    lc = kc.shape[2]
    return pl.pallas_call(
        functools.partial(_attn_kernel, tk=tk),
        grid=(b, h, s // tq),
        in_specs=[pl.BlockSpec((1, 1, tq, dq), lambda bi, hi, i: (bi, hi, i, 0)),
                  pl.BlockSpec((1, 1, s, dq), lambda bi, hi, i: (bi, hi, 0, 0)),
                  pl.BlockSpec((1, 1, s, V_EXT), lambda bi, hi, i: (bi, hi, 0, 0)),
                  pl.BlockSpec((1, 1, lc, dq), lambda bi, hi, i: (bi, hi, 0, 0)),
                  pl.BlockSpec((1, 1, lc, V_EXT), lambda bi, hi, i: (bi, hi, 0, 0))],
        out_specs=pl.BlockSpec((1, tq, V_DIM), lambda bi, hi, i: (bi, i, hi)),
        out_shape=jax.ShapeDtypeStruct((b, s, h * V_DIM), BF16),
        scratch_shapes=[pltpu.VMEM((tq, LANES), F32), pltpu.VMEM((tq, V_EXT), F32),
                        pltpu.VMEM((tq, tk), F32), pltpu.VMEM((tq, tk), F32)],
        compiler_params=_params(("arbitrary", "arbitrary", "arbitrary")),
        name="attention",
    )(q, k, v, kc, vc)


HALO = 16


def _outproj_kernel(u_ref, up_ref, un_ref, bg_ref, cw_ref, ya_ref, wo_ref, x_ref, g1_ref, n2_ref, sc_ref, sh_ref,
                    wr2_ref, x1_ref, tab_ref, afft_ref, *, tm, d):
    i = pl.program_id(1)
    last = pl.num_programs(1) - 1
    u = u_ref[0].astype(F32)
    rows = lax.broadcasted_iota(I32, (tm, 1), 0)
    prev_row = jnp.where(i > 0, up_ref[0][HALO - 1:HALO, :].astype(F32), 0.0)
    next_row = jnp.where(i < last, un_ref[0][0:1, :].astype(F32), 0.0)
    um1 = jnp.where(rows == 0, prev_row, pltpu.roll(u, 1, 0))
    up1 = jnp.where(rows == tm - 1, next_row, pltpu.roll(u, tm - 1, 0))
    cw = cw_ref[...]
    yc = (bg_ref[0].astype(F32) * (um1 * cw[0:1] + u * cw[1:2] + up1 * cw[2:3])).astype(BF16)
    nsub = 2
    tr = tm // nsub
    for r in range(nsub):
        rs = slice(r * tr, (r + 1) * tr)
        y = _dot(yc[rs], wo_ref[0:CONV_WIDTH, :]) + _dot(ya_ref[0, rs, :], wo_ref[CONV_WIDTH:, :])
        x1 = x_ref[0, rs, :] + g1_ref[0] * y
        x1_ref[0, rs, :] = x1
        h2 = _rms(x1) * n2_ref[...]
        h2 = h2 * (1.0 + sc_ref[0]) + sh_ref[0]
        tab_ref[0, rs, 0:d] = h2
        h_hi = h2.astype(BF16)
        h_lo = (h2 - h_hi.astype(F32)).astype(BF16)
        l2 = _dot(h_hi, wr2_ref[...])
        logits = l2[:, 0:LANES] + l2[:, LANES:2 * LANES] + _dot(h_lo, wr2_ref[:, 0:LANES])
        lane = lax.broadcasted_iota(I32, logits.shape, 1)
        logits = jnp.where(lane < N_EXPERTS, logits, -jnp.inf)
        e = jnp.exp(logits - jnp.max(logits, axis=-1, keepdims=True))
        aff = e / jnp.sum(e, axis=-1, keepdims=True)
        tab_ref[0, rs, d:d + LANES] = aff
        afft_ref[0, :, rs] = aff.T[0:N_EXPERTS, :]


def _outproj(u, bg, conv_w, y_attn, w_out, x, g1, n2, sc2, sh2, wr2, *, tm):
    b, s, d = x.shape
    nh = tm // HALO
    nhb = s // HALO
    row = lambda n: pl.BlockSpec((1, tm, n), lambda bi, i: (bi, i, 0))
    vec = pl.BlockSpec((1, 1, d), lambda bi, i: (bi, 0, 0))
    return pl.pallas_call(
        functools.partial(_outproj_kernel, tm=tm, d=d),
        grid=(b, s // tm),
        in_specs=[row(CONV_WIDTH),
                  pl.BlockSpec((1, HALO, CONV_WIDTH), lambda bi, i: (bi, jnp.maximum(i * nh - 1, 0), 0)),
                  pl.BlockSpec((1, HALO, CONV_WIDTH), lambda bi, i: (bi, jnp.minimum((i + 1) * nh, nhb - 1), 0)),
                  row(CONV_WIDTH), _resident(conv_w.shape), row(MLA_WIDTH), _resident(w_out.shape), row(d),
                  vec, _resident((1, d)), vec, vec, _resident(wr2.shape)],
        out_specs=[row(d), row(d + LANES), pl.BlockSpec((1, N_EXPERTS, tm), lambda bi, i: (bi, 0, i))],
        out_shape=[jax.ShapeDtypeStruct((b, s, d), F32), jax.ShapeDtypeStruct((b, s, d + LANES), F32),
                   jax.ShapeDtypeStruct((b, N_EXPERTS, s), F32)],
        compiler_params=_params(("arbitrary", "arbitrary")),
        name="outproj_router",
    )(u, u, u, bg, conv_w, y_attn, w_out, x, g1, n2, sc2, sh2, wr2)


def _routing_kernel(aff_ref, tri_ref, pos_ref, offs_ref, *, cap, blk, rows_per_expert):
    bi = pl.program_id(0)
    aff = aff_ref[0]
    ne, s = aff.shape
    capf = jnp.float32(cap)

    def bisect(t, prefix):
        cand = prefix | jnp.left_shift(jnp.int32(1), 30 - t)
        cnt = jnp.sum(jnp.where(aff >= lax.bitcast_convert_type(cand, F32), 1.0, 0.0), axis=1, keepdims=True)
        return jnp.where(cnt >= capf, cand, prefix)

    floor_bits = lax.fori_loop(0, 31, bisect, jnp.zeros((ne, 1), I32))
    thr = jnp.min(jnp.where(aff >= lax.bitcast_convert_type(floor_bits, F32), aff, jnp.inf), axis=1, keepdims=True)
    gt = aff > thr
    eq = aff == thr
    need = capf - jnp.sum(jnp.where(gt, 1.0, 0.0), axis=1, keepdims=True)
    tri = tri_ref[...]

    def cumsum_blocks(mask_f32):
        run = jnp.zeros((ne, 1), F32)
        parts, starts = [], []
        for kb in range(s // blk):
            c = _dot(mask_f32[:, kb * blk:(kb + 1) * blk].astype(BF16), tri)
            starts.append(run)
            parts.append(c + run)
            run = run + c[:, blk - 1:blk]
        return jnp.concatenate(parts, axis=1), starts, run

    eqf = jnp.where(eq, 1.0, 0.0)
    eq_incl, _, _ = cumsum_blocks(eqf)
    sel = jnp.where(gt, 1.0, jnp.where(eq & (eq_incl - eqf < need), 1.0, 0.0))
    incl, starts, _ = cumsum_blocks(sel)

    base = (lax.broadcasted_iota(I32, (ne, 1), 0) * rows_per_expert + bi * cap)
    pos_ref[0] = jnp.where(sel > 0.0, (incl - sel).astype(I32) + base, -1)
    lane = lax.broadcasted_iota(I32, (ne, LANES), 1)
    offs = jnp.full((ne, LANES), cap, I32) + base
    for kb, st in enumerate(starts):
        offs = jnp.where(lane == kb, st.astype(I32) + base, offs)
    offs_ref[0] = offs


def _routing(aff_t, *, cap, blk):
    b, ne, s = aff_t.shape
    assert s // blk + 1 <= LANES
    tri = jnp.asarray(np.triu(np.ones((blk, blk), np.float32)), BF16)
    return pl.pallas_call(
        functools.partial(_routing_kernel, cap=cap, blk=blk, rows_per_expert=b * cap),
        grid=(b,),
        in_specs=[pl.BlockSpec((1, ne, s), lambda bi: (bi, 0, 0)), _resident(tri.shape)],
        out_specs=[pl.BlockSpec((1, ne, s), lambda bi: (bi, 0, 0)),
                   pl.BlockSpec((1, ne, LANES), lambda bi: (bi, 0, 0))],
        out_shape=[jax.ShapeDtypeStruct((b, ne, s), I32), jax.ShapeDtypeStruct((b, ne, LANES), I32)],
        compiler_params=_params(("arbitrary",)),
        name="routing",
    )(aff_t, tri)


def _slots_kernel(offs_ref, pos_ref, idx_ref, *, cap, blk, w, rows_per_expert):
    bi = pl.program_id(0)
    ne, s = pos_ref.shape[1], pos_ref.shape[2]
    nb = s // blk
    idx_ref[...] = jnp.zeros(idx_ref.shape, I32)
    sub = lax.broadcasted_iota(I32, (w, blk), 0)
    tok = lax.broadcasted_iota(I32, (w, blk), 1)

    def block(kb, carry):
        tok1 = tok + (kb * blk + bi * s + 1)
        windows, trips = [], []
        for e in range(ne):
            base = e * rows_per_expert + bi * cap
            o = (bi * (nb + 1) + kb) * ne + e
            lo8 = (offs_ref[o] - base) // 8 * 8
            hi = offs_ref[o + ne] - base
            posrow = pos_ref[0, e:e + 1, pl.ds(pl.multiple_of(kb * blk, blk), blk)] - base

            def window(r, c, e=e, lo8=lo8, posrow=posrow):
                start = pl.multiple_of(jnp.minimum(lo8 + r * w, cap - w), 8)
                hit = jnp.where(sub + start == posrow, tok1, 0)
                val = jnp.sum(hit.astype(F32), axis=1, keepdims=True).astype(I32)
                old = idx_ref[0, pl.ds(start, w), e:e + 1]
                idx_ref[0, pl.ds(start, w), e:e + 1] = jnp.where(val > 0, val - 1, old)
                return c

            windows.append(window)
            trips.append((hi - lo8 + w - 1) // w)
        for window in windows:
            window(0, 0)
        for window, n in zip(windows, trips):
            lax.fori_loop(1, n, window, 0)
        return carry

    lax.fori_loop(0, nb, block, 0)


def _slots(offs_flat, pos, *, cap, blk, w):
    b, ne, s = pos.shape
    return pl.pallas_call(
        functools.partial(_slots_kernel, cap=cap, blk=blk, w=w, rows_per_expert=b * cap),
        grid_spec=pltpu.PrefetchScalarGridSpec(
            num_scalar_prefetch=1,
            grid=(b,),
            in_specs=[pl.BlockSpec((1, ne, s), lambda bi, o: (bi, 0, 0))],
            out_specs=pl.BlockSpec((1, cap, ne), lambda bi, o: (bi, 0, 0))),
        out_shape=jax.ShapeDtypeStruct((b, cap, ne), I32),
        compiler_params=_params(("arbitrary",)),
        name="slots",
    )(offs_flat, pos)


def _ffn_kernel(idx_ref, tab_hbm, wg_hbm, wu_hbm, wd_hbm, y_ref, buf_a, buf_b, buf_c, wg_ref, wu_ref, wd_ref,
                sg_ref, su_ref, sd_ref, sem, wsem, *, t, d):
    e = pl.program_id(0)
    j = pl.program_id(1)
    ne, nt = pl.num_programs(0), pl.num_programs(1)
    step = e * nt + j
    nsteps = ne * nt
    rc, fc = sg_ref.shape[0], sd_ref.shape[0]

    def row_copy(base, i, buf, sl):
        return pltpu.make_async_copy(tab_hbm.at[pl.ds(idx_ref[base + i], 1)], buf.at[pl.ds(i, 1)], sem.at[sl])

    def wait_rows(buf, sl):
        pltpu.make_async_copy(tab_hbm.at[pl.ds(0, t)], buf, sem.at[sl]).wait()

    def piece_copies(ex, pc):
        return [pltpu.make_async_copy(wg_hbm.at[ex, pl.ds(pl.multiple_of(pc * rc, rc), rc)], sg_ref, wsem.at[0]),
                pltpu.make_async_copy(wu_hbm.at[ex, pl.ds(pl.multiple_of(pc * rc, rc), rc)], su_ref, wsem.at[1]),
                pltpu.make_async_copy(wd_hbm.at[ex, pl.ds(pl.multiple_of(pc * fc, fc), fc)], sd_ref, wsem.at[2])]

    def cast_piece(st, pc):
        wg_ref[st, pl.ds(pl.multiple_of(pc * rc, rc), rc), :] = sg_ref[...].astype(BF16)
        wu_ref[st, pl.ds(pl.multiple_of(pc * rc, rc), rc), :] = su_ref[...].astype(BF16)
        wd_ref[st, pl.ds(pl.multiple_of(pc * fc, fc), fc), :] = sd_ref[...].astype(BF16)

    @pl.when(step == 0)
    def _():
        def issue(i, carry):
            row_copy(0, i, buf_a, 0).start()
            row_copy(jnp.minimum(1, nsteps - 1) * t, i, buf_b, 1).start()
            return carry
        lax.fori_loop(0, t, issue, 0)

        def load(pc, carry):
            cps = piece_copies(0, pc)
            for cp in cps:
                cp.start()
            for cp in cps:
                cp.wait()
            cast_piece(0, pc)
            return carry
        lax.fori_loop(0, nt - 1, load, 0)
        for cp in piece_copies(0, nt - 1):
            cp.start()

    pe = jnp.where(j > 0, e, e - 1)
    pj = jnp.where(j > 0, j - 1, nt - 1)
    nxt_e = jnp.minimum(e + 1, ne - 1)
    ws = e % 2

    def run(cur, cur_sl, mid, mid_sl, far, far_sl):
        wait_rows(cur, cur_sl)
        for cp in piece_copies(jnp.minimum(pe + 1, ne - 1), pj):
            cp.wait()
        cast_piece((pe + 1) % 2, pj)
        for cp in piece_copies(nxt_e, j):
            cp.start()
        base = jnp.minimum(step + 2, nsteps - 1) * t

        @pl.when(idx_ref[base] < 0)
        def _():
            y_ref[0:16, 0:LANES] = jnp.zeros((16, LANES), BF16)

        for i in range(t):
            row_copy(base, i, far, far_sl).start()
        xb = cur[:, 0:d].astype(BF16)
        aff = cur[:, d:d + LANES]
        lane = lax.broadcasted_iota(I32, aff.shape, 1)
        gate_w = jnp.sum(jnp.where(lane == e, aff, 0.0), axis=-1, keepdims=True)
        g = _dot(xb, wg_ref[ws])
        up = _dot(xb, wu_ref[ws])
        hid = ((g / (1.0 + jnp.exp(-g))) * up).astype(BF16)
        y = _dot(hid, wd_ref[ws])
        y_ref[...] = (y * gate_w).astype(BF16)

        @pl.when(step == nsteps - 1)
        def _():
            wait_rows(mid, mid_sl)
            wait_rows(far, far_sl)
            for cp in piece_copies(nxt_e, j):
                cp.wait()

    bufs = (buf_a, buf_b, buf_c)
    for r in range(3):
        @pl.when(step % 3 == r)
        def _(r=r):
            run(bufs[r], r, bufs[(r + 1) % 3], (r + 1) % 3, bufs[(r + 2) % 3], (r + 2) % 3)


def _expert_ffn(idx_flat, tab, w_gate, w_up, w_down, *, t):
    ne, d, ff = w_gate.shape
    rows = idx_flat.shape[0]
    nt = rows // ne // t
    dw = tab.shape[1]
    bf16_rows = 16
    assert d % (nt * bf16_rows) == 0 and ff % (nt * bf16_rows) == 0
    return pl.pallas_call(
        functools.partial(_ffn_kernel, t=t, d=d),
        grid_spec=pltpu.PrefetchScalarGridSpec(
            num_scalar_prefetch=1,
            grid=(ne, nt),
            in_specs=[pl.BlockSpec(memory_space=pl.ANY)] * 4,
            out_specs=pl.BlockSpec((t, d), lambda e, j, idx: (e * nt + j, 0)),
            scratch_shapes=[pltpu.VMEM((t, dw), F32), pltpu.VMEM((t, dw), F32), pltpu.VMEM((t, dw), F32),
                            pltpu.VMEM((2, d, ff), BF16), pltpu.VMEM((2, d, ff), BF16), pltpu.VMEM((2, ff, d), BF16),
                            pltpu.VMEM((d // nt, ff), F32), pltpu.VMEM((d // nt, ff), F32),
                            pltpu.VMEM((ff // nt, d), F32),
                            pltpu.SemaphoreType.DMA((3,)), pltpu.SemaphoreType.DMA((3,))]),
        out_shape=jax.ShapeDtypeStruct((rows, d), BF16),
        compiler_params=_params(("arbitrary", "arbitrary"), vmem=FFN_VMEM_LIMIT),
        name="expert_ffn",
    )(idx_flat, tab, w_gate, w_up, w_down)


def _combine_kernel(offs_ref, pos_ref, y_hbm, x1_ref, g2_ref, fg_ref, o_ref, ybuf, acc_ref, sem, *, tt, w, total_rows):
    ne = pos_ref.shape[1]
    nk = pl.num_programs(1)
    step = pl.program_id(0) * nk + pl.program_id(1)
    nsteps = pl.num_programs(0) * nk
    slot = step % 2

    def bounds(st):
        o0 = (st // nk * (nk + 1) + st % nk) * ne
        lo8 = [(offs_ref[o0 + e] // 8) * 8 for e in range(ne)]
        hi = [offs_ref[o0 + ne + e] for e in range(ne)]
        return lo8, hi

    def window_copies(lo8, r, sl):
        starts = [pl.multiple_of(jnp.minimum(lo8[e] + r * w, total_rows - w), 8) for e in range(ne)]
        copies = [pltpu.make_async_copy(y_hbm.at[pl.ds(starts[e], w)], ybuf.at[sl, pl.ds(e * w, w)], sem.at[sl, e])
                  for e in range(ne)]
        return starts, copies

    @pl.when(step == 0)
    def _():
        for cp in window_copies(bounds(0)[0], 0, 0)[1]:
            cp.start()

    lo8, hi = bounds(step)
    starts0, copies0 = window_copies(lo8, 0, slot)
    for cp in copies0:
        cp.wait()

    @pl.when(step + 1 < nsteps)
    def _():
        for cp in window_copies(bounds(step + 1)[0], 0, 1 - slot)[1]:
            cp.start()

    pos = pos_ref[0]

    def contribution(r, starts):
        sub = lax.broadcasted_iota(I32, (w, tt), 0)
        want = [jnp.where(pos[e:e + 1, :] >= lo8[e] + r * w, pos[e:e + 1, :], -1) for e in range(ne)]
        blocks = [jnp.where(sub + starts[e] == want[e], 1.0, 0.0).astype(BF16) for e in range(ne)]
        onehot_t = jnp.concatenate(blocks, axis=0)
        return lax.dot_general(onehot_t, ybuf[slot], (((0,), (0,)), ((), ())), preferred_element_type=F32)

    acc_ref[...] = contribution(0, starts0)

    rounds = jnp.int32(1)
    for e in range(ne):
        rounds = jnp.maximum(rounds, (hi[e] - lo8[e] + w - 1) // w)

    def extra_round(r, carry):
        starts, copies = window_copies(lo8, r, slot)
        for cp in copies:
            cp.start()
        for cp in copies:
            cp.wait()
        acc_ref[...] += contribution(r, starts)
        return carry

    lax.fori_loop(1, rounds, extra_round, 0)
    x2 = x1_ref[0] + g2_ref[0] * acc_ref[...]
    o_ref[0] = _rms(x2) * fg_ref[...]


def _combine(offs_flat, pos, y, x1, g2, final_g, *, tt, w):
    b, s, d = x1.shape
    ne = pos.shape[1]
    total_rows = y.shape[0]
    return pl.pallas_call(
        functools.partial(_combine_kernel, tt=tt, w=w, total_rows=total_rows),
        grid_spec=pltpu.PrefetchScalarGridSpec(
            num_scalar_prefetch=1,
            grid=(b, s // tt),
            in_specs=[pl.BlockSpec((1, ne, tt), lambda bi, k, o: (bi, 0, k)),
                      pl.BlockSpec(memory_space=pl.ANY),
                      pl.BlockSpec((1, tt, d), lambda bi, k, o: (bi, k, 0)),
                      pl.BlockSpec((1, 1, d), lambda bi, k, o: (bi, 0, 0)),
                      pl.BlockSpec((1, d), lambda bi, k, o: (0, 0))],
            out_specs=pl.BlockSpec((1, tt, d), lambda bi, k, o: (bi, k, 0)),
            scratch_shapes=[pltpu.VMEM((2, ne * w, d), BF16), pltpu.VMEM((tt, d), F32),
                            pltpu.SemaphoreType.DMA((2, ne))]),
        out_shape=jax.ShapeDtypeStruct((b, s, d), F32),
        compiler_params=_params(("arbitrary", "arbitrary")),
        name="combine_final",
    )(offs_flat, pos, y, x1, g2, final_g)


def _rope_tables(s):
    n_rows = s // GRID_W
    row = np.repeat(np.arange(n_rows), GRID_W).astype(np.float64)
    col = np.tile(np.arange(GRID_W), n_rows).astype(np.float64)
    inv_freq = ROPE_THETA ** (-np.arange(ROPE_PAIRS, dtype=np.float64) / ROPE_PAIRS)
    ang_r = row[:, None] * inv_freq[None, :]
    ang_c = col[:, None] * inv_freq[None, :]
    cos64 = np.concatenate([np.cos(ang_r)] * 2 + [np.cos(ang_c)] * 2, axis=-1).astype(np.float32)
    sin64 = np.concatenate([-np.sin(ang_r), np.sin(ang_r), -np.sin(ang_c), np.sin(ang_c)], axis=-1).astype(np.float32)
    return cos64, sin64


ROPE_SWAP = np.concatenate([np.arange(16, 32), np.arange(0, 16), np.arange(48, 64), np.arange(32, 48)])


def kernel(x, c, ctx, c_ctx, w_mod, b_mod, norm1_g, norm2_g, w_in, conv_w, q_norm_g, w_uq, kv_norm_g, w_ukv, w_out,
           w_router, w_gate, w_up, w_down, final_g):
    b, s, d = x.shape
    lc = ctx.shape[1]
    assert w_in.shape[0] == 1, "single-layer stack"
    assert b <= 7 and s % max(ROW_TM, QKV_TM, ATT_TQ, ATT_TK, CMB_TT) == 0
    cap = EC_FACTOR * s // N_EXPERTS
    assert (b * cap) % FFN_T == 0 and cap % IDX_PC == 0

    cvec = jnp.zeros((8, d), F32).at[0:b].set(c).at[b].set(c_ctx)
    mod = _modulation(cvec, w_mod[0], b_mod[0][None, :])
    sh1, sc1, g1, sh2, sc2, g2 = [mod[0:b, None, i * d:(i + 1) * d] for i in range(6)]
    shc1, scc1 = mod[b:b + 1, None, 0:d], mod[b:b + 1, None, d:2 * d]

    n_main = 3 * CONV_WIDTH + Q_LORA + KV_LORA
    w_main = w_in[0, :, 0:n_main].astype(BF16)
    w_kr = w_in[0, :, n_main:]
    w_kr2 = jnp.concatenate([w_kr, w_kr[:, ROPE_SWAP]], axis=-1).astype(BF16)
    w_qn = w_uq[0, :, :, 0:QK_NOPE].reshape(Q_LORA, MLA_HEADS * QK_NOPE).astype(BF16)
    w_qr3 = w_uq[0, :, :, QK_NOPE:]
    w_qr = w_qr3.reshape(Q_LORA, MLA_HEADS * QK_ROPE).astype(BF16)
    w_qrs = w_qr3[:, :, ROPE_SWAP].reshape(Q_LORA, MLA_HEADS * QK_ROPE).astype(BF16)
    w_kn = w_ukv[0, :, :, 0:QK_NOPE].reshape(KV_LORA, MLA_HEADS * QK_NOPE).astype(BF16)
    w_v = w_ukv[0, :, :, QK_NOPE:].reshape(KV_LORA, MLA_HEADS * V_DIM).astype(BF16)
    w_o = w_out[0].astype(BF16)
    wr = jnp.zeros((d, LANES), F32).at[:, 0:N_EXPERTS].set(w_router[0])
    wr_hi = wr.astype(BF16)
    wr2 = jnp.concatenate([wr_hi, (wr - wr_hi.astype(F32)).astype(BF16)], axis=-1)

    cos64, sin64 = _rope_tables(s)
    cos2 = jnp.asarray(np.concatenate([cos64, cos64], axis=-1))
    sin2 = jnp.asarray(np.concatenate([sin64, sin64], axis=-1))
    ktab = jnp.asarray(np.concatenate([cos64, sin64], axis=-1))

    n1 = norm1_g[0][None, :]
    qg, kvg = q_norm_g[0][None, :], kv_norm_g[0][None, :]
    u, bg, cqn, ckvn, kr2 = _inproj(x, sc1, sh1, n1, w_main, w_kr2, qg, kvg, with_conv=True, tm=ROW_TM)
    _, ckvn_c, kr2_c = _inproj(ctx, scc1, shc1, n1, w_main, w_kr2, qg, kvg, with_conv=False, tm=lc)

    q = _qproj(cqn, cos2, sin2, w_qn, w_qr, w_qrs, tm=QKV_TM)
    k, v = _kvproj(ckvn, kr2, ktab, w_kn, w_v, tm=QKV_TM)
    kc, vc = _kvproj(ckvn_c, kr2_c, None, w_kn, w_v, tm=lc)
    y_attn = _attention(q, k, v, kc, vc, tq=ATT_TQ, tk=ATT_TK)

    x1, tab, aff_t = _outproj(u, bg, conv_w[0], y_attn, w_o, x, g1, norm2_g[0][None, :], sc2, sh2, wr2, tm=ROW_TM)

    pos, offs = _routing(aff_t, cap=cap, blk=CMB_TT)
    nk1 = s // CMB_TT + 1
    offs_flat = jnp.swapaxes(offs[:, :, 0:nk1], 1, 2).reshape(-1)
    idx_t = _slots(offs_flat, pos, cap=cap, blk=CMB_TT, w=CMB_W)
    idx_flat = jnp.transpose(idx_t, (2, 0, 1)).reshape(-1)
    y = _expert_ffn(idx_flat, tab.reshape(b * s, d + LANES), w_gate[0], w_up[0], w_down[0], t=FFN_T)
    return _combine(offs_flat, pos, y, x1, g2, final_g[None, :], tt=CMB_TT, w=CMB_W)
```

```python
import functools
import math

import jax
import jax.numpy as jnp
import numpy as np
from jax import lax
from jax.experimental import pallas as pl
from jax.experimental.pallas import tpu as pltpu

F32 = jnp.float32
BF16 = jnp.bfloat16
I32 = jnp.int32

GRID_W = 64
CONV_WIDTH = 1024
MLA_HEADS = 8
QK_NOPE = 128
QK_ROPE = 64
V_DIM = 128
Q_LORA = 512
KV_LORA = 512
QK_DIM = QK_NOPE + QK_ROPE
MLA_WIDTH = MLA_HEADS * V_DIM
V_EXT = 2 * V_DIM
N_EXPERTS = 16
EC_FACTOR = 2
ROPE_THETA = 10000.0
ROPE_PAIRS = QK_ROPE // 4
ATTN_SCALE = 1.0 / math.sqrt(QK_DIM)
LOG2E = math.log2(math.e)
EPS = 1e-6

LANES = 128
VMEM_LIMIT = 56 * 1024 * 1024
FFN_VMEM_LIMIT = 60 * 1024 * 1024

MOD_TN = 1024
ROW_TM = 512
QKV_TM = 1024
ATT_TQ = 1024
ATT_TK = 512
FFN_T = 256
FFN_HEAD_ROWS = 96
CMB_TT = 256
CMB_W = 64
IDX_PC = 64


def _dot(a, b):
    return jnp.dot(a, b, preferred_element_type=F32)


def _params(sem, vmem=VMEM_LIMIT):
    return pltpu.CompilerParams(dimension_semantics=sem, vmem_limit_bytes=vmem)


def _resident(shape):
    nd = len(shape)
    return pl.BlockSpec(shape, lambda *_: (0,) * nd, pipeline_mode=pl.Buffered(1))


def _mod_kernel(c_ref, w_ref, b_ref, o_ref):
    cv = c_ref[...]
    a = cv / (1.0 + jnp.exp(-cv))
    a_hi = a.astype(BF16)
    a_lo = (a - a_hi.astype(F32)).astype(BF16)
    w = w_ref[...].astype(BF16)
    o_ref[...] = _dot(a_hi, w) + _dot(a_lo, w) + b_ref[...]


def _modulation(cvec, w_mod, b_mod):
    d, n = w_mod.shape
    return pl.pallas_call(
        _mod_kernel,
        grid=(n // MOD_TN,),
        in_specs=[pl.BlockSpec((8, d), lambda j: (0, 0)),
                  pl.BlockSpec((d, MOD_TN), lambda j: (0, j)),
                  pl.BlockSpec((1, MOD_TN), lambda j: (0, j))],
        out_specs=pl.BlockSpec((8, MOD_TN), lambda j: (0, j)),
        out_shape=jax.ShapeDtypeStruct((8, n), F32),
        compiler_params=_params(("arbitrary",)),
        name="modulation",
    )(cvec, w_mod, b_mod)


def _rms(v):
    return v * lax.rsqrt(jnp.mean(v * v, axis=-1, keepdims=True) + EPS)


def _inproj_kernel(x_ref, sc_ref, sh_ref, g_ref, w_ref, wkr_ref, qg_ref, kvg_ref, *outs, with_conv):
    h = _rms(x_ref[0]) * g_ref[...]
    h = h * (1.0 + sc_ref[0]) + sh_ref[0]
    hb = h.astype(BF16)
    c = CONV_WIDTH
    if with_conv:
        u_ref, bg_ref, cq_ref, ckv_ref, kr_ref = outs
        xin = _dot(hb, w_ref[:, 0:c])
        cg = _dot(hb, w_ref[:, 2 * c:3 * c])
        u_ref[0] = (cg * xin).astype(BF16)
        bg_ref[0] = _dot(hb, w_ref[:, c:2 * c]).astype(BF16)
    else:
        cq_ref, ckv_ref, kr_ref = outs
    o = 3 * c
    cq = _dot(hb, w_ref[:, o:o + Q_LORA])
    cq_ref[0] = (_rms(cq) * qg_ref[...]).astype(BF16)
    ckv = _dot(hb, w_ref[:, o + Q_LORA:o + Q_LORA + KV_LORA])
    ckv_ref[0] = (_rms(ckv) * kvg_ref[...]).astype(BF16)
    kr_ref[0] = _dot(hb, wkr_ref[...])


def _inproj(x, scale, shift, gain, w_main, w_kr2, q_g, kv_g, *, with_conv, tm):
    b, s, d = x.shape
    per_batch = scale.shape[0] > 1
    vec = pl.BlockSpec((1, 1, d), (lambda bi, i: (bi, 0, 0)) if per_batch else (lambda bi, i: (0, 0, 0)))
    row = lambda n: pl.BlockSpec((1, tm, n), lambda bi, i: (bi, i, 0))
    outs, specs = [], []
    if with_conv:
        outs += [jax.ShapeDtypeStruct((b, s, CONV_WIDTH), BF16)] * 2
        specs += [row(CONV_WIDTH)] * 2
    outs += [jax.ShapeDtypeStruct((b, s, Q_LORA), BF16), jax.ShapeDtypeStruct((b, s, KV_LORA), BF16),
             jax.ShapeDtypeStruct((b, s, 2 * QK_ROPE), F32)]
    specs += [row(Q_LORA), row(KV_LORA), row(2 * QK_ROPE)]
    return pl.pallas_call(
        functools.partial(_inproj_kernel, with_conv=with_conv),
        grid=(b, s // tm),
        in_specs=[row(d), vec, vec, _resident((1, d)), _resident(w_main.shape), _resident(w_kr2.shape),
                  _resident((1, Q_LORA)), _resident((1, KV_LORA))],
        out_specs=specs,
        out_shape=outs,
        compiler_params=_params(("arbitrary", "arbitrary")),
        name="inproj_conv" if with_conv else "inproj_ctx",
    )(x, scale, shift, gain, w_main, w_kr2, q_g, kv_g)


def _qproj_kernel(cq_ref, cos_ref, sin_ref, wn_ref, wr_ref, wrs_ref, q_ref):
    cq = cq_ref[0]
    qn = _dot(cq, wn_ref[...])
    qr = _dot(cq, wr_ref[...])
    qrs = _dot(cq, wrs_ref[...])
    reps = MLA_HEADS * QK_ROPE // LANES
    cos = jnp.concatenate([cos_ref[...]] * reps, axis=-1)
    sin = jnp.concatenate([sin_ref[...]] * reps, axis=-1)
    qrot = qr * cos + qrs * sin
    sc = ATTN_SCALE * LOG2E
    for h in range(MLA_HEADS):
        q_ref[0, h, :, 0:QK_NOPE] = (qn[:, h * QK_NOPE:(h + 1) * QK_NOPE] * sc).astype(BF16)
        q_ref[0, h, :, QK_NOPE:QK_DIM] = (qrot[:, h * QK_ROPE:(h + 1) * QK_ROPE] * sc).astype(BF16)


def _qproj(cqn, cos2, sin2, w_qn, w_qr, w_qrs, *, tm):
    b, s, r = cqn.shape
    return pl.pallas_call(
        _qproj_kernel,
        grid=(b, s // tm),
        in_specs=[pl.BlockSpec((1, tm, r), lambda bi, i: (bi, i, 0)),
                  pl.BlockSpec((tm, LANES), lambda bi, i: (i, 0)),
                  pl.BlockSpec((tm, LANES), lambda bi, i: (i, 0)),
                  _resident(w_qn.shape), _resident(w_qr.shape), _resident(w_qrs.shape)],
        out_specs=pl.BlockSpec((1, MLA_HEADS, tm, QK_DIM), lambda bi, i: (bi, 0, i, 0)),
        out_shape=jax.ShapeDtypeStruct((b, MLA_HEADS, s, QK_DIM), BF16),
        compiler_params=_params(("arbitrary", "arbitrary")),
        name="qproj",
    )(cqn, cos2, sin2, w_qn, w_qr, w_qrs)


def _kvproj_kernel(ckv_ref, kr_ref, *rest, rope):
    if rope:
        tab_ref, wk_ref, wv_ref, k_ref, v_ref = rest
        prod = kr_ref[0] * tab_ref[...]
        krot = prod[:, 0:QK_ROPE] + prod[:, QK_ROPE:2 * QK_ROPE]
    else:
        wk_ref, wv_ref, k_ref, v_ref = rest
        krot = kr_ref[0][:, 0:QK_ROPE]
    ckv = ckv_ref[0]
    kn = _dot(ckv, wk_ref[...])
    vv = _dot(ckv, wv_ref[...])
    krot = krot.astype(BF16)
    for h in range(MLA_HEADS):
        k_ref[0, h, :, 0:QK_NOPE] = kn[:, h * QK_NOPE:(h + 1) * QK_NOPE].astype(BF16)
        k_ref[0, h, :, QK_NOPE:QK_DIM] = krot
        v_ref[0, h, :, 0:V_DIM] = vv[:, h * V_DIM:(h + 1) * V_DIM].astype(BF16)
        v_ref[0, h, :, V_DIM:V_EXT] = jnp.ones((vv.shape[0], V_EXT - V_DIM), BF16)


def _kvproj(ckvn, kr2, tab, w_kn, w_v, *, tm):
    b, s, r = ckvn.shape
    rope = tab is not None
    ins = [ckvn, kr2]
    specs = [pl.BlockSpec((1, tm, r), lambda bi, i: (bi, i, 0)),
             pl.BlockSpec((1, tm, 2 * QK_ROPE), lambda bi, i: (bi, i, 0))]
    if rope:
        ins.append(tab)
        specs.append(pl.BlockSpec((tm, 2 * QK_ROPE), lambda bi, i: (i, 0)))
    ins += [w_kn, w_v]
    specs += [_resident(w_kn.shape), _resident(w_v.shape)]
    return pl.pallas_call(
        functools.partial(_kvproj_kernel, rope=rope),
        grid=(b, s // tm),
        in_specs=specs,
        out_specs=[pl.BlockSpec((1, MLA_HEADS, tm, QK_DIM), lambda bi, i: (bi, 0, i, 0)),
                   pl.BlockSpec((1, MLA_HEADS, tm, V_EXT), lambda bi, i: (bi, 0, i, 0))],
        out_shape=[jax.ShapeDtypeStruct((b, MLA_HEADS, s, QK_DIM), BF16),
                   jax.ShapeDtypeStruct((b, MLA_HEADS, s, V_EXT), BF16)],
        compiler_params=_params(("arbitrary", "arbitrary")),
        name="kvproj_rope" if rope else "kvproj_ctx",
    )(*ins)


def _attn_kernel(q_ref, k_ref, v_ref, kc_ref, vc_ref, o_ref, m_ref, acc_ref, sa_ref, sb_ref, *, tk):
    q = q_ref[0, 0]
    m_ref[...] = jnp.full(m_ref.shape, -jnp.inf, F32)
    acc_ref[...] = jnp.zeros(acc_ref.shape, F32)

    def lanes(a, n):
        return jnp.concatenate([a] * (n // LANES), axis=1)

    def scores(k):
        return lax.dot_general(q, k, (((1,), (1,)), ((), ())), preferred_element_type=F32)

    def update(s, v):
        m_old = m_ref[...]
        m_new = jnp.maximum(m_old, jnp.max(s, axis=-1, keepdims=True))
        alpha = jnp.exp2(m_old - m_new)
        p = jnp.exp2(s - lanes(m_new, s.shape[1]))
        acc_ref[...] = lanes(alpha, acc_ref.shape[1]) * acc_ref[...] + _dot(p.astype(BF16), v)
        m_ref[...] = m_new

    def chunk(ref, j):
        return ref[0, 0, pl.ds(pl.multiple_of(j * tk, tk), tk), :]

    n = k_ref.shape[2] // tk
    bufs = (sa_ref, sb_ref)
    sa_ref[...] = scores(chunk(k_ref, 0))
    for j in range(n):
        nxt = bufs[(j + 1) % 2]
        if j + 1 < n:
            nxt[...] = scores(chunk(k_ref, j + 1))
        else:
            s_ctx = scores(kc_ref[0, 0])
        update(bufs[j % 2][...], chunk(v_ref, j))
    update(s_ctx, vc_ref[0, 0])
    o_ref[0] = (acc_ref[:, 0:V_DIM] / acc_ref[:, V_DIM:V_EXT]).astype(BF16)


def _attention(q, k, v, kc, vc, *, tq, tk):
    b, h, s, dq = q.shape
    lc = kc.shape[2]
    return pl.pallas_call(
        functools.partial(_attn_kernel, tk=tk),
        grid=(b, h, s // tq),
        in_specs=[pl.BlockSpec((1, 1, tq, dq), lambda bi, hi, i: (bi, hi, i, 0)),
                  pl.BlockSpec((1, 1, s, dq), lambda bi, hi, i: (bi, hi, 0, 0)),
                  pl.BlockSpec((1, 1, s, V_EXT), lambda bi, hi, i: (bi, hi, 0, 0)),
                  pl.BlockSpec((1, 1, lc, dq), lambda bi, hi, i: (bi, hi, 0, 0)),
                  pl.BlockSpec((1, 1, lc, V_EXT), lambda bi, hi, i: (bi, hi, 0, 0))],
        out_specs=pl.BlockSpec((1, tq, V_DIM), lambda bi, hi, i: (bi, i, hi)),
        out_shape=jax.ShapeDtypeStruct((b, s, h * V_DIM), BF16),
        scratch_shapes=[pltpu.VMEM((tq, LANES), F32), pltpu.VMEM((tq, V_EXT), F32),
                        pltpu.VMEM((tq, tk), F32), pltpu.VMEM((tq, tk), F32)],
        compiler_params=_params(("arbitrary", "arbitrary", "arbitrary")),
        name="attention",
    )(q, k, v, kc, vc)


HALO = 16


def _outproj_kernel(u_ref, up_ref, un_ref, bg_ref, cw_ref, ya_ref, wo_ref, x_ref, g1_ref, n2_ref, sc_ref, sh_ref,
                    wr2_ref, x1_ref, tab_ref, afft_ref, *, tm, d):
    i = pl.program_id(1)
    last = pl.num_programs(1) - 1
    u = u_ref[0].astype(F32)
    rows = lax.broadcasted_iota(I32, (tm, 1), 0)
    prev_row = jnp.where(i > 0, up_ref[0][HALO - 1:HALO, :].astype(F32), 0.0)
    next_row = jnp.where(i < last, un_ref[0][0:1, :].astype(F32), 0.0)
    um1 = jnp.where(rows == 0, prev_row, pltpu.roll(u, 1, 0))
    up1 = jnp.where(rows == tm - 1, next_row, pltpu.roll(u, tm - 1, 0))
    cw = cw_ref[...]
    yc = (bg_ref[0].astype(F32) * (um1 * cw[0:1] + u * cw[1:2] + up1 * cw[2:3])).astype(BF16)
    nsub = 2
    tr = tm // nsub
    for r in range(nsub):
        rs = slice(r * tr, (r + 1) * tr)
        y = _dot(yc[rs], wo_ref[0:CONV_WIDTH, :]) + _dot(ya_ref[0, rs, :], wo_ref[CONV_WIDTH:, :])
        x1 = x_ref[0, rs, :] + g1_ref[0] * y
        x1_ref[0, rs, :] = x1
        h2 = _rms(x1) * n2_ref[...]
        h2 = h2 * (1.0 + sc_ref[0]) + sh_ref[0]
        tab_ref[0, rs, 0:d] = h2
        h_hi = h2.astype(BF16)
        h_lo = (h2 - h_hi.astype(F32)).astype(BF16)
        l2 = _dot(h_hi, wr2_ref[...])
        logits = l2[:, 0:LANES] + l2[:, LANES:2 * LANES] + _dot(h_lo, wr2_ref[:, 0:LANES])
        lane = lax.broadcasted_iota(I32, logits.shape, 1)
        logits = jnp.where(lane < N_EXPERTS, logits, -jnp.inf)
        e = jnp.exp(logits - jnp.max(logits, axis=-1, keepdims=True))
        aff = e / jnp.sum(e, axis=-1, keepdims=True)
        tab_ref[0, rs, d:d + LANES] = aff
        afft_ref[0, :, rs] = aff.T[0:N_EXPERTS, :]


def _outproj(u, bg, conv_w, y_attn, w_out, x, g1, n2, sc2, sh2, wr2, *, tm):
    b, s, d = x.shape
    nh = tm // HALO
    nhb = s // HALO
    row = lambda n: pl.BlockSpec((1, tm, n), lambda bi, i: (bi, i, 0))
    vec = pl.BlockSpec((1, 1, d), lambda bi, i: (bi, 0, 0))
    return pl.pallas_call(
        functools.partial(_outproj_kernel, tm=tm, d=d),
        grid=(b, s // tm),
        in_specs=[row(CONV_WIDTH),
                  pl.BlockSpec((1, HALO, CONV_WIDTH), lambda bi, i: (bi, jnp.maximum(i * nh - 1, 0), 0)),
                  pl.BlockSpec((1, HALO, CONV_WIDTH), lambda bi, i: (bi, jnp.minimum((i + 1) * nh, nhb - 1), 0)),
                  row(CONV_WIDTH), _resident(conv_w.shape), row(MLA_WIDTH), _resident(w_out.shape), row(d),
                  vec, _resident((1, d)), vec, vec, _resident(wr2.shape)],
        out_specs=[row(d), row(d + LANES), pl.BlockSpec((1, N_EXPERTS, tm), lambda bi, i: (bi, 0, i))],
        out_shape=[jax.ShapeDtypeStruct((b, s, d), F32), jax.ShapeDtypeStruct((b, s, d + LANES), F32),
                   jax.ShapeDtypeStruct((b, N_EXPERTS, s), F32)],
        compiler_params=_params(("arbitrary", "arbitrary")),
        name="outproj_router",
    )(u, u, u, bg, conv_w, y_attn, w_out, x, g1, n2, sc2, sh2, wr2)


def _routing_kernel(aff_ref, tri_ref, pos_ref, offs_ref, *, cap, blk, rows_per_expert):
    bi = pl.program_id(0)
    aff = aff_ref[0]
    ne, s = aff.shape
    capf = jnp.float32(cap)

    def bisect(t, prefix):
        cand = prefix | jnp.left_shift(jnp.int32(1), 30 - t)
        cnt = jnp.sum(jnp.where(aff >= lax.bitcast_convert_type(cand, F32), 1.0, 0.0), axis=1, keepdims=True)
        return jnp.where(cnt >= capf, cand, prefix)

    floor_bits = lax.fori_loop(0, 31, bisect, jnp.zeros((ne, 1), I32))
    thr = jnp.min(jnp.where(aff >= lax.bitcast_convert_type(floor_bits, F32), aff, jnp.inf), axis=1, keepdims=True)
    gt = aff > thr
    eq = aff == thr
    need = capf - jnp.sum(jnp.where(gt, 1.0, 0.0), axis=1, keepdims=True)
    tri = tri_ref[...]

    def cumsum_blocks(mask_f32):
        run = jnp.zeros((ne, 1), F32)
        parts, starts = [], []
        for kb in range(s // blk):
            c = _dot(mask_f32[:, kb * blk:(kb + 1) * blk].astype(BF16), tri)
            starts.append(run)
            parts.append(c + run)
            run = run + c[:, blk - 1:blk]
        return jnp.concatenate(parts, axis=1), starts, run

    eqf = jnp.where(eq, 1.0, 0.0)
    eq_incl, _, _ = cumsum_blocks(eqf)
    sel = jnp.where(gt, 1.0, jnp.where(eq & (eq_incl - eqf < need), 1.0, 0.0))
    incl, starts, _ = cumsum_blocks(sel)

    base = (lax.broadcasted_iota(I32, (ne, 1), 0) * rows_per_expert + bi * cap)
    pos_ref[0] = jnp.where(sel > 0.0, (incl - sel).astype(I32) + base, -1)
    lane = lax.broadcasted_iota(I32, (ne, LANES), 1)
    offs = jnp.full((ne, LANES), cap, I32) + base
    for kb, st in enumerate(starts):
        offs = jnp.where(lane == kb, st.astype(I32) + base, offs)
    offs_ref[0] = offs


def _routing(aff_t, *, cap, blk):
    b, ne, s = aff_t.shape
    assert s // blk + 1 <= LANES
    tri = jnp.asarray(np.triu(np.ones((blk, blk), np.float32)), BF16)
    return pl.pallas_call(
        functools.partial(_routing_kernel, cap=cap, blk=blk, rows_per_expert=b * cap),
        grid=(b,),
        in_specs=[pl.BlockSpec((1, ne, s), lambda bi: (bi, 0, 0)), _resident(tri.shape)],
        out_specs=[pl.BlockSpec((1, ne, s), lambda bi: (bi, 0, 0)),
                   pl.BlockSpec((1, ne, LANES), lambda bi: (bi, 0, 0))],
        out_shape=[jax.ShapeDtypeStruct((b, ne, s), I32), jax.ShapeDtypeStruct((b, ne, LANES), I32)],
        compiler_params=_params(("arbitrary",)),
        name="routing",
    )(aff_t, tri)


def _slots_kernel(offs_ref, pos_ref, idx_ref, *, cap, blk, w, rows_per_expert):
    bi = pl.program_id(0)
    ne, s = pos_ref.shape[1], pos_ref.shape[2]
    nb = s // blk
    idx_ref[...] = jnp.zeros(idx_ref.shape, I32)
    sub = lax.broadcasted_iota(I32, (w, blk), 0)
    tok = lax.broadcasted_iota(I32, (w, blk), 1)

    def block(kb, carry):
        tok1 = tok + (kb * blk + bi * s + 1)
        windows, trips = [], []
        for e in range(ne):
            base = e * rows_per_expert + bi * cap
            o = (bi * (nb + 1) + kb) * ne + e
            lo8 = (offs_ref[o] - base) // 8 * 8
            hi = offs_ref[o + ne] - base
            posrow = pos_ref[0, e:e + 1, pl.ds(pl.multiple_of(kb * blk, blk), blk)] - base

            def window(r, c, e=e, lo8=lo8, posrow=posrow):
                start = pl.multiple_of(jnp.minimum(lo8 + r * w, cap - w), 8)
                hit = jnp.where(sub + start == posrow, tok1, 0)
                val = jnp.sum(hit.astype(F32), axis=1, keepdims=True).astype(I32)
                old = idx_ref[0, pl.ds(start, w), e:e + 1]
                idx_ref[0, pl.ds(start, w), e:e + 1] = jnp.where(val > 0, val - 1, old)
                return c

            windows.append(window)
            trips.append((hi - lo8 + w - 1) // w)
        for window in windows:
            window(0, 0)
        for window, n in zip(windows, trips):
            lax.fori_loop(1, n, window, 0)
        return carry

    lax.fori_loop(0, nb, block, 0)


def _slots(offs_flat, pos, *, cap, blk, w):
    b, ne, s = pos.shape
    return pl.pallas_call(
        functools.partial(_slots_kernel, cap=cap, blk=blk, w=w, rows_per_expert=b * cap),
        grid_spec=pltpu.PrefetchScalarGridSpec(
            num_scalar_prefetch=1,
            grid=(b,),
            in_specs=[pl.BlockSpec((1, ne, s), lambda bi, o: (bi, 0, 0))],
            out_specs=pl.BlockSpec((1, cap, ne), lambda bi, o: (bi, 0, 0))),
        out_shape=jax.ShapeDtypeStruct((b, cap, ne), I32),
        compiler_params=_params(("arbitrary",)),
        name="slots",
    )(offs_flat, pos)


def _ffn_kernel(idx_ref, tab_hbm, wg_hbm, wu_hbm, wd_hbm, y_ref, buf_a, buf_b, buf_c, wg_ref, wu_ref, wd_ref,
                sg_ref, su_ref, sd_ref, sem, wsem, *, t, d):
    e = pl.program_id(0)
    j = pl.program_id(1)
    ne, nt = pl.num_programs(0), pl.num_programs(1)
    step = e * nt + j
    nsteps = ne * nt
    rc, fc = sg_ref.shape[0], sd_ref.shape[0]

    def row_copy(base, i, buf, sl):
        return pltpu.make_async_copy(tab_hbm.at[pl.ds(idx_ref[base + i], 1)], buf.at[pl.ds(i, 1)], sem.at[sl])

    def wait_rows(buf, sl):
        pltpu.make_async_copy(tab_hbm.at[pl.ds(0, t)], buf, sem.at[sl]).wait()

    def piece_copies(ex, pc):
        return [pltpu.make_async_copy(wg_hbm.at[ex, pl.ds(pl.multiple_of(pc * rc, rc), rc)], sg_ref, wsem.at[0]),
                pltpu.make_async_copy(wu_hbm.at[ex, pl.ds(pl.multiple_of(pc * rc, rc), rc)], su_ref, wsem.at[1]),
                pltpu.make_async_copy(wd_hbm.at[ex, pl.ds(pl.multiple_of(pc * fc, fc), fc)], sd_ref, wsem.at[2])]

    def cast_piece(st, pc):
        wg_ref[st, pl.ds(pl.multiple_of(pc * rc, rc), rc), :] = sg_ref[...].astype(BF16)
        wu_ref[st, pl.ds(pl.multiple_of(pc * rc, rc), rc), :] = su_ref[...].astype(BF16)
        wd_ref[st, pl.ds(pl.multiple_of(pc * fc, fc), fc), :] = sd_ref[...].astype(BF16)

    @pl.when(step == 0)
    def _():
        def issue(i, carry):
            row_copy(0, i, buf_a, 0).start()
            row_copy(jnp.minimum(1, nsteps - 1) * t, i, buf_b, 1).start()
            return carry
        lax.fori_loop(0, t, issue, 0)

        def load(pc, carry):
            cps = piece_copies(0, pc)
            for cp in cps:
                cp.start()
            for cp in cps:
                cp.wait()
            cast_piece(0, pc)
            return carry
        lax.fori_loop(0, nt - 1, load, 0)
        for cp in piece_copies(0, nt - 1):
            cp.start()

    pe = jnp.where(j > 0, e, e - 1)
    pj = jnp.where(j > 0, j - 1, nt - 1)
    nxt_e = jnp.minimum(e + 1, ne - 1)
    ws = e % 2

    def run(cur, cur_sl, mid, mid_sl, far, far_sl):
        wait_rows(cur, cur_sl)
        for cp in piece_copies(jnp.minimum(pe + 1, ne - 1), pj):
            cp.wait()
        cast_piece((pe + 1) % 2, pj)
        for cp in piece_copies(nxt_e, j):
            cp.start()
        base = jnp.minimum(step + 2, nsteps - 1) * t
        head_rows = FFN_HEAD_ROWS * t // FFN_T
        for i in range(head_rows):
            row_copy(base, i, far, far_sl).start()

        @pl.when(idx_ref[base] < 0)
        def _():
            y_ref[0:16, 0:LANES] = jnp.zeros((16, LANES), BF16)

        for i in range(head_rows, t):
            row_copy(base, i, far, far_sl).start()
        xb = cur[:, 0:d].astype(BF16)
        aff = cur[:, d:d + LANES]
        lane = lax.broadcasted_iota(I32, aff.shape, 1)
        gate_w = jnp.sum(jnp.where(lane == e, aff, 0.0), axis=-1, keepdims=True)
        g = _dot(xb, wg_ref[ws])
        up = _dot(xb, wu_ref[ws])
        hid = ((g / (1.0 + jnp.exp(-g))) * up).astype(BF16)
        y = _dot(hid, wd_ref[ws])
        y_ref[...] = (y * gate_w).astype(BF16)

        @pl.when(step == nsteps - 1)
        def _():
            wait_rows(mid, mid_sl)
            wait_rows(far, far_sl)
            for cp in piece_copies(nxt_e, j):
                cp.wait()

    bufs = (buf_a, buf_b, buf_c)
    for r in range(3):
        @pl.when(step % 3 == r)
        def _(r=r):
            run(bufs[r], r, bufs[(r + 1) % 3], (r + 1) % 3, bufs[(r + 2) % 3], (r + 2) % 3)


def _expert_ffn(idx_flat, tab, w_gate, w_up, w_down, *, t):
    ne, d, ff = w_gate.shape
    rows = idx_flat.shape[0]
    nt = rows // ne // t
    dw = tab.shape[1]
    bf16_rows = 16
    assert d % (nt * bf16_rows) == 0 and ff % (nt * bf16_rows) == 0
    return pl.pallas_call(
        functools.partial(_ffn_kernel, t=t, d=d),
        grid_spec=pltpu.PrefetchScalarGridSpec(
            num_scalar_prefetch=1,
            grid=(ne, nt),
            in_specs=[pl.BlockSpec(memory_space=pl.ANY)] * 4,
            out_specs=pl.BlockSpec((t, d), lambda e, j, idx: (e * nt + j, 0)),
            scratch_shapes=[pltpu.VMEM((t, dw), F32), pltpu.VMEM((t, dw), F32), pltpu.VMEM((t, dw), F32),
                            pltpu.VMEM((2, d, ff), BF16), pltpu.VMEM((2, d, ff), BF16), pltpu.VMEM((2, ff, d), BF16),
                            pltpu.VMEM((d // nt, ff), F32), pltpu.VMEM((d // nt, ff), F32),
                            pltpu.VMEM((ff // nt, d), F32),
                            pltpu.SemaphoreType.DMA((3,)), pltpu.SemaphoreType.DMA((3,))]),
        out_shape=jax.ShapeDtypeStruct((rows, d), BF16),
        compiler_params=_params(("arbitrary", "arbitrary"), vmem=FFN_VMEM_LIMIT),
        name="expert_ffn",
    )(idx_flat, tab, w_gate, w_up, w_down)


def _combine_kernel(offs_ref, pos_ref, y_hbm, x1_ref, g2_ref, fg_ref, o_ref, ybuf, acc_ref, sem, *, tt, w, total_rows):
    ne = pos_ref.shape[1]
    nk = pl.num_programs(1)
    step = pl.program_id(0) * nk + pl.program_id(1)
    nsteps = pl.num_programs(0) * nk
    slot = step % 2

    def bounds(st):
        o0 = (st // nk * (nk + 1) + st % nk) * ne
        lo8 = [(offs_ref[o0 + e] // 8) * 8 for e in range(ne)]
        hi = [offs_ref[o0 + ne + e] for e in range(ne)]
        return lo8, hi

    def window_copies(lo8, r, sl):
        starts = [pl.multiple_of(jnp.minimum(lo8[e] + r * w, total_rows - w), 8) for e in range(ne)]
        copies = [pltpu.make_async_copy(y_hbm.at[pl.ds(starts[e], w)], ybuf.at[sl, pl.ds(e * w, w)], sem.at[sl, e])
                  for e in range(ne)]
        return starts, copies

    @pl.when(step == 0)
    def _():
        for cp in window_copies(bounds(0)[0], 0, 0)[1]:
            cp.start()

    lo8, hi = bounds(step)
    starts0, copies0 = window_copies(lo8, 0, slot)
    for cp in copies0:
        cp.wait()

    @pl.when(step + 1 < nsteps)
    def _():
        for cp in window_copies(bounds(step + 1)[0], 0, 1 - slot)[1]:
            cp.start()

    pos = pos_ref[0]

    def contribution(r, starts):
        sub = lax.broadcasted_iota(I32, (w, tt), 0)
        want = [jnp.where(pos[e:e + 1, :] >= lo8[e] + r * w, pos[e:e + 1, :], -1) for e in range(ne)]
        blocks = [jnp.where(sub + starts[e] == want[e], 1.0, 0.0).astype(BF16) for e in range(ne)]
        onehot_t = jnp.concatenate(blocks, axis=0)
        return lax.dot_general(onehot_t, ybuf[slot], (((0,), (0,)), ((), ())), preferred_element_type=F32)

    acc_ref[...] = contribution(0, starts0)

    rounds = jnp.int32(1)
    for e in range(ne):
        rounds = jnp.maximum(rounds, (hi[e] - lo8[e] + w - 1) // w)

    def extra_round(r, carry):
        starts, copies = window_copies(lo8, r, slot)
        for cp in copies:
            cp.start()
        for cp in copies:
            cp.wait()
        acc_ref[...] += contribution(r, starts)
        return carry

    lax.fori_loop(1, rounds, extra_round, 0)
    x2 = x1_ref[0] + g2_ref[0] * acc_ref[...]
    o_ref[0] = _rms(x2) * fg_ref[...]


def _combine(offs_flat, pos, y, x1, g2, final_g, *, tt, w):
    b, s, d = x1.shape
    ne = pos.shape[1]
    total_rows = y.shape[0]
    return pl.pallas_call(
        functools.partial(_combine_kernel, tt=tt, w=w, total_rows=total_rows),
        grid_spec=pltpu.PrefetchScalarGridSpec(
            num_scalar_prefetch=1,
            grid=(b, s // tt),
            in_specs=[pl.BlockSpec((1, ne, tt), lambda bi, k, o: (bi, 0, k)),
                      pl.BlockSpec(memory_space=pl.ANY),
                      pl.BlockSpec((1, tt, d), lambda bi, k, o: (bi, k, 0)),
                      pl.BlockSpec((1, 1, d), lambda bi, k, o: (bi, 0, 0)),
                      pl.BlockSpec((1, d), lambda bi, k, o: (0, 0))],
            out_specs=pl.BlockSpec((1, tt, d), lambda bi, k, o: (bi, k, 0)),
            scratch_shapes=[pltpu.VMEM((2, ne * w, d), BF16), pltpu.VMEM((tt, d), F32),
                            pltpu.SemaphoreType.DMA((2, ne))]),
        out_shape=jax.ShapeDtypeStruct((b, s, d), F32),
        compiler_params=_params(("arbitrary", "arbitrary")),
        name="combine_final",
    )(offs_flat, pos, y, x1, g2, final_g)


def _rope_tables(s):
    n_rows = s // GRID_W
    row = np.repeat(np.arange(n_rows), GRID_W).astype(np.float64)
    col = np.tile(np.arange(GRID_W), n_rows).astype(np.float64)
    inv_freq = ROPE_THETA ** (-np.arange(ROPE_PAIRS, dtype=np.float64) / ROPE_PAIRS)
    ang_r = row[:, None] * inv_freq[None, :]
    ang_c = col[:, None] * inv_freq[None, :]
    cos64 = np.concatenate([np.cos(ang_r)] * 2 + [np.cos(ang_c)] * 2, axis=-1).astype(np.float32)
    sin64 = np.concatenate([-np.sin(ang_r), np.sin(ang_r), -np.sin(ang_c), np.sin(ang_c)], axis=-1).astype(np.float32)
    return cos64, sin64


ROPE_SWAP = np.concatenate([np.arange(16, 32), np.arange(0, 16), np.arange(48, 64), np.arange(32, 48)])


def kernel(x, c, ctx, c_ctx, w_mod, b_mod, norm1_g, norm2_g, w_in, conv_w, q_norm_g, w_uq, kv_norm_g, w_ukv, w_out,
           w_router, w_gate, w_up, w_down, final_g):
    b, s, d = x.shape
    lc = ctx.shape[1]
    assert w_in.shape[0] == 1, "single-layer stack"
    assert b <= 7 and s % max(ROW_TM, QKV_TM, ATT_TQ, ATT_TK, CMB_TT) == 0
    cap = EC_FACTOR * s // N_EXPERTS
    assert (b * cap) % FFN_T == 0 and cap % IDX_PC == 0

    cvec = jnp.zeros((8, d), F32).at[0:b].set(c).at[b].set(c_ctx)
    mod = _modulation(cvec, w_mod[0], b_mod[0][None, :])
    sh1, sc1, g1, sh2, sc2, g2 = [mod[0:b, None, i * d:(i + 1) * d] for i in range(6)]
    shc1, scc1 = mod[b:b + 1, None, 0:d], mod[b:b + 1, None, d:2 * d]

    n_main = 3 * CONV_WIDTH + Q_LORA + KV_LORA
    w_main = w_in[0, :, 0:n_main].astype(BF16)
    w_kr = w_in[0, :, n_main:]
    w_kr2 = jnp.concatenate([w_kr, w_kr[:, ROPE_SWAP]], axis=-1).astype(BF16)
    w_qn = w_uq[0, :, :, 0:QK_NOPE].reshape(Q_LORA, MLA_HEADS * QK_NOPE).astype(BF16)
    w_qr3 = w_uq[0, :, :, QK_NOPE:]
    w_qr = w_qr3.reshape(Q_LORA, MLA_HEADS * QK_ROPE).astype(BF16)
    w_qrs = w_qr3[:, :, ROPE_SWAP].reshape(Q_LORA, MLA_HEADS * QK_ROPE).astype(BF16)
    w_kn = w_ukv[0, :, :, 0:QK_NOPE].reshape(KV_LORA, MLA_HEADS * QK_NOPE).astype(BF16)
    w_v = w_ukv[0, :, :, QK_NOPE:].reshape(KV_LORA, MLA_HEADS * V_DIM).astype(BF16)
    w_o = w_out[0].astype(BF16)
    wr = jnp.zeros((d, LANES), F32).at[:, 0:N_EXPERTS].set(w_router[0])
    wr_hi = wr.astype(BF16)
    wr2 = jnp.concatenate([wr_hi, (wr - wr_hi.astype(F32)).astype(BF16)], axis=-1)

    cos64, sin64 = _rope_tables(s)
    cos2 = jnp.asarray(np.concatenate([cos64, cos64], axis=-1))
    sin2 = jnp.asarray(np.concatenate([sin64, sin64], axis=-1))
    ktab = jnp.asarray(np.concatenate([cos64, sin64], axis=-1))

    n1 = norm1_g[0][None, :]
    qg, kvg = q_norm_g[0][None, :], kv_norm_g[0][None, :]
    u, bg, cqn, ckvn, kr2 = _inproj(x, sc1, sh1, n1, w_main, w_kr2, qg, kvg, with_conv=True, tm=ROW_TM)
    _, ckvn_c, kr2_c = _inproj(ctx, scc1, shc1, n1, w_main, w_kr2, qg, kvg, with_conv=False, tm=lc)

    q = _qproj(cqn, cos2, sin2, w_qn, w_qr, w_qrs, tm=QKV_TM)
    k, v = _kvproj(ckvn, kr2, ktab, w_kn, w_v, tm=QKV_TM)
    kc, vc = _kvproj(ckvn_c, kr2_c, None, w_kn, w_v, tm=lc)
    y_attn = _attention(q, k, v, kc, vc, tq=ATT_TQ, tk=ATT_TK)

    x1, tab, aff_t = _outproj(u, bg, conv_w[0], y_attn, w_o, x, g1, norm2_g[0][None, :], sc2, sh2, wr2, tm=ROW_TM)

    pos, offs = _routing(aff_t, cap=cap, blk=CMB_TT)
    nk1 = s // CMB_TT + 1
    offs_flat = jnp.swapaxes(offs[:, :, 0:nk1], 1, 2).reshape(-1)
    idx_t = _slots(offs_flat, pos, cap=cap, blk=CMB_TT, w=CMB_W)
    idx_flat = jnp.transpose(idx_t, (2, 0, 1)).reshape(-1)
    y = _expert_ffn(idx_flat, tab.reshape(b * s, d + LANES), w_gate[0], w_up[0], w_down[0], t=FFN_T)
    return _combine(offs_flat, pos, y, x1, g2, final_g[None, :], tt=CMB_TT, w=CMB_W)
```

```python
import functools
import math

import jax
import jax.numpy as jnp
import numpy as np
from jax import lax
from jax.experimental import pallas as pl
from jax.experimental.pallas import tpu as pltpu

F32 = jnp.float32
BF16 = jnp.bfloat16
I32 = jnp.int32

GRID_W = 64
CONV_WIDTH = 1024
MLA_HEADS = 8
QK_NOPE = 128
QK_ROPE = 64
V_DIM = 128
Q_LORA = 512
KV_LORA = 512
QK_DIM = QK_NOPE + QK_ROPE
MLA_WIDTH = MLA_HEADS * V_DIM
V_EXT = 2 * V_DIM
N_EXPERTS = 16
EC_FACTOR = 2
ROPE_THETA = 10000.0
ROPE_PAIRS = QK_ROPE // 4
ATTN_SCALE = 1.0 / math.sqrt(QK_DIM)
LOG2E = math.log2(math.e)
EPS = 1e-6

LANES = 128
VMEM_LIMIT = 56 * 1024 * 1024
FFN_VMEM_LIMIT = 60 * 1024 * 1024

MOD_TN = 2048
ROW_TM = 512
QKV_TM = 1024
ATT_TQ = 1024
ATT_TK = 512
FFN_T = 256
FFN_HEAD_ROWS = 96
CMB_TT = 256
CMB_W = 64
IDX_PC = 64


def _dot(a, b):
    return jnp.dot(a, b, preferred_element_type=F32)


def _params(sem, vmem=VMEM_LIMIT):
    return pltpu.CompilerParams(dimension_semantics=sem, vmem_limit_bytes=vmem)


def _resident(shape):
    nd = len(shape)
    return pl.BlockSpec(shape, lambda *_: (0,) * nd, pipeline_mode=pl.Buffered(1))


def _mod_kernel(c_ref, w_ref, b_ref, o_ref):
    cv = c_ref[...]
    a = cv / (1.0 + jnp.exp(-cv))
    a_hi = a.astype(BF16)
    a_lo = (a - a_hi.astype(F32)).astype(BF16)
    w = w_ref[...].astype(BF16)
    o_ref[...] = _dot(a_hi, w) + _dot(a_lo, w) + b_ref[...]


def _modulation(cvec, w_mod, b_mod):
    d, n = w_mod.shape
    return pl.pallas_call(
        _mod_kernel,
        grid=(n // MOD_TN,),
        in_specs=[pl.BlockSpec((8, d), lambda j: (0, 0)),
                  pl.BlockSpec((d, MOD_TN), lambda j: (0, j)),
                  pl.BlockSpec((1, MOD_TN), lambda j: (0, j))],
        out_specs=pl.BlockSpec((8, MOD_TN), lambda j: (0, j)),
        out_shape=jax.ShapeDtypeStruct((8, n), F32),
        compiler_params=_params(("arbitrary",)),
        name="modulation",
    )(cvec, w_mod, b_mod)


def _rms(v):
    return v * lax.rsqrt(jnp.mean(v * v, axis=-1, keepdims=True) + EPS)


def _inproj_kernel(x_ref, sc_ref, sh_ref, g_ref, w_ref, wkr_ref, qg_ref, kvg_ref, *outs, with_conv):
    h = _rms(x_ref[0]) * g_ref[...]
    h = h * (1.0 + sc_ref[0]) + sh_ref[0]
    hb = h.astype(BF16)
    c = CONV_WIDTH
    if with_conv:
        u_ref, bg_ref, cq_ref, ckv_ref, kr_ref = outs
        xin = _dot(hb, w_ref[:, 0:c])
        cg = _dot(hb, w_ref[:, 2 * c:3 * c])
        u_ref[0] = (cg * xin).astype(BF16)
        bg_ref[0] = _dot(hb, w_ref[:, c:2 * c]).astype(BF16)
    else:
        cq_ref, ckv_ref, kr_ref = outs
    o = 3 * c
    cq = _dot(hb, w_ref[:, o:o + Q_LORA])
    cq_ref[0] = (_rms(cq) * qg_ref[...]).astype(BF16)
    ckv = _dot(hb, w_ref[:, o + Q_LORA:o + Q_LORA + KV_LORA])
    ckv_ref[0] = (_rms(ckv) * kvg_ref[...]).astype(BF16)
    kr_ref[0] = _dot(hb, wkr_ref[...])


def _inproj(x, scale, shift, gain, w_main, w_kr2, q_g, kv_g, *, with_conv, tm):
    b, s, d = x.shape
    per_batch = scale.shape[0] > 1
    vec = pl.BlockSpec((1, 1, d), (lambda bi, i: (bi, 0, 0)) if per_batch else (lambda bi, i: (0, 0, 0)))
    row = lambda n: pl.BlockSpec((1, tm, n), lambda bi, i: (bi, i, 0))
    outs, specs = [], []
    if with_conv:
        outs += [jax.ShapeDtypeStruct((b, s, CONV_WIDTH), BF16)] * 2
        specs += [row(CONV_WIDTH)] * 2
    outs += [jax.ShapeDtypeStruct((b, s, Q_LORA), BF16), jax.ShapeDtypeStruct((b, s, KV_LORA), BF16),
             jax.ShapeDtypeStruct((b, s, 2 * QK_ROPE), F32)]
    specs += [row(Q_LORA), row(KV_LORA), row(2 * QK_ROPE)]
    return pl.pallas_call(
        functools.partial(_inproj_kernel, with_conv=with_conv),
        grid=(b, s // tm),
        in_specs=[row(d), vec, vec, _resident((1, d)), _resident(w_main.shape), _resident(w_kr2.shape),
                  _resident((1, Q_LORA)), _resident((1, KV_LORA))],
        out_specs=specs,
        out_shape=outs,
        compiler_params=_params(("arbitrary", "arbitrary")),
        name="inproj_conv" if with_conv else "inproj_ctx",
    )(x, scale, shift, gain, w_main, w_kr2, q_g, kv_g)


def _qproj_kernel(cq_ref, cos_ref, sin_ref, wn_ref, wr_ref, wrs_ref, q_ref):
    cq = cq_ref[0]
    qn = _dot(cq, wn_ref[...])
    qr = _dot(cq, wr_ref[...])
    qrs = _dot(cq, wrs_ref[...])
    reps = MLA_HEADS * QK_ROPE // LANES
    cos = jnp.concatenate([cos_ref[...]] * reps, axis=-1)
    sin = jnp.concatenate([sin_ref[...]] * reps, axis=-1)
    qrot = qr * cos + qrs * sin
    sc = ATTN_SCALE * LOG2E
    for h in range(MLA_HEADS):
        q_ref[0, h, :, 0:QK_NOPE] = (qn[:, h * QK_NOPE:(h + 1) * QK_NOPE] * sc).astype(BF16)
        q_ref[0, h, :, QK_NOPE:QK_DIM] = (qrot[:, h * QK_ROPE:(h + 1) * QK_ROPE] * sc).astype(BF16)


def _qproj(cqn, cos2, sin2, w_qn, w_qr, w_qrs, *, tm):
    b, s, r = cqn.shape
    return pl.pallas_call(
        _qproj_kernel,
        grid=(b, s // tm),
        in_specs=[pl.BlockSpec((1, tm, r), lambda bi, i: (bi, i, 0)),
                  pl.BlockSpec((tm, LANES), lambda bi, i: (i, 0)),
                  pl.BlockSpec((tm, LANES), lambda bi, i: (i, 0)),
                  _resident(w_qn.shape), _resident(w_qr.shape), _resident(w_qrs.shape)],
        out_specs=pl.BlockSpec((1, MLA_HEADS, tm, QK_DIM), lambda bi, i: (bi, 0, i, 0)),
        out_shape=jax.ShapeDtypeStruct((b, MLA_HEADS, s, QK_DIM), BF16),
        compiler_params=_params(("arbitrary", "arbitrary")),
        name="qproj",
    )(cqn, cos2, sin2, w_qn, w_qr, w_qrs)


def _kvproj_kernel(ckv_ref, kr_ref, *rest, rope):
    if rope:
        tab_ref, wk_ref, wv_ref, k_ref, v_ref = rest
        prod = kr_ref[0] * tab_ref[...]
        krot = prod[:, 0:QK_ROPE] + prod[:, QK_ROPE:2 * QK_ROPE]
    else:
        wk_ref, wv_ref, k_ref, v_ref = rest
        krot = kr_ref[0][:, 0:QK_ROPE]
    ckv = ckv_ref[0]
    kn = _dot(ckv, wk_ref[...])
    vv = _dot(ckv, wv_ref[...])
    krot = krot.astype(BF16)
    for h in range(MLA_HEADS):
        k_ref[0, h, :, 0:QK_NOPE] = kn[:, h * QK_NOPE:(h + 1) * QK_NOPE].astype(BF16)
        k_ref[0, h, :, QK_NOPE:QK_DIM] = krot
        v_ref[0, h, :, 0:V_DIM] = vv[:, h * V_DIM:(h + 1) * V_DIM].astype(BF16)
        v_ref[0, h, :, V_DIM:V_EXT] = jnp.ones((vv.shape[0], V_EXT - V_DIM), BF16)


def _kvproj(ckvn, kr2, tab, w_kn, w_v, *, tm):
    b, s, r = ckvn.shape
    rope = tab is not None
    ins = [ckvn, kr2]
    specs = [pl.BlockSpec((1, tm, r), lambda bi, i: (bi, i, 0)),
             pl.BlockSpec((1, tm, 2 * QK_ROPE), lambda bi, i: (bi, i, 0))]
    if rope:
        ins.append(tab)
        specs.append(pl.BlockSpec((tm, 2 * QK_ROPE), lambda bi, i: (i, 0)))
    ins += [w_kn, w_v]
    specs += [_resident(w_kn.shape), _resident(w_v.shape)]
    return pl.pallas_call(
        functools.partial(_kvproj_kernel, rope=rope),
        grid=(b, s // tm),
        in_specs=specs,
        out_specs=[pl.BlockSpec((1, MLA_HEADS, tm, QK_DIM), lambda bi, i: (bi, 0, i, 0)),
                   pl.BlockSpec((1, MLA_HEADS, tm, V_EXT), lambda bi, i: (bi, 0, i, 0))],
        out_shape=[jax.ShapeDtypeStruct((b, MLA_HEADS, s, QK_DIM), BF16),
                   jax.ShapeDtypeStruct((b, MLA_HEADS, s, V_EXT), BF16)],
        compiler_params=_params(("arbitrary", "arbitrary")),
        name="kvproj_rope" if rope else "kvproj_ctx",
    )(*ins)


def _attn_kernel(q_ref, k_ref, v_ref, kc_ref, vc_ref, o_ref, m_ref, acc_ref, sa_ref, sb_ref, *, tk):
    q = q_ref[0, 0]
    m_ref[...] = jnp.full(m_ref.shape, -jnp.inf, F32)
    acc_ref[...] = jnp.zeros(acc_ref.shape, F32)

    def lanes(a, n):
        return jnp.concatenate([a] * (n // LANES), axis=1)

    def scores(k):
        return lax.dot_general(q, k, (((1,), (1,)), ((), ())), preferred_element_type=F32)

    def update(s, v):
        m_old = m_ref[...]
        m_new = jnp.maximum(m_old, jnp.max(s, axis=-1, keepdims=True))
        alpha = jnp.exp2(m_old - m_new)
        p = jnp.exp2(s - lanes(m_new, s.shape[1]))
        acc_ref[...] = lanes(alpha, acc_ref.shape[1]) * acc_ref[...] + _dot(p.astype(BF16), v)
        m_ref[...] = m_new

    def chunk(ref, j):
        return ref[0, 0, pl.ds(pl.multiple_of(j * tk, tk), tk), :]

    n = k_ref.shape[2] // tk
    bufs = (sa_ref, sb_ref)
    sa_ref[...] = scores(chunk(k_ref, 0))
    for j in range(n):
        nxt = bufs[(j + 1) % 2]
        if j + 1 < n:
            nxt[...] = scores(chunk(k_ref, j + 1))
        else:
            s_ctx = scores(kc_ref[0, 0])
        update(bufs[j % 2][...], chunk(v_ref, j))
    update(s_ctx, vc_ref[0, 0])
    o_ref[0] = (acc_ref[:, 0:V_DIM] / acc_ref[:, V_DIM:V_EXT]).astype(BF16)


def _attention(q, k, v, kc, vc, *, tq, tk):
    b, h, s, dq = q.shape
    lc = kc.shape[2]
    return pl.pallas_call(
        functools.partial(_attn_kernel, tk=tk),
        grid=(b, h, s // tq),
        in_specs=[pl.BlockSpec((1, 1, tq, dq), lambda bi, hi, i: (bi, hi, i, 0)),
                  pl.BlockSpec((1, 1, s, dq), lambda bi, hi, i: (bi, hi, 0, 0)),
                  pl.BlockSpec((1, 1, s, V_EXT), lambda bi, hi, i: (bi, hi, 0, 0)),
                  pl.BlockSpec((1, 1, lc, dq), lambda bi, hi, i: (bi, hi, 0, 0)),
                  pl.BlockSpec((1, 1, lc, V_EXT), lambda bi, hi, i: (bi, hi, 0, 0))],
        out_specs=pl.BlockSpec((1, tq, V_DIM), lambda bi, hi, i: (bi, i, hi)),
        out_shape=jax.ShapeDtypeStruct((b, s, h * V_DIM), BF16),
        scratch_shapes=[pltpu.VMEM((tq, LANES), F32), pltpu.VMEM((tq, V_EXT), F32),
                        pltpu.VMEM((tq, tk), F32), pltpu.VMEM((tq, tk), F32)],
        compiler_params=_params(("arbitrary", "arbitrary", "arbitrary")),
        name="attention",
    )(q, k, v, kc, vc)


HALO = 16


def _outproj_kernel(u_ref, up_ref, un_ref, bg_ref, cw_ref, ya_ref, wo_ref, x_ref, g1_ref, n2_ref, sc_ref, sh_ref,
                    wr2_ref, x1_ref, tab_ref, afft_ref, *, tm, d):
    i = pl.program_id(1)
    last = pl.num_programs(1) - 1
    u = u_ref[0].astype(F32)
    rows = lax.broadcasted_iota(I32, (tm, 1), 0)
    prev_row = jnp.where(i > 0, up_ref[0][HALO - 1:HALO, :].astype(F32), 0.0)
    next_row = jnp.where(i < last, un_ref[0][0:1, :].astype(F32), 0.0)
    um1 = jnp.where(rows == 0, prev_row, pltpu.roll(u, 1, 0))
    up1 = jnp.where(rows == tm - 1, next_row, pltpu.roll(u, tm - 1, 0))
    cw = cw_ref[...]
    yc = (bg_ref[0].astype(F32) * (um1 * cw[0:1] + u * cw[1:2] + up1 * cw[2:3])).astype(BF16)
    nsub = 2
    tr = tm // nsub
    for r in range(nsub):
        rs = slice(r * tr, (r + 1) * tr)
        y = _dot(yc[rs], wo_ref[0:CONV_WIDTH, :]) + _dot(ya_ref[0, rs, :], wo_ref[CONV_WIDTH:, :])
        x1 = x_ref[0, rs, :] + g1_ref[0] * y
        x1_ref[0, rs, :] = x1
        h2 = _rms(x1) * n2_ref[...]
        h2 = h2 * (1.0 + sc_ref[0]) + sh_ref[0]
        tab_ref[0, rs, 0:d] = h2
        h_hi = h2.astype(BF16)
        h_lo = (h2 - h_hi.astype(F32)).astype(BF16)
        l2 = _dot(h_hi, wr2_ref[...])
        logits = l2[:, 0:LANES] + l2[:, LANES:2 * LANES] + _dot(h_lo, wr2_ref[:, 0:LANES])
        lane = lax.broadcasted_iota(I32, logits.shape, 1)
        logits = jnp.where(lane < N_EXPERTS, logits, -jnp.inf)
        e = jnp.exp(logits - jnp.max(logits, axis=-1, keepdims=True))
        aff = e / jnp.sum(e, axis=-1, keepdims=True)
        tab_ref[0, rs, d:d + LANES] = aff
        afft_ref[0, :, rs] = aff.T[0:N_EXPERTS, :]


def _outproj(u, bg, conv_w, y_attn, w_out, x, g1, n2, sc2, sh2, wr2, *, tm):
    b, s, d = x.shape
    nh = tm // HALO
    nhb = s // HALO
    row = lambda n: pl.BlockSpec((1, tm, n), lambda bi, i: (bi, i, 0))
    vec = pl.BlockSpec((1, 1, d), lambda bi, i: (bi, 0, 0))
    return pl.pallas_call(
        functools.partial(_outproj_kernel, tm=tm, d=d),
        grid=(b, s // tm),
        in_specs=[row(CONV_WIDTH),
                  pl.BlockSpec((1, HALO, CONV_WIDTH), lambda bi, i: (bi, jnp.maximum(i * nh - 1, 0), 0)),
                  pl.BlockSpec((1, HALO, CONV_WIDTH), lambda bi, i: (bi, jnp.minimum((i + 1) * nh, nhb - 1), 0)),
                  row(CONV_WIDTH), _resident(conv_w.shape), row(MLA_WIDTH), _resident(w_out.shape), row(d),
                  vec, _resident((1, d)), vec, vec, _resident(wr2.shape)],
        out_specs=[row(d), row(d + LANES), pl.BlockSpec((1, N_EXPERTS, tm), lambda bi, i: (bi, 0, i))],
        out_shape=[jax.ShapeDtypeStruct((b, s, d), F32), jax.ShapeDtypeStruct((b, s, d + LANES), F32),
                   jax.ShapeDtypeStruct((b, N_EXPERTS, s), F32)],
        compiler_params=_params(("arbitrary", "arbitrary")),
        name="outproj_router",
    )(u, u, u, bg, conv_w, y_attn, w_out, x, g1, n2, sc2, sh2, wr2)


def _routing_kernel(aff_ref, tri_ref, pos_ref, offs_ref, *, cap, blk, rows_per_expert):
    bi = pl.program_id(0)
    aff = aff_ref[0]
    ne, s = aff.shape
    capf = jnp.float32(cap)

    def bisect(t, prefix):
        cand = prefix | jnp.left_shift(jnp.int32(1), 30 - t)
        cnt = jnp.sum(jnp.where(aff >= lax.bitcast_convert_type(cand, F32), 1.0, 0.0), axis=1, keepdims=True)
        return jnp.where(cnt >= capf, cand, prefix)

    floor_bits = lax.fori_loop(0, 31, bisect, jnp.zeros((ne, 1), I32))
    thr = jnp.min(jnp.where(aff >= lax.bitcast_convert_type(floor_bits, F32), aff, jnp.inf), axis=1, keepdims=True)
    gt = aff > thr
    eq = aff == thr
    need = capf - jnp.sum(jnp.where(gt, 1.0, 0.0), axis=1, keepdims=True)
    tri = tri_ref[...]

    def cumsum_blocks(mask_f32):
        run = jnp.zeros((ne, 1), F32)
        parts, starts = [], []
        for kb in range(s // blk):
            c = _dot(mask_f32[:, kb * blk:(kb + 1) * blk].astype(BF16), tri)
            starts.append(run)
            parts.append(c + run)
            run = run + c[:, blk - 1:blk]
        return jnp.concatenate(parts, axis=1), starts, run

    eqf = jnp.where(eq, 1.0, 0.0)
    eq_incl, _, _ = cumsum_blocks(eqf)
    sel = jnp.where(gt, 1.0, jnp.where(eq & (eq_incl - eqf < need), 1.0, 0.0))
    incl, starts, _ = cumsum_blocks(sel)

    base = (lax.broadcasted_iota(I32, (ne, 1), 0) * rows_per_expert + bi * cap)
    pos_ref[0] = jnp.where(sel > 0.0, (incl - sel).astype(I32) + base, -1)
    lane = lax.broadcasted_iota(I32, (ne, LANES), 1)
    offs = jnp.full((ne, LANES), cap, I32) + base
    for kb, st in enumerate(starts):
        offs = jnp.where(lane == kb, st.astype(I32) + base, offs)
    offs_ref[0] = offs


def _routing(aff_t, *, cap, blk):
    b, ne, s = aff_t.shape
    assert s // blk + 1 <= LANES
    tri = jnp.asarray(np.triu(np.ones((blk, blk), np.float32)), BF16)
    return pl.pallas_call(
        functools.partial(_routing_kernel, cap=cap, blk=blk, rows_per_expert=b * cap),
        grid=(b,),
        in_specs=[pl.BlockSpec((1, ne, s), lambda bi: (bi, 0, 0)), _resident(tri.shape)],
        out_specs=[pl.BlockSpec((1, ne, s), lambda bi: (bi, 0, 0)),
                   pl.BlockSpec((1, ne, LANES), lambda bi: (bi, 0, 0))],
        out_shape=[jax.ShapeDtypeStruct((b, ne, s), I32), jax.ShapeDtypeStruct((b, ne, LANES), I32)],
        compiler_params=_params(("arbitrary",)),
        name="routing",
    )(aff_t, tri)


def _slots_kernel(offs_ref, pos_ref, idx_ref, *, cap, blk, w, rows_per_expert):
    bi = pl.program_id(0)
    ne, s = pos_ref.shape[1], pos_ref.shape[2]
    nb = s // blk
    idx_ref[...] = jnp.zeros(idx_ref.shape, I32)
    sub = lax.broadcasted_iota(I32, (w, blk), 0)
    tok = lax.broadcasted_iota(I32, (w, blk), 1)

    def block(kb, carry):
        tok1 = tok + (kb * blk + bi * s + 1)
        windows, trips = [], []
        for e in range(ne):
            base = e * rows_per_expert + bi * cap
            o = (bi * (nb + 1) + kb) * ne + e
            lo8 = (offs_ref[o] - base) // 8 * 8
            hi = offs_ref[o + ne] - base
            posrow = pos_ref[0, e:e + 1, pl.ds(pl.multiple_of(kb * blk, blk), blk)] - base

            def window(r, c, e=e, lo8=lo8, posrow=posrow):
                start = pl.multiple_of(jnp.minimum(lo8 + r * w, cap - w), 8)
                hit = jnp.where(sub + start == posrow, tok1, 0)
                val = jnp.sum(hit.astype(F32), axis=1, keepdims=True).astype(I32)
                old = idx_ref[0, pl.ds(start, w), e:e + 1]
                idx_ref[0, pl.ds(start, w), e:e + 1] = jnp.where(val > 0, val - 1, old)
                return c

            windows.append(window)
            trips.append((hi - lo8 + w - 1) // w)
        for window in windows:
            window(0, 0)
        for window, n in zip(windows, trips):
            lax.fori_loop(1, n, window, 0)
        return carry

    lax.fori_loop(0, nb, block, 0)


def _slots(offs_flat, pos, *, cap, blk, w):
    b, ne, s = pos.shape
    return pl.pallas_call(
        functools.partial(_slots_kernel, cap=cap, blk=blk, w=w, rows_per_expert=b * cap),
        grid_spec=pltpu.PrefetchScalarGridSpec(
            num_scalar_prefetch=1,
            grid=(b,),
            in_specs=[pl.BlockSpec((1, ne, s), lambda bi, o: (bi, 0, 0))],
            out_specs=pl.BlockSpec((1, cap, ne), lambda bi, o: (bi, 0, 0))),
        out_shape=jax.ShapeDtypeStruct((b, cap, ne), I32),
        compiler_params=_params(("arbitrary",)),
        name="slots",
    )(offs_flat, pos)


def _ffn_kernel(idx_ref, tab_hbm, wg_hbm, wu_hbm, wd_hbm, y_ref, buf_a, buf_b, buf_c, wg_ref, wu_ref, wd_ref,
                sg_ref, su_ref, sd_ref, sem, wsem, *, t, d):
    e = pl.program_id(0)
    j = pl.program_id(1)
    ne, nt = pl.num_programs(0), pl.num_programs(1)
    step = e * nt + j
    nsteps = ne * nt
    rc, fc = sg_ref.shape[0], sd_ref.shape[0]

    def row_copy(base, i, buf, sl):
        return pltpu.make_async_copy(tab_hbm.at[pl.ds(idx_ref[base + i], 1)], buf.at[pl.ds(i, 1)], sem.at[sl])

    def wait_rows(buf, sl):
        pltpu.make_async_copy(tab_hbm.at[pl.ds(0, t)], buf, sem.at[sl]).wait()

    def piece_copies(ex, pc):
        return [pltpu.make_async_copy(wg_hbm.at[ex, pl.ds(pl.multiple_of(pc * rc, rc), rc)], sg_ref, wsem.at[0]),
                pltpu.make_async_copy(wu_hbm.at[ex, pl.ds(pl.multiple_of(pc * rc, rc), rc)], su_ref, wsem.at[1]),
                pltpu.make_async_copy(wd_hbm.at[ex, pl.ds(pl.multiple_of(pc * fc, fc), fc)], sd_ref, wsem.at[2])]

    def cast_piece(st, pc):
        wg_ref[st, pl.ds(pl.multiple_of(pc * rc, rc), rc), :] = sg_ref[...].astype(BF16)
        wu_ref[st, pl.ds(pl.multiple_of(pc * rc, rc), rc), :] = su_ref[...].astype(BF16)
        wd_ref[st, pl.ds(pl.multiple_of(pc * fc, fc), fc), :] = sd_ref[...].astype(BF16)

    @pl.when(step == 0)
    def _():
        def issue(i, carry):
            row_copy(0, i, buf_a, 0).start()
            row_copy(jnp.minimum(1, nsteps - 1) * t, i, buf_b, 1).start()
            return carry
        lax.fori_loop(0, t, issue, 0)

        def load(pc, carry):
            cps = piece_copies(0, pc)
            for cp in cps:
                cp.start()
            for cp in cps:
                cp.wait()
            cast_piece(0, pc)
            return carry
        lax.fori_loop(0, nt - 1, load, 0)
        for cp in piece_copies(0, nt - 1):
            cp.start()

    pe = jnp.where(j > 0, e, e - 1)
    pj = jnp.where(j > 0, j - 1, nt - 1)
    nxt_e = jnp.minimum(e + 1, ne - 1)
    ws = e % 2

    def run(cur, cur_sl, mid, mid_sl, far, far_sl):
        wait_rows(cur, cur_sl)
        for cp in piece_copies(jnp.minimum(pe + 1, ne - 1), pj):
            cp.wait()
        cast_piece((pe + 1) % 2, pj)
        for cp in piece_copies(nxt_e, j):
            cp.start()
        base = jnp.minimum(step + 2, nsteps - 1) * t
        head_rows = FFN_HEAD_ROWS * t // FFN_T
        for i in range(head_rows):
            row_copy(base, i, far, far_sl).start()

        @pl.when(idx_ref[base] < 0)
        def _():
            y_ref[0:16, 0:LANES] = jnp.zeros((16, LANES), BF16)

        for i in range(head_rows, t):
            row_copy(base, i, far, far_sl).start()
        xb = cur[:, 0:d].astype(BF16)
        aff = cur[:, d:d + LANES]
        lane = lax.broadcasted_iota(I32, aff.shape, 1)
        gate_w = jnp.sum(jnp.where(lane == e, aff, 0.0), axis=-1, keepdims=True)
        g = _dot(xb, wg_ref[ws])
        up = _dot(xb, wu_ref[ws])
        hid = ((g / (1.0 + jnp.exp(-g))) * up).astype(BF16)
        y = _dot(hid, wd_ref[ws])
        y_ref[...] = (y * gate_w).astype(BF16)

        @pl.when(step == nsteps - 1)
        def _():
            wait_rows(mid, mid_sl)
            wait_rows(far, far_sl)
            for cp in piece_copies(nxt_e, j):
                cp.wait()

    bufs = (buf_a, buf_b, buf_c)
    for r in range(3):
        @pl.when(step % 3 == r)
        def _(r=r):
            run(bufs[r], r, bufs[(r + 1) % 3], (r + 1) % 3, bufs[(r + 2) % 3], (r + 2) % 3)


def _expert_ffn(idx_flat, tab, w_gate, w_up, w_down, *, t):
    ne, d, ff = w_gate.shape
    rows = idx_flat.shape[0]
    nt = rows // ne // t
    dw = tab.shape[1]
    bf16_rows = 16
    assert d % (nt * bf16_rows) == 0 and ff % (nt * bf16_rows) == 0
    return pl.pallas_call(
        functools.partial(_ffn_kernel, t=t, d=d),
        grid_spec=pltpu.PrefetchScalarGridSpec(
            num_scalar_prefetch=1,
            grid=(ne, nt),
            in_specs=[pl.BlockSpec(memory_space=pl.ANY)] * 4,
            out_specs=pl.BlockSpec((t, d), lambda e, j, idx: (e * nt + j, 0)),
            scratch_shapes=[pltpu.VMEM((t, dw), F32), pltpu.VMEM((t, dw), F32), pltpu.VMEM((t, dw), F32),
                            pltpu.VMEM((2, d, ff), BF16), pltpu.VMEM((2, d, ff), BF16), pltpu.VMEM((2, ff, d), BF16),
                            pltpu.VMEM((d // nt, ff), F32), pltpu.VMEM((d // nt, ff), F32),
                            pltpu.VMEM((ff // nt, d), F32),
                            pltpu.SemaphoreType.DMA((3,)), pltpu.SemaphoreType.DMA((3,))]),
        out_shape=jax.ShapeDtypeStruct((rows, d), BF16),
        compiler_params=_params(("arbitrary", "arbitrary"), vmem=FFN_VMEM_LIMIT),
        name="expert_ffn",
    )(idx_flat, tab, w_gate, w_up, w_down)


def _combine_kernel(offs_ref, pos_ref, y_hbm, x1_ref, g2_ref, fg_ref, o_ref, ybuf, acc_ref, sem, *, tt, w, total_rows):
    ne = pos_ref.shape[1]
    nk = pl.num_programs(1)
    step = pl.program_id(0) * nk + pl.program_id(1)
    nsteps = pl.num_programs(0) * nk
    slot = step % 2

    def bounds(st):
        o0 = (st // nk * (nk + 1) + st % nk) * ne
        lo8 = [(offs_ref[o0 + e] // 8) * 8 for e in range(ne)]
        hi = [offs_ref[o0 + ne + e] for e in range(ne)]
        return lo8, hi

    def window_copies(lo8, r, sl):
        starts = [pl.multiple_of(jnp.minimum(lo8[e] + r * w, total_rows - w), 8) for e in range(ne)]
        copies = [pltpu.make_async_copy(y_hbm.at[pl.ds(starts[e], w)], ybuf.at[sl, pl.ds(e * w, w)], sem.at[sl, e])
                  for e in range(ne)]
        return starts, copies

    @pl.when(step == 0)
    def _():
        for cp in window_copies(bounds(0)[0], 0, 0)[1]:
            cp.start()

    lo8, hi = bounds(step)
    starts0, copies0 = window_copies(lo8, 0, slot)
    for cp in copies0:
        cp.wait()

    @pl.when(step + 1 < nsteps)
    def _():
        for cp in window_copies(bounds(step + 1)[0], 0, 1 - slot)[1]:
            cp.start()

    pos = pos_ref[0]

    def contribution(r, starts):
        sub = lax.broadcasted_iota(I32, (w, tt), 0)
        want = [jnp.where(pos[e:e + 1, :] >= lo8[e] + r * w, pos[e:e + 1, :], -1) for e in range(ne)]
        blocks = [jnp.where(sub + starts[e] == want[e], 1.0, 0.0).astype(BF16) for e in range(ne)]
        onehot_t = jnp.concatenate(blocks, axis=0)
        return lax.dot_general(onehot_t, ybuf[slot], (((0,), (0,)), ((), ())), preferred_element_type=F32)

    acc_ref[...] = contribution(0, starts0)

    rounds = jnp.int32(1)
    for e in range(ne):
        rounds = jnp.maximum(rounds, (hi[e] - lo8[e] + w - 1) // w)

    def extra_round(r, carry):
        starts, copies = window_copies(lo8, r, slot)
        for cp in copies:
            cp.start()
        for cp in copies:
            cp.wait()
        acc_ref[...] += contribution(r, starts)
        return carry

    lax.fori_loop(1, rounds, extra_round, 0)
    x2 = x1_ref[0] + g2_ref[0] * acc_ref[...]
    o_ref[0] = _rms(x2) * fg_ref[...]


def _combine(offs_flat, pos, y, x1, g2, final_g, *, tt, w):
    b, s, d = x1.shape
    ne = pos.shape[1]
    total_rows = y.shape[0]
    return pl.pallas_call(
        functools.partial(_combine_kernel, tt=tt, w=w, total_rows=total_rows),
        grid_spec=pltpu.PrefetchScalarGridSpec(
            num_scalar_prefetch=1,
            grid=(b, s // tt),
            in_specs=[pl.BlockSpec((1, ne, tt), lambda bi, k, o: (bi, 0, k)),
                      pl.BlockSpec(memory_space=pl.ANY),
                      pl.BlockSpec((1, tt, d), lambda bi, k, o: (bi, k, 0)),
                      pl.BlockSpec((1, 1, d), lambda bi, k, o: (bi, 0, 0)),
                      pl.BlockSpec((1, d), lambda bi, k, o: (0, 0))],
            out_specs=pl.BlockSpec((1, tt, d), lambda bi, k, o: (bi, k, 0)),
            scratch_shapes=[pltpu.VMEM((2, ne * w, d), BF16), pltpu.VMEM((tt, d), F32),
                            pltpu.SemaphoreType.DMA((2, ne))]),
        out_shape=jax.ShapeDtypeStruct((b, s, d), F32),
        compiler_params=_params(("arbitrary", "arbitrary")),
        name="combine_final",
    )(offs_flat, pos, y, x1, g2, final_g)


def _rope_tables(s):
    n_rows = s // GRID_W
    row = np.repeat(np.arange(n_rows), GRID_W).astype(np.float64)
    col = np.tile(np.arange(GRID_W), n_rows).astype(np.float64)
    inv_freq = ROPE_THETA ** (-np.arange(ROPE_PAIRS, dtype=np.float64) / ROPE_PAIRS)
    ang_r = row[:, None] * inv_freq[None, :]
    ang_c = col[:, None] * inv_freq[None, :]
    cos64 = np.concatenate([np.cos(ang_r)] * 2 + [np.cos(ang_c)] * 2, axis=-1).astype(np.float32)
    sin64 = np.concatenate([-np.sin(ang_r), np.sin(ang_r), -np.sin(ang_c), np.sin(ang_c)], axis=-1).astype(np.float32)
    return cos64, sin64


ROPE_SWAP = np.concatenate([np.arange(16, 32), np.arange(0, 16), np.arange(48, 64), np.arange(32, 48)])


def kernel(x, c, ctx, c_ctx, w_mod, b_mod, norm1_g, norm2_g, w_in, conv_w, q_norm_g, w_uq, kv_norm_g, w_ukv, w_out,
           w_router, w_gate, w_up, w_down, final_g):
    b, s, d = x.shape
    lc = ctx.shape[1]
    assert w_in.shape[0] == 1, "single-layer stack"
    assert b <= 7 and s % max(ROW_TM, QKV_TM, ATT_TQ, ATT_TK, CMB_TT) == 0
    cap = EC_FACTOR * s // N_EXPERTS
    assert (b * cap) % FFN_T == 0 and cap % IDX_PC == 0

    cvec = jnp.zeros((8, d), F32).at[0:b].set(c).at[b].set(c_ctx)
    mod = _modulation(cvec, w_mod[0], b_mod[0][None, :])
    sh1, sc1, g1, sh2, sc2, g2 = [mod[0:b, None, i * d:(i + 1) * d] for i in range(6)]
    shc1, scc1 = mod[b:b + 1, None, 0:d], mod[b:b + 1, None, d:2 * d]

    n_main = 3 * CONV_WIDTH + Q_LORA + KV_LORA
    w_in_bf = w_in[0].astype(BF16)
    w_main = w_in_bf[:, 0:n_main]
    w_kr = w_in_bf[:, n_main:]
    w_kr2 = jnp.concatenate([w_kr, w_kr[:, ROPE_SWAP]], axis=-1)
    w_qn = w_uq[0, :, :, 0:QK_NOPE].reshape(Q_LORA, MLA_HEADS * QK_NOPE).astype(BF16)
    w_qr3 = w_uq[0, :, :, QK_NOPE:]
    w_qr = w_qr3.reshape(Q_LORA, MLA_HEADS * QK_ROPE).astype(BF16)
    w_qrs = w_qr3[:, :, ROPE_SWAP].reshape(Q_LORA, MLA_HEADS * QK_ROPE).astype(BF16)
    w_kn = w_ukv[0, :, :, 0:QK_NOPE].reshape(KV_LORA, MLA_HEADS * QK_NOPE).astype(BF16)
    w_v = w_ukv[0, :, :, QK_NOPE:].reshape(KV_LORA, MLA_HEADS * V_DIM).astype(BF16)
    w_o = w_out[0].astype(BF16)
    wr = jnp.zeros((d, LANES), F32).at[:, 0:N_EXPERTS].set(w_router[0])
    wr_hi = wr.astype(BF16)
    wr2 = jnp.concatenate([wr_hi, (wr - wr_hi.astype(F32)).astype(BF16)], axis=-1)

    cos64, sin64 = _rope_tables(s)
    cos2 = jnp.asarray(np.concatenate([cos64, cos64], axis=-1))
    sin2 = jnp.asarray(np.concatenate([sin64, sin64], axis=-1))
    ktab = jnp.asarray(np.concatenate([cos64, sin64], axis=-1))

    n1 = norm1_g[0][None, :]
    qg, kvg = q_norm_g[0][None, :], kv_norm_g[0][None, :]
    u, bg, cqn, ckvn, kr2 = _inproj(x, sc1, sh1, n1, w_main, w_kr2, qg, kvg, with_conv=True, tm=ROW_TM)
    _, ckvn_c, kr2_c = _inproj(ctx, scc1, shc1, n1, w_main, w_kr2, qg, kvg, with_conv=False, tm=lc)

    q = _qproj(cqn, cos2, sin2, w_qn, w_qr, w_qrs, tm=QKV_TM)
    k, v = _kvproj(ckvn, kr2, ktab, w_kn, w_v, tm=QKV_TM)
    kc, vc = _kvproj(ckvn_c, kr2_c, None, w_kn, w_v, tm=lc)
    y_attn = _attention(q, k, v, kc, vc, tq=ATT_TQ, tk=ATT_TK)

    x1, tab, aff_t = _outproj(u, bg, conv_w[0], y_attn, w_o, x, g1, norm2_g[0][None, :], sc2, sh2, wr2, tm=ROW_TM)

    pos, offs = _routing(aff_t, cap=cap, blk=CMB_TT)
    nk1 = s // CMB_TT + 1
    offs_flat = jnp.swapaxes(offs[:, :, 0:nk1], 1, 2).reshape(-1)
    idx_t = _slots(offs_flat, pos, cap=cap, blk=CMB_TT, w=CMB_W)
    idx_flat = jnp.transpose(idx_t, (2, 0, 1)).reshape(-1)
    y = _expert_ffn(idx_flat, tab.reshape(b * s, d + LANES), w_gate[0], w_up[0], w_down[0], t=FFN_T)
    return _combine(offs_flat, pos, y, x1, g2, final_g[None, :], tt=CMB_TT, w=CMB_W)
```

```python
import functools
import math

import jax
import jax.numpy as jnp
import numpy as np
from jax import lax
from jax.experimental import pallas as pl
from jax.experimental.pallas import tpu as pltpu

F32 = jnp.float32
BF16 = jnp.bfloat16
I32 = jnp.int32

GRID_W = 64
CONV_WIDTH = 1024
MLA_HEADS = 8
QK_NOPE = 128
QK_ROPE = 64
V_DIM = 128
Q_LORA = 512
KV_LORA = 512
QK_DIM = QK_NOPE + QK_ROPE
MLA_WIDTH = MLA_HEADS * V_DIM
V_EXT = 2 * V_DIM
N_EXPERTS = 16
EC_FACTOR = 2
ROPE_THETA = 10000.0
ROPE_PAIRS = QK_ROPE // 4
ATTN_SCALE = 1.0 / math.sqrt(QK_DIM)
LOG2E = math.log2(math.e)
EPS = 1e-6

LANES = 128
VMEM_LIMIT = 56 * 1024 * 1024
FFN_VMEM_LIMIT = 60 * 1024 * 1024

MOD_TN = 2048
ROW_TM = 512
QKV_TM = 1024
ATT_TQ = 1024
ATT_TK = 512
FFN_T = 256
FFN_HEAD_ROWS = 96
CMB_TT = 256
CMB_W = 64


def _dot(a, b):
    return jnp.dot(a, b, preferred_element_type=F32)


def _params(sem, vmem=VMEM_LIMIT):
    return pltpu.CompilerParams(dimension_semantics=sem, vmem_limit_bytes=vmem)


def _resident(shape):
    nd = len(shape)
    return pl.BlockSpec(shape, lambda *_: (0,) * nd, pipeline_mode=pl.Buffered(1))


def _mod_kernel(c_ref, w_ref, b_ref, o_ref):
    cv = c_ref[...]
    a = cv / (1.0 + jnp.exp(-cv))
    a_hi = a.astype(BF16)
    a_lo = (a - a_hi.astype(F32)).astype(BF16)
    w = w_ref[...].astype(BF16)
    o_ref[...] = _dot(a_hi, w) + _dot(a_lo, w) + b_ref[...]


def _modulation(cvec, w_mod, b_mod):
    d, n = w_mod.shape
    return pl.pallas_call(
        _mod_kernel,
        grid=(n // MOD_TN,),
        in_specs=[pl.BlockSpec((8, d), lambda j: (0, 0)),
                  pl.BlockSpec((d, MOD_TN), lambda j: (0, j)),
                  pl.BlockSpec((1, MOD_TN), lambda j: (0, j))],
        out_specs=pl.BlockSpec((8, MOD_TN), lambda j: (0, j)),
        out_shape=jax.ShapeDtypeStruct((8, n), F32),
        compiler_params=_params(("arbitrary",)),
        name="modulation",
    )(cvec, w_mod, b_mod)


def _rms(v):
    return v * lax.rsqrt(jnp.mean(v * v, axis=-1, keepdims=True) + EPS)


def _inproj_kernel(x_ref, sc_ref, sh_ref, g_ref, w_ref, wkr_ref, qg_ref, kvg_ref, *outs, with_conv):
    h = _rms(x_ref[0]) * g_ref[...]
    h = h * (1.0 + sc_ref[0]) + sh_ref[0]
    hb = h.astype(BF16)
    c = CONV_WIDTH
    if with_conv:
        u_ref, bg_ref, cq_ref, ckv_ref, kr_ref = outs
        xin = _dot(hb, w_ref[:, 0:c])
        cg = _dot(hb, w_ref[:, 2 * c:3 * c])
        u_ref[0] = (cg * xin).astype(BF16)
        bg_ref[0] = _dot(hb, w_ref[:, c:2 * c]).astype(BF16)
    else:
        cq_ref, ckv_ref, kr_ref = outs
    o = 3 * c
    cq = _dot(hb, w_ref[:, o:o + Q_LORA])
    cq_ref[0] = (_rms(cq) * qg_ref[...]).astype(BF16)
    ckv = _dot(hb, w_ref[:, o + Q_LORA:o + Q_LORA + KV_LORA])
    ckv_ref[0] = (_rms(ckv) * kvg_ref[...]).astype(BF16)
    kr_ref[0] = _dot(hb, wkr_ref[...])


def _inproj(x, scale, shift, gain, w_main, w_kr2, q_g, kv_g, *, with_conv, tm):
    b, s, d = x.shape
    per_batch = scale.shape[0] > 1
    vec = pl.BlockSpec((1, 1, d), (lambda bi, i: (bi, 0, 0)) if per_batch else (lambda bi, i: (0, 0, 0)))
    row = lambda n: pl.BlockSpec((1, tm, n), lambda bi, i: (bi, i, 0))
    outs, specs = [], []
    if with_conv:
        outs += [jax.ShapeDtypeStruct((b, s, CONV_WIDTH), BF16)] * 2
        specs += [row(CONV_WIDTH)] * 2
    outs += [jax.ShapeDtypeStruct((b, s, Q_LORA), BF16), jax.ShapeDtypeStruct((b, s, KV_LORA), BF16),
             jax.ShapeDtypeStruct((b, s, 2 * QK_ROPE), F32)]
    specs += [row(Q_LORA), row(KV_LORA), row(2 * QK_ROPE)]
    return pl.pallas_call(
        functools.partial(_inproj_kernel, with_conv=with_conv),
        grid=(b, s // tm),
        in_specs=[row(d), vec, vec, _resident((1, d)), _resident(w_main.shape), _resident(w_kr2.shape),
                  _resident((1, Q_LORA)), _resident((1, KV_LORA))],
        out_specs=specs,
        out_shape=outs,
        compiler_params=_params(("arbitrary", "arbitrary")),
        name="inproj_conv" if with_conv else "inproj_ctx",
    )(x, scale, shift, gain, w_main, w_kr2, q_g, kv_g)


def _qproj_kernel(cq_ref, cos_ref, sin_ref, wn_ref, wr_ref, wrs_ref, q_ref):
    cq = cq_ref[0]
    qn = _dot(cq, wn_ref[...])
    qr = _dot(cq, wr_ref[...])
    qrs = _dot(cq, wrs_ref[...])
    reps = MLA_HEADS * QK_ROPE // LANES
    cos = jnp.concatenate([cos_ref[...]] * reps, axis=-1)
    sin = jnp.concatenate([sin_ref[...]] * reps, axis=-1)
    qrot = qr * cos + qrs * sin
    sc = ATTN_SCALE * LOG2E
    for h in range(MLA_HEADS):
        q_ref[0, h, :, 0:QK_NOPE] = (qn[:, h * QK_NOPE:(h + 1) * QK_NOPE] * sc).astype(BF16)
        q_ref[0, h, :, QK_NOPE:QK_DIM] = (qrot[:, h * QK_ROPE:(h + 1) * QK_ROPE] * sc).astype(BF16)


def _qproj(cqn, cos2, sin2, w_qn, w_qr, w_qrs, *, tm):
    b, s, r = cqn.shape
    return pl.pallas_call(
        _qproj_kernel,
        grid=(b, s // tm),
        in_specs=[pl.BlockSpec((1, tm, r), lambda bi, i: (bi, i, 0)),
                  pl.BlockSpec((tm, LANES), lambda bi, i: (i, 0)),
                  pl.BlockSpec((tm, LANES), lambda bi, i: (i, 0)),
                  _resident(w_qn.shape), _resident(w_qr.shape), _resident(w_qrs.shape)],
        out_specs=pl.BlockSpec((1, MLA_HEADS, tm, QK_DIM), lambda bi, i: (bi, 0, i, 0)),
        out_shape=jax.ShapeDtypeStruct((b, MLA_HEADS, s, QK_DIM), BF16),
        compiler_params=_params(("arbitrary", "arbitrary")),
        name="qproj",
    )(cqn, cos2, sin2, w_qn, w_qr, w_qrs)


def _kvproj_kernel(ckv_ref, kr_ref, *rest, rope):
    if rope:
        tab_ref, wk_ref, wv_ref, k_ref, v_ref = rest
        prod = kr_ref[0] * tab_ref[...]
        krot = prod[:, 0:QK_ROPE] + prod[:, QK_ROPE:2 * QK_ROPE]
    else:
        wk_ref, wv_ref, k_ref, v_ref = rest
        krot = kr_ref[0][:, 0:QK_ROPE]
    ckv = ckv_ref[0]
    kn = _dot(ckv, wk_ref[...])
    vv = _dot(ckv, wv_ref[...])
    krot = krot.astype(BF16)
    for h in range(MLA_HEADS):
        k_ref[0, h, :, 0:QK_NOPE] = kn[:, h * QK_NOPE:(h + 1) * QK_NOPE].astype(BF16)
        k_ref[0, h, :, QK_NOPE:QK_DIM] = krot
        v_ref[0, h, :, 0:V_DIM] = vv[:, h * V_DIM:(h + 1) * V_DIM].astype(BF16)
        v_ref[0, h, :, V_DIM:V_EXT] = jnp.ones((vv.shape[0], V_EXT - V_DIM), BF16)


def _kvproj(ckvn, kr2, tab, w_kn, w_v, *, tm):
    b, s, r = ckvn.shape
    rope = tab is not None
    ins = [ckvn, kr2]
    specs = [pl.BlockSpec((1, tm, r), lambda bi, i: (bi, i, 0)),
             pl.BlockSpec((1, tm, 2 * QK_ROPE), lambda bi, i: (bi, i, 0))]
    if rope:
        ins.append(tab)
        specs.append(pl.BlockSpec((tm, 2 * QK_ROPE), lambda bi, i: (i, 0)))
    ins += [w_kn, w_v]
    specs += [_resident(w_kn.shape), _resident(w_v.shape)]
    return pl.pallas_call(
        functools.partial(_kvproj_kernel, rope=rope),
        grid=(b, s // tm),
        in_specs=specs,
        out_specs=[pl.BlockSpec((1, MLA_HEADS, tm, QK_DIM), lambda bi, i: (bi, 0, i, 0)),
                   pl.BlockSpec((1, MLA_HEADS, tm, V_EXT), lambda bi, i: (bi, 0, i, 0))],
        out_shape=[jax.ShapeDtypeStruct((b, MLA_HEADS, s, QK_DIM), BF16),
                   jax.ShapeDtypeStruct((b, MLA_HEADS, s, V_EXT), BF16)],
        compiler_params=_params(("arbitrary", "arbitrary")),
        name="kvproj_rope" if rope else "kvproj_ctx",
    )(*ins)


def _attn_kernel(q_ref, k_ref, v_ref, kc_ref, vc_ref, o_ref, m_ref, acc_ref, sa_ref, sb_ref, *, tk):
    q = q_ref[0, 0]
    m_ref[...] = jnp.full(m_ref.shape, -jnp.inf, F32)
    acc_ref[...] = jnp.zeros(acc_ref.shape, F32)

    def lanes(a, n):
        return jnp.concatenate([a] * (n // LANES), axis=1)

    def scores(k):
        return lax.dot_general(q, k, (((1,), (1,)), ((), ())), preferred_element_type=F32)

    def update(s, v):
        m_old = m_ref[...]
        m_new = jnp.maximum(m_old, jnp.max(s, axis=-1, keepdims=True))
        alpha = jnp.exp2(m_old - m_new)
        p = jnp.exp2(s - lanes(m_new, s.shape[1]))
        acc_ref[...] = lanes(alpha, acc_ref.shape[1]) * acc_ref[...] + _dot(p.astype(BF16), v)
        m_ref[...] = m_new

    def chunk(ref, j):
        return ref[0, 0, pl.ds(pl.multiple_of(j * tk, tk), tk), :]

    n = k_ref.shape[2] // tk
    bufs = (sa_ref, sb_ref)
    sa_ref[...] = scores(chunk(k_ref, 0))
    for j in range(n):
        nxt = bufs[(j + 1) % 2]
        if j + 1 < n:
            nxt[...] = scores(chunk(k_ref, j + 1))
        else:
            s_ctx = scores(kc_ref[0, 0])
        update(bufs[j % 2][...], chunk(v_ref, j))
    update(s_ctx, vc_ref[0, 0])
    o_ref[0] = (acc_ref[:, 0:V_DIM] / acc_ref[:, V_DIM:V_EXT]).astype(BF16)


def _attention(q, k, v, kc, vc, *, tq, tk):
    b, h, s, dq = q.shape
    lc = kc.shape[2]
    return pl.pallas_call(
        functools.partial(_attn_kernel, tk=tk),
        grid=(b, h, s // tq),
        in_specs=[pl.BlockSpec((1, 1, tq, dq), lambda bi, hi, i: (bi, hi, i, 0)),
                  pl.BlockSpec((1, 1, s, dq), lambda bi, hi, i: (bi, hi, 0, 0)),
                  pl.BlockSpec((1, 1, s, V_EXT), lambda bi, hi, i: (bi, hi, 0, 0)),
                  pl.BlockSpec((1, 1, lc, dq), lambda bi, hi, i: (bi, hi, 0, 0)),
                  pl.BlockSpec((1, 1, lc, V_EXT), lambda bi, hi, i: (bi, hi, 0, 0))],
        out_specs=pl.BlockSpec((1, tq, V_DIM), lambda bi, hi, i: (bi, i, hi)),
        out_shape=jax.ShapeDtypeStruct((b, s, h * V_DIM), BF16),
        scratch_shapes=[pltpu.VMEM((tq, LANES), F32), pltpu.VMEM((tq, V_EXT), F32),
                        pltpu.VMEM((tq, tk), F32), pltpu.VMEM((tq, tk), F32)],
        compiler_params=_params(("arbitrary", "arbitrary", "arbitrary")),
        name="attention",
    )(q, k, v, kc, vc)


HALO = 16


def _outproj_kernel(u_ref, up_ref, un_ref, bg_ref, cw_ref, ya_ref, wo_ref, x_ref, g1_ref, n2_ref, sc_ref, sh_ref,
                    wr2_ref, x1_ref, tab_ref, afft_ref, *, tm, d):
    i = pl.program_id(1)
    last = pl.num_programs(1) - 1
    u = u_ref[0].astype(F32)
    rows = lax.broadcasted_iota(I32, (tm, 1), 0)
    prev_row = jnp.where(i > 0, up_ref[0][HALO - 1:HALO, :].astype(F32), 0.0)
    next_row = jnp.where(i < last, un_ref[0][0:1, :].astype(F32), 0.0)
    um1 = jnp.where(rows == 0, prev_row, pltpu.roll(u, 1, 0))
    up1 = jnp.where(rows == tm - 1, next_row, pltpu.roll(u, tm - 1, 0))
    cw = cw_ref[...]
    yc = (bg_ref[0].astype(F32) * (um1 * cw[0:1] + u * cw[1:2] + up1 * cw[2:3])).astype(BF16)
    nsub = 2
    tr = tm // nsub
    for r in range(nsub):
        rs = slice(r * tr, (r + 1) * tr)
        y = _dot(yc[rs], wo_ref[0:CONV_WIDTH, :]) + _dot(ya_ref[0, rs, :], wo_ref[CONV_WIDTH:, :])
        x1 = x_ref[0, rs, :] + g1_ref[0] * y
        x1_ref[0, rs, :] = x1
        h2 = _rms(x1) * n2_ref[...]
        h2 = h2 * (1.0 + sc_ref[0]) + sh_ref[0]
        tab_ref[0, rs, 0:d] = h2
        h_hi = h2.astype(BF16)
        h_lo = (h2 - h_hi.astype(F32)).astype(BF16)
        l2 = _dot(h_hi, wr2_ref[...])
        logits = l2[:, 0:LANES] + l2[:, LANES:2 * LANES] + _dot(h_lo, wr2_ref[:, 0:LANES])
        lane = lax.broadcasted_iota(I32, logits.shape, 1)
        logits = jnp.where(lane < N_EXPERTS, logits, -jnp.inf)
        e = jnp.exp(logits - jnp.max(logits, axis=-1, keepdims=True))
        aff = e / jnp.sum(e, axis=-1, keepdims=True)
        tab_ref[0, rs, d:d + LANES] = aff
        afft_ref[0, :, rs] = aff.T[0:N_EXPERTS, :]


def _outproj(u, bg, conv_w, y_attn, w_out, x, g1, n2, sc2, sh2, wr2, *, tm):
    b, s, d = x.shape
    nh = tm // HALO
    nhb = s // HALO
    row = lambda n: pl.BlockSpec((1, tm, n), lambda bi, i: (bi, i, 0))
    vec = pl.BlockSpec((1, 1, d), lambda bi, i: (bi, 0, 0))
    return pl.pallas_call(
        functools.partial(_outproj_kernel, tm=tm, d=d),
        grid=(b, s // tm),
        in_specs=[row(CONV_WIDTH),
                  pl.BlockSpec((1, HALO, CONV_WIDTH), lambda bi, i: (bi, jnp.maximum(i * nh - 1, 0), 0)),
                  pl.BlockSpec((1, HALO, CONV_WIDTH), lambda bi, i: (bi, jnp.minimum((i + 1) * nh, nhb - 1), 0)),
                  row(CONV_WIDTH), _resident(conv_w.shape), row(MLA_WIDTH), _resident(w_out.shape), row(d),
                  vec, _resident((1, d)), vec, vec, _resident(wr2.shape)],
        out_specs=[row(d), row(d + LANES), pl.BlockSpec((1, N_EXPERTS, tm), lambda bi, i: (bi, 0, i))],
        out_shape=[jax.ShapeDtypeStruct((b, s, d), F32), jax.ShapeDtypeStruct((b, s, d + LANES), F32),
                   jax.ShapeDtypeStruct((b, N_EXPERTS, s), F32)],
        compiler_params=_params(("arbitrary", "arbitrary")),
        name="outproj_router",
    )(u, u, u, bg, conv_w, y_attn, w_out, x, g1, n2, sc2, sh2, wr2)


def _routing_kernel(aff_ref, tri_ref, pos_ref, offs_ref, *, cap, blk, rows_per_expert):
    bi = pl.program_id(0)
    aff = aff_ref[0]
    ne, s = aff.shape
    capf = jnp.float32(cap)

    def bisect(t, prefix):
        cand = prefix | jnp.left_shift(jnp.int32(1), 30 - t)
        cnt = jnp.sum(jnp.where(aff >= lax.bitcast_convert_type(cand, F32), 1.0, 0.0), axis=1, keepdims=True)
        return jnp.where(cnt >= capf, cand, prefix)

    floor_bits = lax.fori_loop(0, 31, bisect, jnp.zeros((ne, 1), I32))
    thr = jnp.min(jnp.where(aff >= lax.bitcast_convert_type(floor_bits, F32), aff, jnp.inf), axis=1, keepdims=True)
    gt = aff > thr
    eq = aff == thr
    need = capf - jnp.sum(jnp.where(gt, 1.0, 0.0), axis=1, keepdims=True)
    tri = tri_ref[...]

    def cumsum_blocks(mask_f32):
        run = jnp.zeros((ne, 1), F32)
        parts, starts = [], []
        for kb in range(s // blk):
            c = _dot(mask_f32[:, kb * blk:(kb + 1) * blk].astype(BF16), tri)
            starts.append(run)
            parts.append(c + run)
            run = run + c[:, blk - 1:blk]
        return jnp.concatenate(parts, axis=1), starts, run

    eqf = jnp.where(eq, 1.0, 0.0)
    eq_incl, _, _ = cumsum_blocks(eqf)
    sel = jnp.where(gt, 1.0, jnp.where(eq & (eq_incl - eqf < need), 1.0, 0.0))
    incl, starts, _ = cumsum_blocks(sel)

    base = (lax.broadcasted_iota(I32, (ne, 1), 0) * rows_per_expert + bi * cap)
    pos_ref[0] = jnp.where(sel > 0.0, (incl - sel).astype(I32) + base, -1)
    lane = lax.broadcasted_iota(I32, (ne, LANES), 1)
    offs = jnp.full((ne, LANES), cap, I32) + base
    for kb, st in enumerate(starts):
        offs = jnp.where(lane == kb, st.astype(I32) + base, offs)
    offs_ref[0] = offs


def _routing(aff_t, *, cap, blk):
    b, ne, s = aff_t.shape
    assert s // blk + 1 <= LANES
    tri = jnp.asarray(np.triu(np.ones((blk, blk), np.float32)), BF16)
    return pl.pallas_call(
        functools.partial(_routing_kernel, cap=cap, blk=blk, rows_per_expert=b * cap),
        grid=(b,),
        in_specs=[pl.BlockSpec((1, ne, s), lambda bi: (bi, 0, 0)), _resident(tri.shape)],
        out_specs=[pl.BlockSpec((1, ne, s), lambda bi: (bi, 0, 0)),
                   pl.BlockSpec((1, ne, LANES), lambda bi: (bi, 0, 0))],
        out_shape=[jax.ShapeDtypeStruct((b, ne, s), I32), jax.ShapeDtypeStruct((b, ne, LANES), I32)],
        compiler_params=_params(("arbitrary",)),
        name="routing",
    )(aff_t, tri)


def _slots_kernel(offs_ref, pos_ref, idx_ref, *, cap, blk, w, rows_per_expert):
    bi = pl.program_id(0)
    ne, s = pos_ref.shape[1], pos_ref.shape[2]
    nb = s // blk
    idx_ref[...] = jnp.zeros(idx_ref.shape, I32)
    sub = lax.broadcasted_iota(I32, (w, blk), 0)
    tok = lax.broadcasted_iota(I32, (w, blk), 1)

    def block(kb, carry):
        tok1 = tok + (kb * blk + bi * s + 1)
        windows, trips = [], []
        for e in range(ne):
            base = e * rows_per_expert + bi * cap
            o = (bi * (nb + 1) + kb) * ne + e
            lo8 = (offs_ref[o] - base) // 8 * 8
            hi = offs_ref[o + ne] - base
            posrow = pos_ref[0, e:e + 1, pl.ds(pl.multiple_of(kb * blk, blk), blk)] - base

            def window(r, c, e=e, lo8=lo8, posrow=posrow):
                start = pl.multiple_of(jnp.minimum(lo8 + r * w, cap - w), 8)
                hit = jnp.where(sub + start == posrow, tok1, 0)
                val = jnp.sum(hit.astype(F32), axis=1, keepdims=True).astype(I32)
                old = idx_ref[0, pl.ds(start, w), e:e + 1]
                idx_ref[0, pl.ds(start, w), e:e + 1] = jnp.where(val > 0, val - 1, old)
                return c

            windows.append(window)
            trips.append((hi - lo8 + w - 1) // w)
        for window in windows:
            window(0, 0)
        for window, n in zip(windows, trips):
            lax.fori_loop(1, n, window, 0)
        return carry

    lax.fori_loop(0, nb, block, 0)


def _slots(offs_flat, pos, *, cap, blk, w):
    b, ne, s = pos.shape
    return pl.pallas_call(
        functools.partial(_slots_kernel, cap=cap, blk=blk, w=w, rows_per_expert=b * cap),
        grid_spec=pltpu.PrefetchScalarGridSpec(
            num_scalar_prefetch=1,
            grid=(b,),
            in_specs=[pl.BlockSpec((1, ne, s), lambda bi, o: (bi, 0, 0))],
            out_specs=pl.BlockSpec((1, cap, ne), lambda bi, o: (bi, 0, 0))),
        out_shape=jax.ShapeDtypeStruct((b, cap, ne), I32),
        compiler_params=_params(("arbitrary",)),
        name="slots",
    )(offs_flat, pos)


def _ffn_kernel(idx_ref, tab_hbm, wg_hbm, wu_hbm, wd_hbm, y_ref, buf_a, buf_b, buf_c, wg_ref, wu_ref, wd_ref,
                sg_ref, su_ref, sd_ref, sem, wsem, *, t, d):
    e = pl.program_id(0)
    j = pl.program_id(1)
    ne, nt = pl.num_programs(0), pl.num_programs(1)
    step = e * nt + j
    nsteps = ne * nt
    rc, fc = sg_ref.shape[0], sd_ref.shape[0]

    def row_copy(base, i, buf, sl):
        return pltpu.make_async_copy(tab_hbm.at[pl.ds(idx_ref[base + i], 1)], buf.at[pl.ds(i, 1)], sem.at[sl])

    def wait_rows(buf, sl):
        pltpu.make_async_copy(tab_hbm.at[pl.ds(0, t)], buf, sem.at[sl]).wait()

    def piece_copies(ex, pc):
        return [pltpu.make_async_copy(wg_hbm.at[ex, pl.ds(pl.multiple_of(pc * rc, rc), rc)], sg_ref, wsem.at[0]),
                pltpu.make_async_copy(wu_hbm.at[ex, pl.ds(pl.multiple_of(pc * rc, rc), rc)], su_ref, wsem.at[1]),
                pltpu.make_async_copy(wd_hbm.at[ex, pl.ds(pl.multiple_of(pc * fc, fc), fc)], sd_ref, wsem.at[2])]

    def cast_piece(st, pc):
        wg_ref[st, pl.ds(pl.multiple_of(pc * rc, rc), rc), :] = sg_ref[...].astype(BF16)
        wu_ref[st, pl.ds(pl.multiple_of(pc * rc, rc), rc), :] = su_ref[...].astype(BF16)
        wd_ref[st, pl.ds(pl.multiple_of(pc * fc, fc), fc), :] = sd_ref[...].astype(BF16)

    @pl.when(step == 0)
    def _():
        def issue(i, carry):
            row_copy(0, i, buf_a, 0).start()
            row_copy(jnp.minimum(1, nsteps - 1) * t, i, buf_b, 1).start()
            return carry
        lax.fori_loop(0, t, issue, 0)

        def load(pc, carry):
            cps = piece_copies(0, pc)
            for cp in cps:
                cp.start()
            for cp in cps:
                cp.wait()
            cast_piece(0, pc)
            return carry
        lax.fori_loop(0, nt - 1, load, 0)
        for cp in piece_copies(0, nt - 1):
            cp.start()

    pe = jnp.where(j > 0, e, e - 1)
    pj = jnp.where(j > 0, j - 1, nt - 1)
    nxt_e = jnp.minimum(e + 1, ne - 1)
    ws = e % 2

    def run(cur, cur_sl, mid, mid_sl, far, far_sl):
        wait_rows(cur, cur_sl)
        for cp in piece_copies(jnp.minimum(pe + 1, ne - 1), pj):
            cp.wait()
        cast_piece((pe + 1) % 2, pj)
        for cp in piece_copies(nxt_e, j):
            cp.start()
        base = jnp.minimum(step + 2, nsteps - 1) * t
        head_rows = FFN_HEAD_ROWS * t // FFN_T
        for i in range(head_rows):
            row_copy(base, i, far, far_sl).start()

        @pl.when(idx_ref[base] < 0)
        def _():
            y_ref[0:16, 0:LANES] = jnp.zeros((16, LANES), BF16)

        for i in range(head_rows, t):
            row_copy(base, i, far, far_sl).start()
        xb = cur[:, 0:d].astype(BF16)
        aff = cur[:, d:d + LANES]
        lane = lax.broadcasted_iota(I32, aff.shape, 1)
        gate_w = jnp.sum(jnp.where(lane == e, aff, 0.0), axis=-1, keepdims=True)
        g = _dot(xb, wg_ref[ws])
        up = _dot(xb, wu_ref[ws])
        hid = ((g / (1.0 + jnp.exp(-g))) * up).astype(BF16)
        y = _dot(hid, wd_ref[ws])
        y_ref[...] = (y * gate_w).astype(BF16)

        @pl.when(step == nsteps - 1)
        def _():
            wait_rows(mid, mid_sl)
            wait_rows(far, far_sl)
            for cp in piece_copies(nxt_e, j):
                cp.wait()

    bufs = (buf_a, buf_b, buf_c)
    for r in range(3):
        @pl.when(step % 3 == r)
        def _(r=r):
            run(bufs[r], r, bufs[(r + 1) % 3], (r + 1) % 3, bufs[(r + 2) % 3], (r + 2) % 3)


def _expert_ffn(idx_flat, tab, w_gate, w_up, w_down, *, t):
    ne, d, ff = w_gate.shape
    rows = idx_flat.shape[0]
    nt = rows // ne // t
    dw = tab.shape[1]
    bf16_rows = 16
    assert d % (nt * bf16_rows) == 0 and ff % (nt * bf16_rows) == 0
    return pl.pallas_call(
        functools.partial(_ffn_kernel, t=t, d=d),
        grid_spec=pltpu.PrefetchScalarGridSpec(
            num_scalar_prefetch=1,
            grid=(ne, nt),
            in_specs=[pl.BlockSpec(memory_space=pl.ANY)] * 4,
            out_specs=pl.BlockSpec((t, d), lambda e, j, idx: (e * nt + j, 0)),
            scratch_shapes=[pltpu.VMEM((t, dw), F32), pltpu.VMEM((t, dw), F32), pltpu.VMEM((t, dw), F32),
                            pltpu.VMEM((2, d, ff), BF16), pltpu.VMEM((2, d, ff), BF16), pltpu.VMEM((2, ff, d), BF16),
                            pltpu.VMEM((d // nt, ff), F32), pltpu.VMEM((d // nt, ff), F32),
                            pltpu.VMEM((ff // nt, d), F32),
                            pltpu.SemaphoreType.DMA((3,)), pltpu.SemaphoreType.DMA((3,))]),
        out_shape=jax.ShapeDtypeStruct((rows, d), BF16),
        compiler_params=_params(("arbitrary", "arbitrary"), vmem=FFN_VMEM_LIMIT),
        name="expert_ffn",
    )(idx_flat, tab, w_gate, w_up, w_down)


def _combine_kernel(offs_ref, pos_ref, y_hbm, x1_ref, g2_ref, fg_ref, o_ref, ybuf, acc_ref, sem, *, tt, w, total_rows):
    ne = pos_ref.shape[1]
    nk = pl.num_programs(1)
    step = pl.program_id(0) * nk + pl.program_id(1)
    nsteps = pl.num_programs(0) * nk
    slot = step % 2

    def bounds(st):
        o0 = (st // nk * (nk + 1) + st % nk) * ne
        lo8 = [(offs_ref[o0 + e] // 8) * 8 for e in range(ne)]
        hi = [offs_ref[o0 + ne + e] for e in range(ne)]
        return lo8, hi

    def window_copies(lo8, r, sl):
        starts = [pl.multiple_of(jnp.minimum(lo8[e] + r * w, total_rows - w), 8) for e in range(ne)]
        copies = [pltpu.make_async_copy(y_hbm.at[pl.ds(starts[e], w)], ybuf.at[sl, pl.ds(e * w, w)], sem.at[sl, e])
                  for e in range(ne)]
        return starts, copies

    @pl.when(step == 0)
    def _():
        for cp in window_copies(bounds(0)[0], 0, 0)[1]:
            cp.start()

    lo8, hi = bounds(step)
    starts0, copies0 = window_copies(lo8, 0, slot)
    for cp in copies0:
        cp.wait()

    @pl.when(step + 1 < nsteps)
    def _():
        for cp in window_copies(bounds(step + 1)[0], 0, 1 - slot)[1]:
            cp.start()

    pos = pos_ref[0]

    def contribution(r, starts):
        sub = lax.broadcasted_iota(I32, (w, tt), 0)
        want = [jnp.where(pos[e:e + 1, :] >= lo8[e] + r * w, pos[e:e + 1, :], -1) for e in range(ne)]
        blocks = [jnp.where(sub + starts[e] == want[e], 1.0, 0.0).astype(BF16) for e in range(ne)]
        onehot_t = jnp.concatenate(blocks, axis=0)
        return lax.dot_general(onehot_t, ybuf[slot], (((0,), (0,)), ((), ())), preferred_element_type=F32)

    acc_ref[...] = contribution(0, starts0)

    rounds = jnp.int32(1)
    for e in range(ne):
        rounds = jnp.maximum(rounds, (hi[e] - lo8[e] + w - 1) // w)

    def extra_round(r, carry):
        starts, copies = window_copies(lo8, r, slot)
        for cp in copies:
            cp.start()
        for cp in copies:
            cp.wait()
        acc_ref[...] += contribution(r, starts)
        return carry

    lax.fori_loop(1, rounds, extra_round, 0)
    x2 = x1_ref[0] + g2_ref[0] * acc_ref[...]
    o_ref[0] = _rms(x2) * fg_ref[...]


def _combine(offs_flat, pos, y, x1, g2, final_g, *, tt, w):
    b, s, d = x1.shape
    ne = pos.shape[1]
    total_rows = y.shape[0]
    return pl.pallas_call(
        functools.partial(_combine_kernel, tt=tt, w=w, total_rows=total_rows),
        grid_spec=pltpu.PrefetchScalarGridSpec(
            num_scalar_prefetch=1,
            grid=(b, s // tt),
            in_specs=[pl.BlockSpec((1, ne, tt), lambda bi, k, o: (bi, 0, k)),
                      pl.BlockSpec(memory_space=pl.ANY),
                      pl.BlockSpec((1, tt, d), lambda bi, k, o: (bi, k, 0)),
                      pl.BlockSpec((1, 1, d), lambda bi, k, o: (bi, 0, 0)),
                      pl.BlockSpec((1, d), lambda bi, k, o: (0, 0))],
            out_specs=pl.BlockSpec((1, tt, d), lambda bi, k, o: (bi, k, 0)),
            scratch_shapes=[pltpu.VMEM((2, ne * w, d), BF16), pltpu.VMEM((tt, d), F32),
                            pltpu.SemaphoreType.DMA((2, ne))]),
        out_shape=jax.ShapeDtypeStruct((b, s, d), F32),
        compiler_params=_params(("arbitrary", "arbitrary")),
        name="combine_final",
    )(offs_flat, pos, y, x1, g2, final_g)


def _rope_tables(s):
    n_rows = s // GRID_W
    row = np.repeat(np.arange(n_rows), GRID_W).astype(np.float64)
    col = np.tile(np.arange(GRID_W), n_rows).astype(np.float64)
    inv_freq = ROPE_THETA ** (-np.arange(ROPE_PAIRS, dtype=np.float64) / ROPE_PAIRS)
    ang_r = row[:, None] * inv_freq[None, :]
    ang_c = col[:, None] * inv_freq[None, :]
    cos64 = np.concatenate([np.cos(ang_r)] * 2 + [np.cos(ang_c)] * 2, axis=-1).astype(np.float32)
    sin64 = np.concatenate([-np.sin(ang_r), np.sin(ang_r), -np.sin(ang_c), np.sin(ang_c)], axis=-1).astype(np.float32)
    return cos64, sin64


ROPE_SWAP = np.concatenate([np.arange(16, 32), np.arange(0, 16), np.arange(48, 64), np.arange(32, 48)])


def kernel(x, c, ctx, c_ctx, w_mod, b_mod, norm1_g, norm2_g, w_in, conv_w, q_norm_g, w_uq, kv_norm_g, w_ukv, w_out,
           w_router, w_gate, w_up, w_down, final_g):
    b, s, d = x.shape
    lc = ctx.shape[1]
    assert w_in.shape[0] == 1, "single-layer stack"
    assert b <= 7 and s % max(ROW_TM, QKV_TM, ATT_TQ, ATT_TK, CMB_TT) == 0
    cap = EC_FACTOR * s // N_EXPERTS
    assert (b * cap) % FFN_T == 0 and cap % CMB_W == 0

    cvec = jnp.zeros((8, d), F32).at[0:b].set(c).at[b].set(c_ctx)
    mod = _modulation(cvec, w_mod[0], b_mod[0][None, :])
    sh1, sc1, g1, sh2, sc2, g2 = [mod[0:b, None, i * d:(i + 1) * d] for i in range(6)]
    shc1, scc1 = mod[b:b + 1, None, 0:d], mod[b:b + 1, None, d:2 * d]

    n_main = 3 * CONV_WIDTH + Q_LORA + KV_LORA
    w_main = w_in[0, :, 0:n_main].astype(BF16)
    w_kr = w_in[0, :, n_main:]
    w_kr2 = jnp.concatenate([w_kr, w_kr[:, ROPE_SWAP]], axis=-1).astype(BF16)
    w_qn = w_uq[0, :, :, 0:QK_NOPE].reshape(Q_LORA, MLA_HEADS * QK_NOPE).astype(BF16)
    w_qr3 = w_uq[0, :, :, QK_NOPE:]
    w_qr = w_qr3.reshape(Q_LORA, MLA_HEADS * QK_ROPE).astype(BF16)
    w_qrs = w_qr3[:, :, ROPE_SWAP].reshape(Q_LORA, MLA_HEADS * QK_ROPE).astype(BF16)
    w_kn = w_ukv[0, :, :, 0:QK_NOPE].reshape(KV_LORA, MLA_HEADS * QK_NOPE).astype(BF16)
    w_v = w_ukv[0, :, :, QK_NOPE:].reshape(KV_LORA, MLA_HEADS * V_DIM).astype(BF16)
    w_o = w_out[0].astype(BF16)
    wr = jnp.zeros((d, LANES), F32).at[:, 0:N_EXPERTS].set(w_router[0])
    wr_hi = wr.astype(BF16)
    wr2 = jnp.concatenate([wr_hi, (wr - wr_hi.astype(F32)).astype(BF16)], axis=-1)

    cos64, sin64 = _rope_tables(s)
    cos2 = jnp.asarray(np.concatenate([cos64, cos64], axis=-1))
    sin2 = jnp.asarray(np.concatenate([sin64, sin64], axis=-1))
    ktab = jnp.asarray(np.concatenate([cos64, sin64], axis=-1))

    n1 = norm1_g[0][None, :]
    qg, kvg = q_norm_g[0][None, :], kv_norm_g[0][None, :]
    u, bg, cqn, ckvn, kr2 = _inproj(x, sc1, sh1, n1, w_main, w_kr2, qg, kvg, with_conv=True, tm=ROW_TM)
    _, ckvn_c, kr2_c = _inproj(ctx, scc1, shc1, n1, w_main, w_kr2, qg, kvg, with_conv=False, tm=lc)

    q = _qproj(cqn, cos2, sin2, w_qn, w_qr, w_qrs, tm=QKV_TM)
    k, v = _kvproj(ckvn, kr2, ktab, w_kn, w_v, tm=QKV_TM)
    kc, vc = _kvproj(ckvn_c, kr2_c, None, w_kn, w_v, tm=lc)
    y_attn = _attention(q, k, v, kc, vc, tq=ATT_TQ, tk=ATT_TK)

    x1, tab, aff_t = _outproj(u, bg, conv_w[0], y_attn, w_o, x, g1, norm2_g[0][None, :], sc2, sh2, wr2, tm=ROW_TM)

    pos, offs = _routing(aff_t, cap=cap, blk=CMB_TT)
    nk1 = s // CMB_TT + 1
    offs_flat = jnp.swapaxes(offs[:, :, 0:nk1], 1, 2).reshape(-1)
    idx_t = _slots(offs_flat, pos, cap=cap, blk=CMB_TT, w=CMB_W)
    idx_flat = jnp.transpose(idx_t, (2, 0, 1)).reshape(-1)
    y = _expert_ffn(idx_flat, tab.reshape(b * s, d + LANES), w_gate[0], w_up[0], w_down[0], t=FFN_T)
    return _combine(offs_flat, pos, y, x1, g2, final_g[None, :], tt=CMB_TT, w=CMB_W)
```

```python
import functools
import math

import jax
import jax.numpy as jnp
import numpy as np
from jax import lax
from jax.experimental import pallas as pl
from jax.experimental.pallas import tpu as pltpu

F32 = jnp.float32
BF16 = jnp.bfloat16
I32 = jnp.int32

GRID_W = 64
CONV_WIDTH = 1024
MLA_HEADS = 8
QK_NOPE = 128
QK_ROPE = 64
V_DIM = 128
Q_LORA = 512
KV_LORA = 512
QK_DIM = QK_NOPE + QK_ROPE
MLA_WIDTH = MLA_HEADS * V_DIM
V_EXT = 2 * V_DIM
N_EXPERTS = 16
EC_FACTOR = 2
ROPE_THETA = 10000.0
ROPE_PAIRS = QK_ROPE // 4
ATTN_SCALE = 1.0 / math.sqrt(QK_DIM)
LOG2E = math.log2(math.e)
EPS = 1e-6

LANES = 128
VMEM_LIMIT = 56 * 1024 * 1024
FFN_VMEM_LIMIT = 60 * 1024 * 1024

MOD_TN = 2048
ROW_TM = 512
QKV_TM = 1024
ATT_TQ = 1024
ATT_TK = 512
FFN_T = 256
FFN_HEAD_ROWS = 96
CMB_TT = 256
CMB_W = 64


def _dot(a, b):
    return jnp.dot(a, b, preferred_element_type=F32)


def _params(sem, vmem=VMEM_LIMIT):
    return pltpu.CompilerParams(dimension_semantics=sem, vmem_limit_bytes=vmem)


def _resident(shape):
    nd = len(shape)
    return pl.BlockSpec(shape, lambda *_: (0,) * nd, pipeline_mode=pl.Buffered(1))


def _mod_kernel(c_ref, w_ref, b_ref, o_ref):
    cv = c_ref[...]
    a = cv / (1.0 + jnp.exp(-cv))
    a_hi = a.astype(BF16)
    a_lo = (a - a_hi.astype(F32)).astype(BF16)
    w = w_ref[...].astype(BF16)
    o_ref[...] = _dot(a_hi, w) + _dot(a_lo, w) + b_ref[...]


def _modulation(cvec, w_mod, b_mod):
    d, n = w_mod.shape
    return pl.pallas_call(
        _mod_kernel,
        grid=(n // MOD_TN,),
        in_specs=[pl.BlockSpec((8, d), lambda j: (0, 0)),
                  pl.BlockSpec((d, MOD_TN), lambda j: (0, j)),
                  pl.BlockSpec((1, MOD_TN), lambda j: (0, j))],
        out_specs=pl.BlockSpec((8, MOD_TN), lambda j: (0, j)),
        out_shape=jax.ShapeDtypeStruct((8, n), F32),
        compiler_params=_params(("arbitrary",)),
        name="modulation",
    )(cvec, w_mod, b_mod)


def _rms(v):
    return v * lax.rsqrt(jnp.mean(v * v, axis=-1, keepdims=True) + EPS)


def _inproj_kernel(x_ref, sc_ref, sh_ref, g_ref, w_ref, wkr_ref, qg_ref, kvg_ref, *outs, with_conv):
    h = _rms(x_ref[0]) * g_ref[...]
    h = h * (1.0 + sc_ref[0]) + sh_ref[0]
    hb = h.astype(BF16)
    c = CONV_WIDTH
    if with_conv:
        u_ref, bg_ref, cq_ref, ckv_ref, kr_ref = outs
        xin = _dot(hb, w_ref[:, 0:c])
        cg = _dot(hb, w_ref[:, 2 * c:3 * c])
        u_ref[0] = (cg * xin).astype(BF16)
        bg_ref[0] = _dot(hb, w_ref[:, c:2 * c]).astype(BF16)
    else:
        cq_ref, ckv_ref, kr_ref = outs
    o = 3 * c
    cq = _dot(hb, w_ref[:, o:o + Q_LORA])
    cq_ref[0] = (_rms(cq) * qg_ref[...]).astype(BF16)
    ckv = _dot(hb, w_ref[:, o + Q_LORA:o + Q_LORA + KV_LORA])
    ckv_ref[0] = (_rms(ckv) * kvg_ref[...]).astype(BF16)
    kr_ref[0] = _dot(hb, wkr_ref[...])


def _inproj(x, scale, shift, gain, w_main, w_kr2, q_g, kv_g, *, with_conv, tm):
    b, s, d = x.shape
    per_batch = scale.shape[0] > 1
    vec = pl.BlockSpec((1, 1, d), (lambda bi, i: (bi, 0, 0)) if per_batch else (lambda bi, i: (0, 0, 0)))
    row = lambda n: pl.BlockSpec((1, tm, n), lambda bi, i: (bi, i, 0))
    outs, specs = [], []
    if with_conv:
        outs += [jax.ShapeDtypeStruct((b, s, CONV_WIDTH), BF16)] * 2
        specs += [row(CONV_WIDTH)] * 2
    outs += [jax.ShapeDtypeStruct((b, s, Q_LORA), BF16), jax.ShapeDtypeStruct((b, s, KV_LORA), BF16),
             jax.ShapeDtypeStruct((b, s, 2 * QK_ROPE), F32)]
    specs += [row(Q_LORA), row(KV_LORA), row(2 * QK_ROPE)]
    return pl.pallas_call(
        functools.partial(_inproj_kernel, with_conv=with_conv),
        grid=(b, s // tm),
        in_specs=[row(d), vec, vec, _resident((1, d)), _resident(w_main.shape), _resident(w_kr2.shape),
                  _resident((1, Q_LORA)), _resident((1, KV_LORA))],
        out_specs=specs,
        out_shape=outs,
        compiler_params=_params(("arbitrary", "arbitrary")),
        name="inproj_conv" if with_conv else "inproj_ctx",
    )(x, scale, shift, gain, w_main, w_kr2, q_g, kv_g)


def _qproj_kernel(cq_ref, cos_ref, sin_ref, wn_ref, wr_ref, wrs_ref, q_ref):
    cq = cq_ref[0]
    qn = _dot(cq, wn_ref[...])
    qr = _dot(cq, wr_ref[...])
    qrs = _dot(cq, wrs_ref[...])
    reps = MLA_HEADS * QK_ROPE // LANES
    cos = jnp.concatenate([cos_ref[...]] * reps, axis=-1)
    sin = jnp.concatenate([sin_ref[...]] * reps, axis=-1)
    qrot = qr * cos + qrs * sin
    sc = ATTN_SCALE * LOG2E
    for h in range(MLA_HEADS):
        q_ref[0, h, :, 0:QK_NOPE] = (qn[:, h * QK_NOPE:(h + 1) * QK_NOPE] * sc).astype(BF16)
        q_ref[0, h, :, QK_NOPE:QK_DIM] = (qrot[:, h * QK_ROPE:(h + 1) * QK_ROPE] * sc).astype(BF16)


def _qproj(cqn, cos2, sin2, w_qn, w_qr, w_qrs, *, tm):
    b, s, r = cqn.shape
    return pl.pallas_call(
        _qproj_kernel,
        grid=(b, s // tm),
        in_specs=[pl.BlockSpec((1, tm, r), lambda bi, i: (bi, i, 0)),
                  pl.BlockSpec((tm, LANES), lambda bi, i: (i, 0)),
                  pl.BlockSpec((tm, LANES), lambda bi, i: (i, 0)),
                  _resident(w_qn.shape), _resident(w_qr.shape), _resident(w_qrs.shape)],
        out_specs=pl.BlockSpec((1, MLA_HEADS, tm, QK_DIM), lambda bi, i: (bi, 0, i, 0)),
        out_shape=jax.ShapeDtypeStruct((b, MLA_HEADS, s, QK_DIM), BF16),
        compiler_params=_params(("arbitrary", "arbitrary")),
        name="qproj",
    )(cqn, cos2, sin2, w_qn, w_qr, w_qrs)


def _kvproj_kernel(ckv_ref, kr_ref, *rest, rope):
    if rope:
        tab_ref, wk_ref, wv_ref, k_ref, v_ref = rest
        prod = kr_ref[0] * tab_ref[...]
        krot = prod[:, 0:QK_ROPE] + prod[:, QK_ROPE:2 * QK_ROPE]
    else:
        wk_ref, wv_ref, k_ref, v_ref = rest
        krot = kr_ref[0][:, 0:QK_ROPE]
    ckv = ckv_ref[0]
    kn = _dot(ckv, wk_ref[...])
    vv = _dot(ckv, wv_ref[...])
    krot = krot.astype(BF16)
    for h in range(MLA_HEADS):
        k_ref[0, h, :, 0:QK_NOPE] = kn[:, h * QK_NOPE:(h + 1) * QK_NOPE].astype(BF16)
        k_ref[0, h, :, QK_NOPE:QK_DIM] = krot
        v_ref[0, h, :, 0:V_DIM] = vv[:, h * V_DIM:(h + 1) * V_DIM].astype(BF16)
        v_ref[0, h, :, V_DIM:V_EXT] = jnp.ones((vv.shape[0], V_EXT - V_DIM), BF16)


def _kvproj(ckvn, kr2, tab, w_kn, w_v, *, tm):
    b, s, r = ckvn.shape
    rope = tab is not None
    ins = [ckvn, kr2]
    specs = [pl.BlockSpec((1, tm, r), lambda bi, i: (bi, i, 0)),
             pl.BlockSpec((1, tm, 2 * QK_ROPE), lambda bi, i: (bi, i, 0))]
    if rope:
        ins.append(tab)
        specs.append(pl.BlockSpec((tm, 2 * QK_ROPE), lambda bi, i: (i, 0)))
    ins += [w_kn, w_v]
    specs += [_resident(w_kn.shape), _resident(w_v.shape)]
    return pl.pallas_call(
        functools.partial(_kvproj_kernel, rope=rope),
        grid=(b, s // tm),
        in_specs=specs,
        out_specs=[pl.BlockSpec((1, MLA_HEADS, tm, QK_DIM), lambda bi, i: (bi, 0, i, 0)),
                   pl.BlockSpec((1, MLA_HEADS, tm, V_EXT), lambda bi, i: (bi, 0, i, 0))],
        out_shape=[jax.ShapeDtypeStruct((b, MLA_HEADS, s, QK_DIM), BF16),
                   jax.ShapeDtypeStruct((b, MLA_HEADS, s, V_EXT), BF16)],
        compiler_params=_params(("arbitrary", "arbitrary")),
        name="kvproj_rope" if rope else "kvproj_ctx",
    )(*ins)


def _attn_kernel(q_ref, k_ref, v_ref, kc_ref, vc_ref, o_ref, m_ref, acc_ref, sa_ref, sb_ref, *, tk):
    q = q_ref[0, 0]
    m_ref[...] = jnp.full(m_ref.shape, -jnp.inf, F32)
    acc_ref[...] = jnp.zeros(acc_ref.shape, F32)

    def lanes(a, n):
        return jnp.concatenate([a] * (n // LANES), axis=1)

    def scores(k):
        return lax.dot_general(q, k, (((1,), (1,)), ((), ())), preferred_element_type=F32)

    def update(s, v):
        m_old = m_ref[...]
        m_new = jnp.maximum(m_old, jnp.max(s, axis=-1, keepdims=True))
        alpha = jnp.exp2(m_old - m_new)
        p = jnp.exp2(s - lanes(m_new, s.shape[1]))
        acc_ref[...] = lanes(alpha, acc_ref.shape[1]) * acc_ref[...] + _dot(p.astype(BF16), v)
        m_ref[...] = m_new

    def chunk(ref, j):
        return ref[0, 0, pl.ds(pl.multiple_of(j * tk, tk), tk), :]

    n = k_ref.shape[2] // tk
    bufs = (sa_ref, sb_ref)
    sa_ref[...] = scores(chunk(k_ref, 0))
    for j in range(n):
        nxt = bufs[(j + 1) % 2]
        if j + 1 < n:
            nxt[...] = scores(chunk(k_ref, j + 1))
        else:
            s_ctx = scores(kc_ref[0, 0])
        update(bufs[j % 2][...], chunk(v_ref, j))
    update(s_ctx, vc_ref[0, 0])
    o_ref[0] = (acc_ref[:, 0:V_DIM] / acc_ref[:, V_DIM:V_EXT]).astype(BF16)


def _attention(q, k, v, kc, vc, *, tq, tk):
    b, h, s, dq = q.shape
    lc = kc.shape[2]
    return pl.pallas_call(
        functools.partial(_attn_kernel, tk=tk),
        grid=(b, h, s // tq),
        in_specs=[pl.BlockSpec((1, 1, tq, dq), lambda bi, hi, i: (bi, hi, i, 0)),
                  pl.BlockSpec((1, 1, s, dq), lambda bi, hi, i: (bi, hi, 0, 0)),
                  pl.BlockSpec((1, 1, s, V_EXT), lambda bi, hi, i: (bi, hi, 0, 0)),
                  pl.BlockSpec((1, 1, lc, dq), lambda bi, hi, i: (bi, hi, 0, 0)),
                  pl.BlockSpec((1, 1, lc, V_EXT), lambda bi, hi, i: (bi, hi, 0, 0))],
        out_specs=pl.BlockSpec((1, tq, V_DIM), lambda bi, hi, i: (bi, i, hi)),
        out_shape=jax.ShapeDtypeStruct((b, s, h * V_DIM), BF16),
        scratch_shapes=[pltpu.VMEM((tq, LANES), F32), pltpu.VMEM((tq, V_EXT), F32),
                        pltpu.VMEM((tq, tk), F32), pltpu.VMEM((tq, tk), F32)],
        compiler_params=_params(("arbitrary", "arbitrary", "arbitrary")),
        name="attention",
    )(q, k, v, kc, vc)


HALO = 16


def _outproj_kernel(u_ref, up_ref, un_ref, bg_ref, cw_ref, ya_ref, wo_ref, x_ref, g1_ref, n2_ref, sc_ref, sh_ref,
                    wr2_ref, x1_ref, tab_ref, afft_ref, *, tm, d):
    i = pl.program_id(1)
    last = pl.num_programs(1) - 1
    u = u_ref[0].astype(F32)
    rows = lax.broadcasted_iota(I32, (tm, 1), 0)
    prev_row = jnp.where(i > 0, up_ref[0][HALO - 1:HALO, :].astype(F32), 0.0)
    next_row = jnp.where(i < last, un_ref[0][0:1, :].astype(F32), 0.0)
    um1 = jnp.where(rows == 0, prev_row, pltpu.roll(u, 1, 0))
    up1 = jnp.where(rows == tm - 1, next_row, pltpu.roll(u, tm - 1, 0))
    cw = cw_ref[...]
    yc = (bg_ref[0].astype(F32) * (um1 * cw[0:1] + u * cw[1:2] + up1 * cw[2:3])).astype(BF16)
    nsub = 2
    tr = tm // nsub
    for r in range(nsub):
        rs = slice(r * tr, (r + 1) * tr)
        y = _dot(yc[rs], wo_ref[0:CONV_WIDTH, :]) + _dot(ya_ref[0, rs, :], wo_ref[CONV_WIDTH:, :])
        x1 = x_ref[0, rs, :] + g1_ref[0] * y
        x1_ref[0, rs, :] = x1
        h2 = _rms(x1) * n2_ref[...]
        h2 = h2 * (1.0 + sc_ref[0]) + sh_ref[0]
        tab_ref[0, rs, 0:d] = h2
        h_hi = h2.astype(BF16)
        h_lo = (h2 - h_hi.astype(F32)).astype(BF16)
        l2 = _dot(h_hi, wr2_ref[...])
        logits = l2[:, 0:LANES] + l2[:, LANES:2 * LANES] + _dot(h_lo, wr2_ref[:, 0:LANES])
        lane = lax.broadcasted_iota(I32, logits.shape, 1)
        logits = jnp.where(lane < N_EXPERTS, logits, -jnp.inf)
        e = jnp.exp(logits - jnp.max(logits, axis=-1, keepdims=True))
        aff = e / jnp.sum(e, axis=-1, keepdims=True)
        tab_ref[0, rs, d:d + LANES] = aff
        afft_ref[0, :, rs] = aff.T[0:N_EXPERTS, :]


def _outproj(u, bg, conv_w, y_attn, w_out, x, g1, n2, sc2, sh2, wr2, *, tm):
    b, s, d = x.shape
    nh = tm // HALO
    nhb = s // HALO
    row = lambda n: pl.BlockSpec((1, tm, n), lambda bi, i: (bi, i, 0))
    vec = pl.BlockSpec((1, 1, d), lambda bi, i: (bi, 0, 0))
    return pl.pallas_call(
        functools.partial(_outproj_kernel, tm=tm, d=d),
        grid=(b, s // tm),
        in_specs=[row(CONV_WIDTH),
                  pl.BlockSpec((1, HALO, CONV_WIDTH), lambda bi, i: (bi, jnp.maximum(i * nh - 1, 0), 0)),
                  pl.BlockSpec((1, HALO, CONV_WIDTH), lambda bi, i: (bi, jnp.minimum((i + 1) * nh, nhb - 1), 0)),
                  row(CONV_WIDTH), _resident(conv_w.shape), row(MLA_WIDTH), _resident(w_out.shape), row(d),
                  vec, _resident((1, d)), vec, vec, _resident(wr2.shape)],
        out_specs=[row(d), row(d + LANES), pl.BlockSpec((1, N_EXPERTS, tm), lambda bi, i: (bi, 0, i))],
        out_shape=[jax.ShapeDtypeStruct((b, s, d), F32), jax.ShapeDtypeStruct((b, s, d + LANES), F32),
                   jax.ShapeDtypeStruct((b, N_EXPERTS, s), F32)],
        compiler_params=_params(("arbitrary", "arbitrary")),
        name="outproj_router",
    )(u, u, u, bg, conv_w, y_attn, w_out, x, g1, n2, sc2, sh2, wr2)


def _routing_kernel(aff_ref, tri_ref, pos_ref, offs_ref, *, cap, blk, rows_per_expert):
    bi = pl.program_id(0)
    aff = aff_ref[0]
    ne, s = aff.shape
    capf = jnp.float32(cap)

    def bisect(t, prefix):
        cand = prefix | jnp.left_shift(jnp.int32(1), 30 - t)
        cnt = jnp.sum(jnp.where(aff >= lax.bitcast_convert_type(cand, F32), 1.0, 0.0), axis=1, keepdims=True)
        return jnp.where(cnt >= capf, cand, prefix)

    floor_bits = lax.fori_loop(0, 31, bisect, jnp.zeros((ne, 1), I32))
    thr = jnp.min(jnp.where(aff >= lax.bitcast_convert_type(floor_bits, F32), aff, jnp.inf), axis=1, keepdims=True)
    gt = aff > thr
    eq = aff == thr
    need = capf - jnp.sum(jnp.where(gt, 1.0, 0.0), axis=1, keepdims=True)
    tri = tri_ref[...]

    def cumsum_blocks(mask_f32):
        run = jnp.zeros((ne, 1), F32)
        parts, starts = [], []
        for kb in range(s // blk):
            c = _dot(mask_f32[:, kb * blk:(kb + 1) * blk].astype(BF16), tri)
            starts.append(run)
            parts.append(c + run)
            run = run + c[:, blk - 1:blk]
        return jnp.concatenate(parts, axis=1), starts, run

    eqf = jnp.where(eq, 1.0, 0.0)
    eq_incl, _, _ = cumsum_blocks(eqf)
    sel = jnp.where(gt, 1.0, jnp.where(eq & (eq_incl - eqf < need), 1.0, 0.0))
    incl, starts, _ = cumsum_blocks(sel)

    base = (lax.broadcasted_iota(I32, (ne, 1), 0) * rows_per_expert + bi * cap)
    pos_ref[0] = jnp.where(sel > 0.0, (incl - sel).astype(I32) + base, -1)
    lane = lax.broadcasted_iota(I32, (ne, LANES), 1)
    offs = jnp.full((ne, LANES), cap, I32) + base
    for kb, st in enumerate(starts):
        offs = jnp.where(lane == kb, st.astype(I32) + base, offs)
    offs_ref[0] = offs


def _routing(aff_t, *, cap, blk):
    b, ne, s = aff_t.shape
    assert s // blk + 1 <= LANES
    tri = jnp.asarray(np.triu(np.ones((blk, blk), np.float32)), BF16)
    return pl.pallas_call(
        functools.partial(_routing_kernel, cap=cap, blk=blk, rows_per_expert=b * cap),
        grid=(b,),
        in_specs=[pl.BlockSpec((1, ne, s), lambda bi: (bi, 0, 0)), _resident(tri.shape)],
        out_specs=[pl.BlockSpec((1, ne, s), lambda bi: (bi, 0, 0)),
                   pl.BlockSpec((1, ne, LANES), lambda bi: (bi, 0, 0))],
        out_shape=[jax.ShapeDtypeStruct((b, ne, s), I32), jax.ShapeDtypeStruct((b, ne, LANES), I32)],
        compiler_params=_params(("arbitrary",)),
        name="routing",
    )(aff_t, tri)


def _slots_kernel(offs_ref, pos_ref, idx_ref, *, cap, blk, w, rows_per_expert):
    bi = pl.program_id(0)
    ne, s = pos_ref.shape[1], pos_ref.shape[2]
    nb = s // blk
    idx_ref[...] = jnp.zeros(idx_ref.shape, I32)
    sub = lax.broadcasted_iota(I32, (w, blk), 0)
    tok = lax.broadcasted_iota(I32, (w, blk), 1)

    def block(kb, carry):
        tok1 = tok + (kb * blk + bi * s + 1)
        windows, trips = [], []
        for e in range(ne):
            base = e * rows_per_expert + bi * cap
            o = (bi * (nb + 1) + kb) * ne + e
            lo8 = (offs_ref[o] - base) // 8 * 8
            hi = offs_ref[o + ne] - base
            posrow = pos_ref[0, e:e + 1, pl.ds(pl.multiple_of(kb * blk, blk), blk)] - base

            def window(r, c, e=e, lo8=lo8, posrow=posrow):
                start = pl.multiple_of(jnp.minimum(lo8 + r * w, cap - w), 8)
                hit = jnp.where(sub + start == posrow, tok1, 0)
                val = jnp.sum(hit.astype(F32), axis=1, keepdims=True).astype(I32)
                old = idx_ref[0, pl.ds(start, w), e:e + 1]
                idx_ref[0, pl.ds(start, w), e:e + 1] = jnp.where(val > 0, val - 1, old)
                return c

            windows.append(window)
            trips.append((hi - lo8 + w - 1) // w)
        for window in windows:
            window(0, 0)
        for window, n in zip(windows, trips):
            lax.fori_loop(1, n, window, 0)
        return carry

    lax.fori_loop(0, nb, block, 0)


def _slots(offs_flat, pos, *, cap, blk, w):
    b, ne, s = pos.shape
    return pl.pallas_call(
        functools.partial(_slots_kernel, cap=cap, blk=blk, w=w, rows_per_expert=b * cap),
        grid_spec=pltpu.PrefetchScalarGridSpec(
            num_scalar_prefetch=1,
            grid=(b,),
            in_specs=[pl.BlockSpec((1, ne, s), lambda bi, o: (bi, 0, 0))],
            out_specs=pl.BlockSpec((1, cap, ne), lambda bi, o: (bi, 0, 0))),
        out_shape=jax.ShapeDtypeStruct((b, cap, ne), I32),
        compiler_params=_params(("arbitrary",)),
        name="slots",
    )(offs_flat, pos)


def _ffn_kernel(idx_ref, tab_hbm, wg_hbm, wu_hbm, wd_hbm, y_ref, buf_a, buf_b, buf_c, wg_ref, wu_ref, wd_ref,
                sg_ref, su_ref, sd_ref, sem, wsem, *, t, d):
    e = pl.program_id(0)
    j = pl.program_id(1)
    ne, nt = pl.num_programs(0), pl.num_programs(1)
    step = e * nt + j
    nsteps = ne * nt
    rc, fc = sg_ref.shape[0], sd_ref.shape[0]

    def row_copy(base, i, buf, sl):
        return pltpu.make_async_copy(tab_hbm.at[pl.ds(idx_ref[base + i], 1)], buf.at[pl.ds(i, 1)], sem.at[sl])

    def wait_rows(buf, sl):
        pltpu.make_async_copy(tab_hbm.at[pl.ds(0, t)], buf, sem.at[sl]).wait()

    def piece_copies(ex, pc):
        return [pltpu.make_async_copy(wg_hbm.at[ex, pl.ds(pl.multiple_of(pc * rc, rc), rc)], sg_ref, wsem.at[0]),
                pltpu.make_async_copy(wu_hbm.at[ex, pl.ds(pl.multiple_of(pc * rc, rc), rc)], su_ref, wsem.at[1]),
                pltpu.make_async_copy(wd_hbm.at[ex, pl.ds(pl.multiple_of(pc * fc, fc), fc)], sd_ref, wsem.at[2])]

    def cast_piece(st, pc):
        wg_ref[st, pl.ds(pl.multiple_of(pc * rc, rc), rc), :] = sg_ref[...].astype(BF16)
        wu_ref[st, pl.ds(pl.multiple_of(pc * rc, rc), rc), :] = su_ref[...].astype(BF16)
        wd_ref[st, pl.ds(pl.multiple_of(pc * fc, fc), fc), :] = sd_ref[...].astype(BF16)

    @pl.when(step == 0)
    def _():
        def issue(i, carry):
            row_copy(0, i, buf_a, 0).start()
            row_copy(jnp.minimum(1, nsteps - 1) * t, i, buf_b, 1).start()
            return carry
        lax.fori_loop(0, t, issue, 0)

        def load(pc, carry):
            cps = piece_copies(0, pc)
            for cp in cps:
                cp.start()
            for cp in cps:
                cp.wait()
            cast_piece(0, pc)
            return carry
        lax.fori_loop(0, nt - 1, load, 0)
        for cp in piece_copies(0, nt - 1):
            cp.start()

    pe = jnp.where(j > 0, e, e - 1)
    pj = jnp.where(j > 0, j - 1, nt - 1)
    nxt_e = jnp.minimum(e + 1, ne - 1)
    ws = e % 2

    def run(cur, cur_sl, mid, mid_sl, far, far_sl):
        wait_rows(cur, cur_sl)
        for cp in piece_copies(jnp.minimum(pe + 1, ne - 1), pj):
            cp.wait()
        cast_piece((pe + 1) % 2, pj)
        for cp in piece_copies(nxt_e, j):
            cp.start()
        base = jnp.minimum(step + 2, nsteps - 1) * t
        head_rows = FFN_HEAD_ROWS * t // FFN_T
        for i in range(head_rows):
            row_copy(base, i, far, far_sl).start()

        @pl.when(idx_ref[base] < 0)
        def _():
            y_ref[0:16, 0:LANES] = jnp.zeros((16, LANES), BF16)

        for i in range(head_rows, t):
            row_copy(base, i, far, far_sl).start()
        xb = cur[:, 0:d].astype(BF16)
        aff = cur[:, d:d + LANES]
        lane = lax.broadcasted_iota(I32, aff.shape, 1)
        gate_w = jnp.sum(jnp.where(lane == e, aff, 0.0), axis=-1, keepdims=True)
        g = _dot(xb, wg_ref[ws])
        up = _dot(xb, wu_ref[ws])
        hid = ((g / (1.0 + jnp.exp(-g))) * up).astype(BF16)
        y = _dot(hid, wd_ref[ws])
        y_ref[...] = (y * gate_w).astype(BF16)

        @pl.when(step == nsteps - 1)
        def _():
            wait_rows(mid, mid_sl)
            wait_rows(far, far_sl)
            for cp in piece_copies(nxt_e, j):
                cp.wait()

    bufs = (buf_a, buf_b, buf_c)
    for r in range(3):
        @pl.when(step % 3 == r)
        def _(r=r):
            run(bufs[r], r, bufs[(r + 1) % 3], (r + 1) % 3, bufs[(r + 2) % 3], (r + 2) % 3)


def _expert_ffn(idx_flat, tab, w_gate, w_up, w_down, *, t):
    ne, d, ff = w_gate.shape
    rows = idx_flat.shape[0]
    nt = rows // ne // t
    dw = tab.shape[1]
    bf16_rows = 16
    assert d % (nt * bf16_rows) == 0 and ff % (nt * bf16_rows) == 0
    return pl.pallas_call(
        functools.partial(_ffn_kernel, t=t, d=d),
        grid_spec=pltpu.PrefetchScalarGridSpec(
            num_scalar_prefetch=1,
            grid=(ne, nt),
            in_specs=[pl.BlockSpec(memory_space=pl.ANY)] * 4,
            out_specs=pl.BlockSpec((t, d), lambda e, j, idx: (e * nt + j, 0)),
            scratch_shapes=[pltpu.VMEM((t, dw), F32), pltpu.VMEM((t, dw), F32), pltpu.VMEM((t, dw), F32),
                            pltpu.VMEM((2, d, ff), BF16), pltpu.VMEM((2, d, ff), BF16), pltpu.VMEM((2, ff, d), BF16),
                            pltpu.VMEM((d // nt, ff), F32), pltpu.VMEM((d // nt, ff), F32),
                            pltpu.VMEM((ff // nt, d), F32),
                            pltpu.SemaphoreType.DMA((3,)), pltpu.SemaphoreType.DMA((3,))]),
        out_shape=jax.ShapeDtypeStruct((rows, d), BF16),
        compiler_params=_params(("arbitrary", "arbitrary"), vmem=FFN_VMEM_LIMIT),
        name="expert_ffn",
    )(idx_flat, tab, w_gate, w_up, w_down)


def _combine_kernel(offs_ref, pos_ref, y_hbm, x1_ref, g2_ref, fg_ref, o_ref, ybuf, acc_ref, sem, *, tt, w, total_rows):
    ne = pos_ref.shape[1]
    nk = pl.num_programs(1)
    step = pl.program_id(0) * nk + pl.program_id(1)
    nsteps = pl.num_programs(0) * nk
    slot = step % 2

    def bounds(st):
        o0 = (st // nk * (nk + 1) + st % nk) * ne
        lo8 = [(offs_ref[o0 + e] // 8) * 8 for e in range(ne)]
        hi = [offs_ref[o0 + ne + e] for e in range(ne)]
        return lo8, hi

    def window_copies(lo8, r, sl):
        starts = [pl.multiple_of(jnp.minimum(lo8[e] + r * w, total_rows - w), 8) for e in range(ne)]
        copies = [pltpu.make_async_copy(y_hbm.at[pl.ds(starts[e], w)], ybuf.at[sl, pl.ds(e * w, w)], sem.at[sl, e])
                  for e in range(ne)]
        return starts, copies

    @pl.when(step == 0)
    def _():
        for cp in window_copies(bounds(0)[0], 0, 0)[1]:
            cp.start()

    lo8, hi = bounds(step)
    starts0, copies0 = window_copies(lo8, 0, slot)
    for cp in copies0:
        cp.wait()

    @pl.when(step + 1 < nsteps)
    def _():
        for cp in window_copies(bounds(step + 1)[0], 0, 1 - slot)[1]:
            cp.start()

    pos = pos_ref[0]

    d = x1_ref.shape[2]
    halves = (slice(0, d // 2), slice(d // 2, d))

    def onehot(r, starts):
        sub = lax.broadcasted_iota(I32, (w, tt), 0)
        want = [jnp.where(pos[e:e + 1, :] >= lo8[e] + r * w, pos[e:e + 1, :], -1) for e in range(ne)]
        blocks = [jnp.where(sub + starts[e] == want[e], 1.0, 0.0).astype(BF16) for e in range(ne)]
        return jnp.concatenate(blocks, axis=0)

    def contribution(oh, cols):
        return lax.dot_general(oh, ybuf[slot, :, cols], (((0,), (0,)), ((), ())), preferred_element_type=F32)

    def finish(parts):
        x2 = [x1_ref[0, :, cols] + g2_ref[0, :, cols] * part for cols, part in zip(halves, parts)]
        ms = sum(jnp.sum(v * v, axis=-1, keepdims=True) for v in x2) / d
        scale = lax.rsqrt(ms + EPS)
        for cols, v in zip(halves, x2):
            o_ref[0, :, cols] = (v * scale) * fg_ref[:, cols]

    oh0 = onehot(0, starts0)
    parts = [contribution(oh0, cols) for cols in halves]
    finish(parts)

    rounds = jnp.int32(1)
    for e in range(ne):
        rounds = jnp.maximum(rounds, (hi[e] - lo8[e] + w - 1) // w)

    @pl.when(rounds > 1)
    def _():
        for cols, part in zip(halves, parts):
            acc_ref[:, cols] = part

        def extra_round(r, carry):
            starts, copies = window_copies(lo8, r, slot)
            for cp in copies:
                cp.start()
            for cp in copies:
                cp.wait()
            oh = onehot(r, starts)
            for cols in halves:
                acc_ref[:, cols] += contribution(oh, cols)
            return carry

        lax.fori_loop(1, rounds, extra_round, 0)
        finish([acc_ref[:, cols] for cols in halves])


def _combine(offs_flat, pos, y, x1, g2, final_g, *, tt, w):
    b, s, d = x1.shape
    ne = pos.shape[1]
    total_rows = y.shape[0]
    return pl.pallas_call(
        functools.partial(_combine_kernel, tt=tt, w=w, total_rows=total_rows),
        grid_spec=pltpu.PrefetchScalarGridSpec(
            num_scalar_prefetch=1,
            grid=(b, s // tt),
            in_specs=[pl.BlockSpec((1, ne, tt), lambda bi, k, o: (bi, 0, k)),
                      pl.BlockSpec(memory_space=pl.ANY),
                      pl.BlockSpec((1, tt, d), lambda bi, k, o: (bi, k, 0)),
                      pl.BlockSpec((1, 1, d), lambda bi, k, o: (bi, 0, 0)),
                      pl.BlockSpec((1, d), lambda bi, k, o: (0, 0))],
            out_specs=pl.BlockSpec((1, tt, d), lambda bi, k, o: (bi, k, 0)),
            scratch_shapes=[pltpu.VMEM((2, ne * w, d), BF16), pltpu.VMEM((tt, d), F32),
                            pltpu.SemaphoreType.DMA((2, ne))]),
        out_shape=jax.ShapeDtypeStruct((b, s, d), F32),
        compiler_params=_params(("arbitrary", "arbitrary")),
        name="combine_final",
    )(offs_flat, pos, y, x1, g2, final_g)


def _rope_tables(s):
    n_rows = s // GRID_W
    row = np.repeat(np.arange(n_rows), GRID_W).astype(np.float64)
    col = np.tile(np.arange(GRID_W), n_rows).astype(np.float64)
    inv_freq = ROPE_THETA ** (-np.arange(ROPE_PAIRS, dtype=np.float64) / ROPE_PAIRS)
    ang_r = row[:, None] * inv_freq[None, :]
    ang_c = col[:, None] * inv_freq[None, :]
    cos64 = np.concatenate([np.cos(ang_r)] * 2 + [np.cos(ang_c)] * 2, axis=-1).astype(np.float32)
    sin64 = np.concatenate([-np.sin(ang_r), np.sin(ang_r), -np.sin(ang_c), np.sin(ang_c)], axis=-1).astype(np.float32)
    return cos64, sin64


ROPE_SWAP = np.concatenate([np.arange(16, 32), np.arange(0, 16), np.arange(48, 64), np.arange(32, 48)])


def kernel(x, c, ctx, c_ctx, w_mod, b_mod, norm1_g, norm2_g, w_in, conv_w, q_norm_g, w_uq, kv_norm_g, w_ukv, w_out,
           w_router, w_gate, w_up, w_down, final_g):
    b, s, d = x.shape
    lc = ctx.shape[1]
    assert w_in.shape[0] == 1, "single-layer stack"
    assert b <= 7 and s % max(ROW_TM, QKV_TM, ATT_TQ, ATT_TK, CMB_TT) == 0
    cap = EC_FACTOR * s // N_EXPERTS
    assert (b * cap) % FFN_T == 0 and cap % CMB_W == 0

    cvec = jnp.zeros((8, d), F32).at[0:b].set(c).at[b].set(c_ctx)
    mod = _modulation(cvec, w_mod[0], b_mod[0][None, :])
    sh1, sc1, g1, sh2, sc2, g2 = [mod[0:b, None, i * d:(i + 1) * d] for i in range(6)]
    shc1, scc1 = mod[b:b + 1, None, 0:d], mod[b:b + 1, None, d:2 * d]

    n_main = 3 * CONV_WIDTH + Q_LORA + KV_LORA
    w_main = w_in[0, :, 0:n_main].astype(BF16)
    w_kr = w_in[0, :, n_main:]
    w_kr2 = jnp.concatenate([w_kr, w_kr[:, ROPE_SWAP]], axis=-1).astype(BF16)
    w_qn = w_uq[0, :, :, 0:QK_NOPE].reshape(Q_LORA, MLA_HEADS * QK_NOPE).astype(BF16)
    w_qr3 = w_uq[0, :, :, QK_NOPE:]
    w_qr = w_qr3.reshape(Q_LORA, MLA_HEADS * QK_ROPE).astype(BF16)
    w_qrs = w_qr3[:, :, ROPE_SWAP].reshape(Q_LORA, MLA_HEADS * QK_ROPE).astype(BF16)
    w_kn = w_ukv[0, :, :, 0:QK_NOPE].reshape(KV_LORA, MLA_HEADS * QK_NOPE).astype(BF16)
    w_v = w_ukv[0, :, :, QK_NOPE:].reshape(KV_LORA, MLA_HEADS * V_DIM).astype(BF16)
    w_o = w_out[0].astype(BF16)
    wr = jnp.zeros((d, LANES), F32).at[:, 0:N_EXPERTS].set(w_router[0])
    wr_hi = wr.astype(BF16)
    wr2 = jnp.concatenate([wr_hi, (wr - wr_hi.astype(F32)).astype(BF16)], axis=-1)

    cos64, sin64 = _rope_tables(s)
    cos2 = jnp.asarray(np.concatenate([cos64, cos64], axis=-1))
    sin2 = jnp.asarray(np.concatenate([sin64, sin64], axis=-1))
    ktab = jnp.asarray(np.concatenate([cos64, sin64], axis=-1))

    n1 = norm1_g[0][None, :]
    qg, kvg = q_norm_g[0][None, :], kv_norm_g[0][None, :]
    u, bg, cqn, ckvn, kr2 = _inproj(x, sc1, sh1, n1, w_main, w_kr2, qg, kvg, with_conv=True, tm=ROW_TM)
    _, ckvn_c, kr2_c = _inproj(ctx, scc1, shc1, n1, w_main, w_kr2, qg, kvg, with_conv=False, tm=lc)

    q = _qproj(cqn, cos2, sin2, w_qn, w_qr, w_qrs, tm=QKV_TM)
    k, v = _kvproj(ckvn, kr2, ktab, w_kn, w_v, tm=QKV_TM)
    kc, vc = _kvproj(ckvn_c, kr2_c, None, w_kn, w_v, tm=lc)
    y_attn = _attention(q, k, v, kc, vc, tq=ATT_TQ, tk=ATT_TK)

    x1, tab, aff_t = _outproj(u, bg, conv_w[0], y_attn, w_o, x, g1, norm2_g[0][None, :], sc2, sh2, wr2, tm=ROW_TM)

    pos, offs = _routing(aff_t, cap=cap, blk=CMB_TT)
    nk1 = s // CMB_TT + 1
    offs_flat = jnp.swapaxes(offs[:, :, 0:nk1], 1, 2).reshape(-1)
    idx_t = _slots(offs_flat, pos, cap=cap, blk=CMB_TT, w=CMB_W)
    idx_flat = jnp.transpose(idx_t, (2, 0, 1)).reshape(-1)
    y = _expert_ffn(idx_flat, tab.reshape(b * s, d + LANES), w_gate[0], w_up[0], w_down[0], t=FFN_T)
    return _combine(offs_flat, pos, y, x1, g2, final_g[None, :], tt=CMB_TT, w=CMB_W)
```

```python
import functools
import math

import jax
import jax.numpy as jnp
import numpy as np
from jax import lax
from jax.experimental import pallas as pl
from jax.experimental.pallas import tpu as pltpu

F32 = jnp.float32
BF16 = jnp.bfloat16
I32 = jnp.int32

GRID_W = 64
CONV_WIDTH = 1024
MLA_HEADS = 8
QK_NOPE = 128
QK_ROPE = 64
V_DIM = 128
Q_LORA = 512
KV_LORA = 512
QK_DIM = QK_NOPE + QK_ROPE
MLA_WIDTH = MLA_HEADS * V_DIM
V_EXT = 2 * V_DIM
N_EXPERTS = 16
EC_FACTOR = 2
ROPE_THETA = 10000.0
ROPE_PAIRS = QK_ROPE // 4
ATTN_SCALE = 1.0 / math.sqrt(QK_DIM)
LOG2E = math.log2(math.e)
EPS = 1e-6

LANES = 128
VMEM_LIMIT = 56 * 1024 * 1024
FFN_VMEM_LIMIT = 60 * 1024 * 1024

MOD_TK = 256
ROW_TM = 512
QKV_TM = 1024
ATT_TQ = 1024
ATT_TK = 512
FFN_T = 256
FFN_HEAD_ROWS = 96
CMB_TT = 256
CMB_W = 64


def _dot(a, b):
    return jnp.dot(a, b, preferred_element_type=F32)


def _params(sem, vmem=VMEM_LIMIT):
    return pltpu.CompilerParams(dimension_semantics=sem, vmem_limit_bytes=vmem)


def _resident(shape):
    nd = len(shape)
    return pl.BlockSpec(shape, lambda *_: (0,) * nd, pipeline_mode=pl.Buffered(1))


def _mod_kernel(c_ref, w_ref, b_ref, o_ref):
    k = pl.program_id(0)
    cv = c_ref[...]
    a = cv / (1.0 + jnp.exp(-cv))
    a_hi = a.astype(BF16)
    a_lo = (a - a_hi.astype(F32)).astype(BF16)
    w = w_ref[...].astype(BF16)
    p = _dot(jnp.concatenate([a_hi, a_lo], axis=0), w)
    part = p[0:8] + p[8:16]

    @pl.when(k == 0)
    def _():
        o_ref[...] = part + b_ref[...]

    @pl.when(k > 0)
    def _():
        o_ref[...] += part


def _modulation(cvec, w_mod, b_mod):
    d, n = w_mod.shape
    return pl.pallas_call(
        _mod_kernel,
        grid=(d // MOD_TK,),
        in_specs=[pl.BlockSpec((8, MOD_TK), lambda k: (0, k)),
                  pl.BlockSpec((MOD_TK, n), lambda k: (k, 0)),
                  pl.BlockSpec((1, n), lambda k: (0, 0))],
        out_specs=pl.BlockSpec((8, n), lambda k: (0, 0)),
        out_shape=jax.ShapeDtypeStruct((8, n), F32),
        compiler_params=_params(("arbitrary",)),
        name="modulation",
    )(cvec, w_mod, b_mod)


def _rms(v):
    return v * lax.rsqrt(jnp.mean(v * v, axis=-1, keepdims=True) + EPS)


def _inproj_kernel(x_ref, sc_ref, sh_ref, g_ref, w_ref, wkr_ref, qg_ref, kvg_ref, *outs, with_conv):
    h = _rms(x_ref[0]) * g_ref[...]
    h = h * (1.0 + sc_ref[0]) + sh_ref[0]
    hb = h.astype(BF16)
    c = CONV_WIDTH
    if with_conv:
        u_ref, bg_ref, cq_ref, ckv_ref, kr_ref = outs
        xin = _dot(hb, w_ref[:, 0:c])
        cg = _dot(hb, w_ref[:, 2 * c:3 * c])
        u_ref[0] = (cg * xin).astype(BF16)
        bg_ref[0] = _dot(hb, w_ref[:, c:2 * c]).astype(BF16)
    else:
        cq_ref, ckv_ref, kr_ref = outs
    o = 3 * c
    cq = _dot(hb, w_ref[:, o:o + Q_LORA])
    cq_ref[0] = (_rms(cq) * qg_ref[...]).astype(BF16)
    ckv = _dot(hb, w_ref[:, o + Q_LORA:o + Q_LORA + KV_LORA])
    ckv_ref[0] = (_rms(ckv) * kvg_ref[...]).astype(BF16)
    kr_ref[0] = _dot(hb, wkr_ref[...])


def _inproj(x, scale, shift, gain, w_main, w_kr2, q_g, kv_g, *, with_conv, tm):
    b, s, d = x.shape
    per_batch = scale.shape[0] > 1
    vec = pl.BlockSpec((1, 1, d), (lambda bi, i: (bi, 0, 0)) if per_batch else (lambda bi, i: (0, 0, 0)))
    row = lambda n: pl.BlockSpec((1, tm, n), lambda bi, i: (bi, i, 0))
    outs, specs = [], []
    if with_conv:
        outs += [jax.ShapeDtypeStruct((b, s, CONV_WIDTH), BF16)] * 2
        specs += [row(CONV_WIDTH)] * 2
    outs += [jax.ShapeDtypeStruct((b, s, Q_LORA), BF16), jax.ShapeDtypeStruct((b, s, KV_LORA), BF16),
             jax.ShapeDtypeStruct((b, s, 2 * QK_ROPE), F32)]
    specs += [row(Q_LORA), row(KV_LORA), row(2 * QK_ROPE)]
    return pl.pallas_call(
        functools.partial(_inproj_kernel, with_conv=with_conv),
        grid=(b, s // tm),
        in_specs=[row(d), vec, vec, _resident((1, d)), _resident(w_main.shape), _resident(w_kr2.shape),
                  _resident((1, Q_LORA)), _resident((1, KV_LORA))],
        out_specs=specs,
        out_shape=outs,
        compiler_params=_params(("arbitrary", "arbitrary")),
        name="inproj_conv" if with_conv else "inproj_ctx",
    )(x, scale, shift, gain, w_main, w_kr2, q_g, kv_g)


def _qproj_kernel(cq_ref, cos_ref, sin_ref, wn_ref, wr_ref, wrs_ref, q_ref):
    cq = cq_ref[0]
    qn = _dot(cq, wn_ref[...])
    qr = _dot(cq, wr_ref[...])
    qrs = _dot(cq, wrs_ref[...])
    reps = MLA_HEADS * QK_ROPE // LANES
    cos = jnp.concatenate([cos_ref[...]] * reps, axis=-1)
    sin = jnp.concatenate([sin_ref[...]] * reps, axis=-1)
    qrot = qr * cos + qrs * sin
    sc = ATTN_SCALE * LOG2E
    for h in range(MLA_HEADS):
        q_ref[0, h, :, 0:QK_NOPE] = (qn[:, h * QK_NOPE:(h + 1) * QK_NOPE] * sc).astype(BF16)
        q_ref[0, h, :, QK_NOPE:QK_DIM] = (qrot[:, h * QK_ROPE:(h + 1) * QK_ROPE] * sc).astype(BF16)


def _qproj(cqn, cos2, sin2, w_qn, w_qr, w_qrs, *, tm):
    b, s, r = cqn.shape
    return pl.pallas_call(
        _qproj_kernel,
        grid=(b, s // tm),
        in_specs=[pl.BlockSpec((1, tm, r), lambda bi, i: (bi, i, 0)),
                  pl.BlockSpec((tm, LANES), lambda bi, i: (i, 0)),
                  pl.BlockSpec((tm, LANES), lambda bi, i: (i, 0)),
                  _resident(w_qn.shape), _resident(w_qr.shape), _resident(w_qrs.shape)],
        out_specs=pl.BlockSpec((1, MLA_HEADS, tm, QK_DIM), lambda bi, i: (bi, 0, i, 0)),
        out_shape=jax.ShapeDtypeStruct((b, MLA_HEADS, s, QK_DIM), BF16),
        compiler_params=_params(("arbitrary", "arbitrary")),
        name="qproj",
    )(cqn, cos2, sin2, w_qn, w_qr, w_qrs)


def _kvproj_kernel(ckv_ref, kr_ref, *rest, rope):
    if rope:
        tab_ref, wk_ref, wv_ref, k_ref, v_ref = rest
        prod = kr_ref[0] * tab_ref[...]
        krot = prod[:, 0:QK_ROPE] + prod[:, QK_ROPE:2 * QK_ROPE]
    else:
        wk_ref, wv_ref, k_ref, v_ref = rest
        krot = kr_ref[0][:, 0:QK_ROPE]
    ckv = ckv_ref[0]
    kn = _dot(ckv, wk_ref[...])
    vv = _dot(ckv, wv_ref[...])
    krot = krot.astype(BF16)
    for h in range(MLA_HEADS):
        k_ref[0, h, :, 0:QK_NOPE] = kn[:, h * QK_NOPE:(h + 1) * QK_NOPE].astype(BF16)
        k_ref[0, h, :, QK_NOPE:QK_DIM] = krot
        v_ref[0, h, :, 0:V_DIM] = vv[:, h * V_DIM:(h + 1) * V_DIM].astype(BF16)
        v_ref[0, h, :, V_DIM:V_EXT] = jnp.ones((vv.shape[0], V_EXT - V_DIM), BF16)


def _kvproj(ckvn, kr2, tab, w_kn, w_v, *, tm):
    b, s, r = ckvn.shape
    rope = tab is not None
    ins = [ckvn, kr2]
    specs = [pl.BlockSpec((1, tm, r), lambda bi, i: (bi, i, 0)),
             pl.BlockSpec((1, tm, 2 * QK_ROPE), lambda bi, i: (bi, i, 0))]
    if rope:
        ins.append(tab)
        specs.append(pl.BlockSpec((tm, 2 * QK_ROPE), lambda bi, i: (i, 0)))
    ins += [w_kn, w_v]
    specs += [_resident(w_kn.shape), _resident(w_v.shape)]
    return pl.pallas_call(
        functools.partial(_kvproj_kernel, rope=rope),
        grid=(b, s // tm),
        in_specs=specs,
        out_specs=[pl.BlockSpec((1, MLA_HEADS, tm, QK_DIM), lambda bi, i: (bi, 0, i, 0)),
                   pl.BlockSpec((1, MLA_HEADS, tm, V_EXT), lambda bi, i: (bi, 0, i, 0))],
        out_shape=[jax.ShapeDtypeStruct((b, MLA_HEADS, s, QK_DIM), BF16),
                   jax.ShapeDtypeStruct((b, MLA_HEADS, s, V_EXT), BF16)],
        compiler_params=_params(("arbitrary", "arbitrary")),
        name="kvproj_rope" if rope else "kvproj_ctx",
    )(*ins)


def _attn_kernel(q_ref, k_ref, v_ref, kc_ref, vc_ref, o_ref, m_ref, acc_ref, sa_ref, sb_ref, *, tk):
    q = q_ref[0, 0]
    m_ref[...] = jnp.full(m_ref.shape, -jnp.inf, F32)
    acc_ref[...] = jnp.zeros(acc_ref.shape, F32)

    def lanes(a, n):
        return jnp.concatenate([a] * (n // LANES), axis=1)

    def scores(k):
        return lax.dot_general(q, k, (((1,), (1,)), ((), ())), preferred_element_type=F32)

    def update(s, v):
        m_old = m_ref[...]
        m_new = jnp.maximum(m_old, jnp.max(s, axis=-1, keepdims=True))
        alpha = jnp.exp2(m_old - m_new)
        p = jnp.exp2(s - lanes(m_new, s.shape[1]))
        acc_ref[...] = lanes(alpha, acc_ref.shape[1]) * acc_ref[...] + _dot(p.astype(BF16), v)
        m_ref[...] = m_new

    def chunk(ref, j):
        return ref[0, 0, pl.ds(pl.multiple_of(j * tk, tk), tk), :]

    n = k_ref.shape[2] // tk
    bufs = (sa_ref, sb_ref)
    sa_ref[...] = scores(chunk(k_ref, 0))
    for j in range(n):
        nxt = bufs[(j + 1) % 2]
        if j + 1 < n:
            nxt[...] = scores(chunk(k_ref, j + 1))
        else:
            s_ctx = scores(kc_ref[0, 0])
        update(bufs[j % 2][...], chunk(v_ref, j))
    update(s_ctx, vc_ref[0, 0])
    o_ref[0] = (acc_ref[:, 0:V_DIM] / acc_ref[:, V_DIM:V_EXT]).astype(BF16)


def _attention(q, k, v, kc, vc, *, tq, tk):
    b, h, s, dq = q.shape
    lc = kc.shape[2]
    return pl.pallas_call(
        functools.partial(_attn_kernel, tk=tk),
        grid=(b, h, s // tq),
        in_specs=[pl.BlockSpec((1, 1, tq, dq), lambda bi, hi, i: (bi, hi, i, 0)),
                  pl.BlockSpec((1, 1, s, dq), lambda bi, hi, i: (bi, hi, 0, 0)),
                  pl.BlockSpec((1, 1, s, V_EXT), lambda bi, hi, i: (bi, hi, 0, 0)),
                  pl.BlockSpec((1, 1, lc, dq), lambda bi, hi, i: (bi, hi, 0, 0)),
                  pl.BlockSpec((1, 1, lc, V_EXT), lambda bi, hi, i: (bi, hi, 0, 0))],
        out_specs=pl.BlockSpec((1, tq, V_DIM), lambda bi, hi, i: (bi, i, hi)),
        out_shape=jax.ShapeDtypeStruct((b, s, h * V_DIM), BF16),
        scratch_shapes=[pltpu.VMEM((tq, LANES), F32), pltpu.VMEM((tq, V_EXT), F32),
                        pltpu.VMEM((tq, tk), F32), pltpu.VMEM((tq, tk), F32)],
        compiler_params=_params(("arbitrary", "arbitrary", "arbitrary")),
        name="attention",
    )(q, k, v, kc, vc)


HALO = 16


def _outproj_kernel(u_ref, up_ref, un_ref, bg_ref, cw_ref, ya_ref, wo_ref, x_ref, g1_ref, n2_ref, sc_ref, sh_ref,
                    wr2_ref, x1_ref, tab_ref, afft_ref, *, tm, d):
    i = pl.program_id(1)
    last = pl.num_programs(1) - 1
    u = u_ref[0].astype(F32)
    rows = lax.broadcasted_iota(I32, (tm, 1), 0)
    prev_row = jnp.where(i > 0, up_ref[0][HALO - 1:HALO, :].astype(F32), 0.0)
    next_row = jnp.where(i < last, un_ref[0][0:1, :].astype(F32), 0.0)
    um1 = jnp.where(rows == 0, prev_row, pltpu.roll(u, 1, 0))
    up1 = jnp.where(rows == tm - 1, next_row, pltpu.roll(u, tm - 1, 0))
    cw = cw_ref[...]
    yc = (bg_ref[0].astype(F32) * (um1 * cw[0:1] + u * cw[1:2] + up1 * cw[2:3])).astype(BF16)
    nsub = 2
    tr = tm // nsub
    for r in range(nsub):
        rs = slice(r * tr, (r + 1) * tr)
        y = _dot(yc[rs], wo_ref[0:CONV_WIDTH, :]) + _dot(ya_ref[0, rs, :], wo_ref[CONV_WIDTH:, :])
        x1 = x_ref[0, rs, :] + g1_ref[0] * y
        x1_ref[0, rs, :] = x1
        h2 = _rms(x1) * n2_ref[...]
        h2 = h2 * (1.0 + sc_ref[0]) + sh_ref[0]
        tab_ref[0, rs, 0:d] = h2
        h_hi = h2.astype(BF16)
        h_lo = (h2 - h_hi.astype(F32)).astype(BF16)
        l2 = _dot(h_hi, wr2_ref[...])
        logits = l2[:, 0:LANES] + l2[:, LANES:2 * LANES] + _dot(h_lo, wr2_ref[:, 0:LANES])
        lane = lax.broadcasted_iota(I32, logits.shape, 1)
        logits = jnp.where(lane < N_EXPERTS, logits, -jnp.inf)
        e = jnp.exp(logits - jnp.max(logits, axis=-1, keepdims=True))
        aff = e / jnp.sum(e, axis=-1, keepdims=True)
        tab_ref[0, rs, d:d + LANES] = aff
        afft_ref[0, :, rs] = aff.T[0:N_EXPERTS, :]


def _outproj(u, bg, conv_w, y_attn, w_out, x, g1, n2, sc2, sh2, wr2, *, tm):
    b, s, d = x.shape
    nh = tm // HALO
    nhb = s // HALO
    row = lambda n: pl.BlockSpec((1, tm, n), lambda bi, i: (bi, i, 0))
    vec = pl.BlockSpec((1, 1, d), lambda bi, i: (bi, 0, 0))
    return pl.pallas_call(
        functools.partial(_outproj_kernel, tm=tm, d=d),
        grid=(b, s // tm),
        in_specs=[row(CONV_WIDTH),
                  pl.BlockSpec((1, HALO, CONV_WIDTH), lambda bi, i: (bi, jnp.maximum(i * nh - 1, 0), 0)),
                  pl.BlockSpec((1, HALO, CONV_WIDTH), lambda bi, i: (bi, jnp.minimum((i + 1) * nh, nhb - 1), 0)),
                  row(CONV_WIDTH), _resident(conv_w.shape), row(MLA_WIDTH), _resident(w_out.shape), row(d),
                  vec, _resident((1, d)), vec, vec, _resident(wr2.shape)],
        out_specs=[row(d), row(d + LANES), pl.BlockSpec((1, N_EXPERTS, tm), lambda bi, i: (bi, 0, i))],
        out_shape=[jax.ShapeDtypeStruct((b, s, d), F32), jax.ShapeDtypeStruct((b, s, d + LANES), F32),
                   jax.ShapeDtypeStruct((b, N_EXPERTS, s), F32)],
        compiler_params=_params(("arbitrary", "arbitrary")),
        name="outproj_router",
    )(u, u, u, bg, conv_w, y_attn, w_out, x, g1, n2, sc2, sh2, wr2)


def _routing_kernel(aff_ref, tri_ref, pos_ref, offs_ref, *, cap, blk, rows_per_expert):
    bi = pl.program_id(0)
    aff = aff_ref[0]
    ne, s = aff.shape
    capf = jnp.float32(cap)

    def bisect(t, prefix):
        cand = prefix | jnp.left_shift(jnp.int32(1), 30 - t)
        cnt = jnp.sum(jnp.where(aff >= lax.bitcast_convert_type(cand, F32), 1.0, 0.0), axis=1, keepdims=True)
        return jnp.where(cnt >= capf, cand, prefix)

    floor_bits = lax.fori_loop(0, 31, bisect, jnp.zeros((ne, 1), I32))
    thr = jnp.min(jnp.where(aff >= lax.bitcast_convert_type(floor_bits, F32), aff, jnp.inf), axis=1, keepdims=True)
    gt = aff > thr
    eq = aff == thr
    need = capf - jnp.sum(jnp.where(gt, 1.0, 0.0), axis=1, keepdims=True)
    tri = tri_ref[...]

    def cumsum_blocks(mask_f32):
        run = jnp.zeros((ne, 1), F32)
        parts, starts = [], []
        for kb in range(s // blk):
            c = _dot(mask_f32[:, kb * blk:(kb + 1) * blk].astype(BF16), tri)
            starts.append(run)
            parts.append(c + run)
            run = run + c[:, blk - 1:blk]
        return jnp.concatenate(parts, axis=1), starts, run

    eqf = jnp.where(eq, 1.0, 0.0)
    eq_incl, _, _ = cumsum_blocks(eqf)
    sel = jnp.where(gt, 1.0, jnp.where(eq & (eq_incl - eqf < need), 1.0, 0.0))
    incl, starts, _ = cumsum_blocks(sel)

    base = (lax.broadcasted_iota(I32, (ne, 1), 0) * rows_per_expert + bi * cap)
    pos_ref[0] = jnp.where(sel > 0.0, (incl - sel).astype(I32) + base, -1)
    lane = lax.broadcasted_iota(I32, (ne, LANES), 1)
    offs = jnp.full((ne, LANES), cap, I32) + base
    for kb, st in enumerate(starts):
        offs = jnp.where(lane == kb, st.astype(I32) + base, offs)
    offs_ref[0] = offs


def _routing(aff_t, *, cap, blk):
    b, ne, s = aff_t.shape
    assert s // blk + 1 <= LANES
    tri = jnp.asarray(np.triu(np.ones((blk, blk), np.float32)), BF16)
    return pl.pallas_call(
        functools.partial(_routing_kernel, cap=cap, blk=blk, rows_per_expert=b * cap),
        grid=(b,),
        in_specs=[pl.BlockSpec((1, ne, s), lambda bi: (bi, 0, 0)), _resident(tri.shape)],
        out_specs=[pl.BlockSpec((1, ne, s), lambda bi: (bi, 0, 0)),
                   pl.BlockSpec((1, ne, LANES), lambda bi: (bi, 0, 0))],
        out_shape=[jax.ShapeDtypeStruct((b, ne, s), I32), jax.ShapeDtypeStruct((b, ne, LANES), I32)],
        compiler_params=_params(("arbitrary",)),
        name="routing",
    )(aff_t, tri)


def _slots_kernel(offs_ref, pos_ref, idx_ref, *, cap, blk, w, rows_per_expert):
    bi = pl.program_id(0)
    ne, s = pos_ref.shape[1], pos_ref.shape[2]
    nb = s // blk
    idx_ref[...] = jnp.zeros(idx_ref.shape, I32)
    sub = lax.broadcasted_iota(I32, (w, blk), 0)
    tok = lax.broadcasted_iota(I32, (w, blk), 1)

    def block(kb, carry):
        tok1 = tok + (kb * blk + bi * s + 1)
        windows, trips = [], []
        for e in range(ne):
            base = e * rows_per_expert + bi * cap
            o = (bi * (nb + 1) + kb) * ne + e
            lo8 = (offs_ref[o] - base) // 8 * 8
            hi = offs_ref[o + ne] - base
            posrow = pos_ref[0, e:e + 1, pl.ds(pl.multiple_of(kb * blk, blk), blk)] - base

            def window(r, c, e=e, lo8=lo8, posrow=posrow):
                start = pl.multiple_of(jnp.minimum(lo8 + r * w, cap - w), 8)
                hit = jnp.where(sub + start == posrow, tok1, 0)
                val = jnp.sum(hit.astype(F32), axis=1, keepdims=True).astype(I32)
                old = idx_ref[0, pl.ds(start, w), e:e + 1]
                idx_ref[0, pl.ds(start, w), e:e + 1] = jnp.where(val > 0, val - 1, old)
                return c

            windows.append(window)
            trips.append((hi - lo8 + w - 1) // w)
        for window in windows:
            window(0, 0)
        for window, n in zip(windows, trips):
            lax.fori_loop(1, n, window, 0)
        return carry

    lax.fori_loop(0, nb, block, 0)


def _slots(offs_flat, pos, *, cap, blk, w):
    b, ne, s = pos.shape
    return pl.pallas_call(
        functools.partial(_slots_kernel, cap=cap, blk=blk, w=w, rows_per_expert=b * cap),
        grid_spec=pltpu.PrefetchScalarGridSpec(
            num_scalar_prefetch=1,
            grid=(b,),
            in_specs=[pl.BlockSpec((1, ne, s), lambda bi, o: (bi, 0, 0))],
            out_specs=pl.BlockSpec((1, cap, ne), lambda bi, o: (bi, 0, 0))),
        out_shape=jax.ShapeDtypeStruct((b, cap, ne), I32),
        compiler_params=_params(("arbitrary",)),
        name="slots",
    )(offs_flat, pos)


def _ffn_kernel(idx_ref, tab_hbm, wg_hbm, wu_hbm, wd_hbm, y_ref, buf_a, buf_b, buf_c, wg_ref, wu_ref, wd_ref,
                sg_ref, su_ref, sd_ref, sem, wsem, *, t, d):
    e = pl.program_id(0)
    j = pl.program_id(1)
    ne, nt = pl.num_programs(0), pl.num_programs(1)
    step = e * nt + j
    nsteps = ne * nt
    rc, fc = sg_ref.shape[0], sd_ref.shape[0]

    def row_copy(base, i, buf, sl):
        return pltpu.make_async_copy(tab_hbm.at[pl.ds(idx_ref[base + i], 1)], buf.at[pl.ds(i, 1)], sem.at[sl])

    def wait_rows(buf, sl):
        pltpu.make_async_copy(tab_hbm.at[pl.ds(0, t)], buf, sem.at[sl]).wait()

    def piece_copies(ex, pc):
        return [pltpu.make_async_copy(wg_hbm.at[ex, pl.ds(pl.multiple_of(pc * rc, rc), rc)], sg_ref, wsem.at[0]),
                pltpu.make_async_copy(wu_hbm.at[ex, pl.ds(pl.multiple_of(pc * rc, rc), rc)], su_ref, wsem.at[1]),
                pltpu.make_async_copy(wd_hbm.at[ex, pl.ds(pl.multiple_of(pc * fc, fc), fc)], sd_ref, wsem.at[2])]

    def cast_piece(st, pc):
        wg_ref[st, pl.ds(pl.multiple_of(pc * rc, rc), rc), :] = sg_ref[...].astype(BF16)
        wu_ref[st, pl.ds(pl.multiple_of(pc * rc, rc), rc), :] = su_ref[...].astype(BF16)
        wd_ref[st, pl.ds(pl.multiple_of(pc * fc, fc), fc), :] = sd_ref[...].astype(BF16)

    @pl.when(step == 0)
    def _():
        def issue(i, carry):
            row_copy(0, i, buf_a, 0).start()
            row_copy(jnp.minimum(1, nsteps - 1) * t, i, buf_b, 1).start()
            return carry
        lax.fori_loop(0, t, issue, 0)

        def load(pc, carry):
            cps = piece_copies(0, pc)
            for cp in cps:
                cp.start()
            for cp in cps:
                cp.wait()
            cast_piece(0, pc)
            return carry
        lax.fori_loop(0, nt - 1, load, 0)
        for cp in piece_copies(0, nt - 1):
            cp.start()

    pe = jnp.where(j > 0, e, e - 1)
    pj = jnp.where(j > 0, j - 1, nt - 1)
    nxt_e = jnp.minimum(e + 1, ne - 1)
    ws = e % 2

    def run(cur, cur_sl, mid, mid_sl, far, far_sl):
        wait_rows(cur, cur_sl)
        for cp in piece_copies(jnp.minimum(pe + 1, ne - 1), pj):
            cp.wait()
        cast_piece((pe + 1) % 2, pj)
        for cp in piece_copies(nxt_e, j):
            cp.start()
        base = jnp.minimum(step + 2, nsteps - 1) * t
        head_rows = FFN_HEAD_ROWS * t // FFN_T
        for i in range(head_rows):
            row_copy(base, i, far, far_sl).start()

        @pl.when(idx_ref[base] < 0)
        def _():
            y_ref[0:16, 0:LANES] = jnp.zeros((16, LANES), BF16)

        for i in range(head_rows, t):
            row_copy(base, i, far, far_sl).start()
        xb = cur[:, 0:d].astype(BF16)
        aff = cur[:, d:d + LANES]
        lane = lax.broadcasted_iota(I32, aff.shape, 1)
        gate_w = jnp.sum(jnp.where(lane == e, aff, 0.0), axis=-1, keepdims=True)
        g = _dot(xb, wg_ref[ws])
        up = _dot(xb, wu_ref[ws])
        hid = ((g / (1.0 + jnp.exp(-g))) * up).astype(BF16)
        y = _dot(hid, wd_ref[ws])
        y_ref[...] = (y * gate_w).astype(BF16)

        @pl.when(step == nsteps - 1)
        def _():
            wait_rows(mid, mid_sl)
            wait_rows(far, far_sl)
            for cp in piece_copies(nxt_e, j):
                cp.wait()

    bufs = (buf_a, buf_b, buf_c)
    for r in range(3):
        @pl.when(step % 3 == r)
        def _(r=r):
            run(bufs[r], r, bufs[(r + 1) % 3], (r + 1) % 3, bufs[(r + 2) % 3], (r + 2) % 3)


def _expert_ffn(idx_flat, tab, w_gate, w_up, w_down, *, t):
    ne, d, ff = w_gate.shape
    rows = idx_flat.shape[0]
    nt = rows // ne // t
    dw = tab.shape[1]
    bf16_rows = 16
    assert d % (nt * bf16_rows) == 0 and ff % (nt * bf16_rows) == 0
    return pl.pallas_call(
        functools.partial(_ffn_kernel, t=t, d=d),
        grid_spec=pltpu.PrefetchScalarGridSpec(
            num_scalar_prefetch=1,
            grid=(ne, nt),
            in_specs=[pl.BlockSpec(memory_space=pl.ANY)] * 4,
            out_specs=pl.BlockSpec((t, d), lambda e, j, idx: (e * nt + j, 0)),
            scratch_shapes=[pltpu.VMEM((t, dw), F32), pltpu.VMEM((t, dw), F32), pltpu.VMEM((t, dw), F32),
                            pltpu.VMEM((2, d, ff), BF16), pltpu.VMEM((2, d, ff), BF16), pltpu.VMEM((2, ff, d), BF16),
                            pltpu.VMEM((d // nt, ff), F32), pltpu.VMEM((d // nt, ff), F32),
                            pltpu.VMEM((ff // nt, d), F32),
                            pltpu.SemaphoreType.DMA((3,)), pltpu.SemaphoreType.DMA((3,))]),
        out_shape=jax.ShapeDtypeStruct((rows, d), BF16),
        compiler_params=_params(("arbitrary", "arbitrary"), vmem=FFN_VMEM_LIMIT),
        name="expert_ffn",
    )(idx_flat, tab, w_gate, w_up, w_down)


def _combine_kernel(offs_ref, pos_ref, y_hbm, x1_ref, g2_ref, fg_ref, o_ref, ybuf, acc_ref, sem, *, tt, w, total_rows):
    ne = pos_ref.shape[1]
    nk = pl.num_programs(1)
    step = pl.program_id(0) * nk + pl.program_id(1)
    nsteps = pl.num_programs(0) * nk
    slot = step % 2

    def bounds(st):
        o0 = (st // nk * (nk + 1) + st % nk) * ne
        lo8 = [(offs_ref[o0 + e] // 8) * 8 for e in range(ne)]
        hi = [offs_ref[o0 + ne + e] for e in range(ne)]
        return lo8, hi

    def window_copies(lo8, r, sl):
        starts = [pl.multiple_of(jnp.minimum(lo8[e] + r * w, total_rows - w), 8) for e in range(ne)]
        copies = [pltpu.make_async_copy(y_hbm.at[pl.ds(starts[e], w)], ybuf.at[sl, pl.ds(e * w, w)], sem.at[sl, e])
                  for e in range(ne)]
        return starts, copies

    @pl.when(step == 0)
    def _():
        for cp in window_copies(bounds(0)[0], 0, 0)[1]:
            cp.start()

    lo8, hi = bounds(step)
    starts0, copies0 = window_copies(lo8, 0, slot)
    for cp in copies0:
        cp.wait()

    @pl.when(step + 1 < nsteps)
    def _():
        for cp in window_copies(bounds(step + 1)[0], 0, 1 - slot)[1]:
            cp.start()

    pos = pos_ref[0]

    d = x1_ref.shape[2]
    halves = (slice(0, d // 2), slice(d // 2, d))

    def onehot(r, starts):
        sub = lax.broadcasted_iota(I32, (w, tt), 0)
        want = [jnp.where(pos[e:e + 1, :] >= lo8[e] + r * w, pos[e:e + 1, :], -1) for e in range(ne)]
        blocks = [jnp.where(sub + starts[e] == want[e], 1.0, 0.0).astype(BF16) for e in range(ne)]
        return jnp.concatenate(blocks, axis=0)

    def contribution(oh, cols):
        return lax.dot_general(oh, ybuf[slot, :, cols], (((0,), (0,)), ((), ())), preferred_element_type=F32)

    def finish(parts):
        x2 = [x1_ref[0, :, cols] + g2_ref[0, :, cols] * part for cols, part in zip(halves, parts)]
        ms = sum(jnp.sum(v * v, axis=-1, keepdims=True) for v in x2) / d
        scale = lax.rsqrt(ms + EPS)
        for cols, v in zip(halves, x2):
            o_ref[0, :, cols] = (v * scale) * fg_ref[:, cols]

    oh0 = onehot(0, starts0)
    parts = [contribution(oh0, cols) for cols in halves]
    finish(parts)

    rounds = jnp.int32(1)
    for e in range(ne):
        rounds = jnp.maximum(rounds, (hi[e] - lo8[e] + w - 1) // w)

    @pl.when(rounds > 1)
    def _():
        for cols, part in zip(halves, parts):
            acc_ref[:, cols] = part

        def extra_round(r, carry):
            starts, copies = window_copies(lo8, r, slot)
            for cp in copies:
                cp.start()
            for cp in copies:
                cp.wait()
            oh = onehot(r, starts)
            for cols in halves:
                acc_ref[:, cols] += contribution(oh, cols)
            return carry

        lax.fori_loop(1, rounds, extra_round, 0)
        finish([acc_ref[:, cols] for cols in halves])


def _combine(offs_flat, pos, y, x1, g2, final_g, *, tt, w):
    b, s, d = x1.shape
    ne = pos.shape[1]
    total_rows = y.shape[0]
    return pl.pallas_call(
        functools.partial(_combine_kernel, tt=tt, w=w, total_rows=total_rows),
        grid_spec=pltpu.PrefetchScalarGridSpec(
            num_scalar_prefetch=1,
            grid=(b, s // tt),
            in_specs=[pl.BlockSpec((1, ne, tt), lambda bi, k, o: (bi, 0, k)),
                      pl.BlockSpec(memory_space=pl.ANY),
                      pl.BlockSpec((1, tt, d), lambda bi, k, o: (bi, k, 0)),
                      pl.BlockSpec((1, 1, d), lambda bi, k, o: (bi, 0, 0)),
                      pl.BlockSpec((1, d), lambda bi, k, o: (0, 0))],
            out_specs=pl.BlockSpec((1, tt, d), lambda bi, k, o: (bi, k, 0)),
            scratch_shapes=[pltpu.VMEM((2, ne * w, d), BF16), pltpu.VMEM((tt, d), F32),
                            pltpu.SemaphoreType.DMA((2, ne))]),
        out_shape=jax.ShapeDtypeStruct((b, s, d), F32),
        compiler_params=_params(("arbitrary", "arbitrary")),
        name="combine_final",
    )(offs_flat, pos, y, x1, g2, final_g)


def _rope_tables(s):
    n_rows = s // GRID_W
    row = np.repeat(np.arange(n_rows), GRID_W).astype(np.float64)
    col = np.tile(np.arange(GRID_W), n_rows).astype(np.float64)
    inv_freq = ROPE_THETA ** (-np.arange(ROPE_PAIRS, dtype=np.float64) / ROPE_PAIRS)
    ang_r = row[:, None] * inv_freq[None, :]
    ang_c = col[:, None] * inv_freq[None, :]
    cos64 = np.concatenate([np.cos(ang_r)] * 2 + [np.cos(ang_c)] * 2, axis=-1).astype(np.float32)
    sin64 = np.concatenate([-np.sin(ang_r), np.sin(ang_r), -np.sin(ang_c), np.sin(ang_c)], axis=-1).astype(np.float32)
    return cos64, sin64


ROPE_SWAP = np.concatenate([np.arange(16, 32), np.arange(0, 16), np.arange(48, 64), np.arange(32, 48)])


def kernel(x, c, ctx, c_ctx, w_mod, b_mod, norm1_g, norm2_g, w_in, conv_w, q_norm_g, w_uq, kv_norm_g, w_ukv, w_out,
           w_router, w_gate, w_up, w_down, final_g):
    b, s, d = x.shape
    lc = ctx.shape[1]
    assert w_in.shape[0] == 1, "single-layer stack"
    assert b <= 7 and s % max(ROW_TM, QKV_TM, ATT_TQ, ATT_TK, CMB_TT) == 0
    cap = EC_FACTOR * s // N_EXPERTS
    assert (b * cap) % FFN_T == 0 and cap % CMB_W == 0

    cvec = jnp.zeros((8, d), F32).at[0:b].set(c).at[b].set(c_ctx)
    mod = _modulation(cvec, w_mod[0], b_mod[0][None, :])
    sh1, sc1, g1, sh2, sc2, g2 = [mod[0:b, None, i * d:(i + 1) * d] for i in range(6)]
    shc1, scc1 = mod[b:b + 1, None, 0:d], mod[b:b + 1, None, d:2 * d]

    n_main = 3 * CONV_WIDTH + Q_LORA + KV_LORA
    w_main = w_in[0, :, 0:n_main].astype(BF16)
    w_kr = w_in[0, :, n_main:]
    w_kr2 = jnp.concatenate([w_kr, w_kr[:, ROPE_SWAP]], axis=-1).astype(BF16)
    w_qn = w_uq[0, :, :, 0:QK_NOPE].reshape(Q_LORA, MLA_HEADS * QK_NOPE).astype(BF16)
    w_qr3 = w_uq[0, :, :, QK_NOPE:]
    w_qr = w_qr3.reshape(Q_LORA, MLA_HEADS * QK_ROPE).astype(BF16)
    w_qrs = w_qr3[:, :, ROPE_SWAP].reshape(Q_LORA, MLA_HEADS * QK_ROPE).astype(BF16)
    w_kn = w_ukv[0, :, :, 0:QK_NOPE].reshape(KV_LORA, MLA_HEADS * QK_NOPE).astype(BF16)
    w_v = w_ukv[0, :, :, QK_NOPE:].reshape(KV_LORA, MLA_HEADS * V_DIM).astype(BF16)
    w_o = w_out[0].astype(BF16)
    wr = jnp.zeros((d, LANES), F32).at[:, 0:N_EXPERTS].set(w_router[0])
    wr_hi = wr.astype(BF16)
    wr2 = jnp.concatenate([wr_hi, (wr - wr_hi.astype(F32)).astype(BF16)], axis=-1)

    cos64, sin64 = _rope_tables(s)
    cos2 = jnp.asarray(np.concatenate([cos64, cos64], axis=-1))
    sin2 = jnp.asarray(np.concatenate([sin64, sin64], axis=-1))
    ktab = jnp.asarray(np.concatenate([cos64, sin64], axis=-1))

    n1 = norm1_g[0][None, :]
    qg, kvg = q_norm_g[0][None, :], kv_norm_g[0][None, :]
    u, bg, cqn, ckvn, kr2 = _inproj(x, sc1, sh1, n1, w_main, w_kr2, qg, kvg, with_conv=True, tm=ROW_TM)
    _, ckvn_c, kr2_c = _inproj(ctx, scc1, shc1, n1, w_main, w_kr2, qg, kvg, with_conv=False, tm=lc)

    q = _qproj(cqn, cos2, sin2, w_qn, w_qr, w_qrs, tm=QKV_TM)
    k, v = _kvproj(ckvn, kr2, ktab, w_kn, w_v, tm=QKV_TM)
    kc, vc = _kvproj(ckvn_c, kr2_c, None, w_kn, w_v, tm=lc)
    y_attn = _attention(q, k, v, kc, vc, tq=ATT_TQ, tk=ATT_TK)

    x1, tab, aff_t = _outproj(u, bg, conv_w[0], y_attn, w_o, x, g1, norm2_g[0][None, :], sc2, sh2, wr2, tm=ROW_TM)

    pos, offs = _routing(aff_t, cap=cap, blk=CMB_TT)
    nk1 = s // CMB_TT + 1
    offs_flat = jnp.swapaxes(offs[:, :, 0:nk1], 1, 2).reshape(-1)
    idx_t = _slots(offs_flat, pos, cap=cap, blk=CMB_TT, w=CMB_W)
    idx_flat = jnp.transpose(idx_t, (2, 0, 1)).reshape(-1)
    y = _expert_ffn(idx_flat, tab.reshape(b * s, d + LANES), w_gate[0], w_up[0], w_down[0], t=FFN_T)
    return _combine(offs_flat, pos, y, x1, g2, final_g[None, :], tt=CMB_TT, w=CMB_W)
```

```python
import functools
import math

import jax
import jax.numpy as jnp
import numpy as np
from jax import lax
from jax.experimental import pallas as pl
from jax.experimental.pallas import tpu as pltpu

F32 = jnp.float32
BF16 = jnp.bfloat16
I32 = jnp.int32

GRID_W = 64
CONV_WIDTH = 1024
MLA_HEADS = 8
QK_NOPE = 128
QK_ROPE = 64
V_DIM = 128
Q_LORA = 512
KV_LORA = 512
QK_DIM = QK_NOPE + QK_ROPE
MLA_WIDTH = MLA_HEADS * V_DIM
V_EXT = 2 * V_DIM
N_EXPERTS = 16
EC_FACTOR = 2
ROPE_THETA = 10000.0
ROPE_PAIRS = QK_ROPE // 4
ATTN_SCALE = 1.0 / math.sqrt(QK_DIM)
LOG2E = math.log2(math.e)
EPS = 1e-6

LANES = 128
VMEM_LIMIT = 56 * 1024 * 1024
FFN_VMEM_LIMIT = 60 * 1024 * 1024

MOD_TK = 256
ROW_TM = 512
QKV_TM = 1024
ATT_TQ = 1024
ATT_TK = 512
FFN_T = 256
FFN_HEAD_ROWS = 96
CMB_TT = 256
CMB_W = 64


def _dot(a, b):
    return jnp.dot(a, b, preferred_element_type=F32)


def _floor8(n):
    return lax.shift_left(lax.shift_right_logical(n, 3), 3)


def _cdiv_pow2(n, m):
    assert m & (m - 1) == 0
    return lax.shift_right_logical(n + (m - 1), m.bit_length() - 1)


def _params(sem, vmem=VMEM_LIMIT):
    return pltpu.CompilerParams(dimension_semantics=sem, vmem_limit_bytes=vmem)


def _resident(shape):
    nd = len(shape)
    return pl.BlockSpec(shape, lambda *_: (0,) * nd, pipeline_mode=pl.Buffered(1))


def _mod_kernel(c_ref, w_ref, b_ref, o_ref):
    k = pl.program_id(0)
    cv = c_ref[...]
    a = cv / (1.0 + jnp.exp(-cv))
    a_hi = a.astype(BF16)
    a_lo = (a - a_hi.astype(F32)).astype(BF16)
    w = w_ref[...].astype(BF16)
    p = _dot(jnp.concatenate([a_hi, a_lo], axis=0), w)
    part = p[0:8] + p[8:16]

    @pl.when(k == 0)
    def _():
        o_ref[...] = part + b_ref[...]

    @pl.when(k > 0)
    def _():
        o_ref[...] += part


def _modulation(cvec, w_mod, b_mod):
    d, n = w_mod.shape
    return pl.pallas_call(
        _mod_kernel,
        grid=(d // MOD_TK,),
        in_specs=[pl.BlockSpec((8, MOD_TK), lambda k: (0, k)),
                  pl.BlockSpec((MOD_TK, n), lambda k: (k, 0)),
                  pl.BlockSpec((1, n), lambda k: (0, 0))],
        out_specs=pl.BlockSpec((8, n), lambda k: (0, 0)),
        out_shape=jax.ShapeDtypeStruct((8, n), F32),
        compiler_params=_params(("arbitrary",)),
        name="modulation",
    )(cvec, w_mod, b_mod)


def _rms(v):
    return v * lax.rsqrt(jnp.mean(v * v, axis=-1, keepdims=True) + EPS)


def _inproj_kernel(x_ref, sc_ref, sh_ref, g_ref, w_ref, wkr_ref, qg_ref, kvg_ref, *outs, with_conv):
    h = _rms(x_ref[0]) * g_ref[...]
    h = h * (1.0 + sc_ref[0]) + sh_ref[0]
    hb = h.astype(BF16)
    c = CONV_WIDTH
    if with_conv:
        u_ref, bg_ref, cq_ref, ckv_ref, kr_ref = outs
        xin = _dot(hb, w_ref[:, 0:c])
        cg = _dot(hb, w_ref[:, 2 * c:3 * c])
        u_ref[0] = (cg * xin).astype(BF16)
        bg_ref[0] = _dot(hb, w_ref[:, c:2 * c]).astype(BF16)
    else:
        cq_ref, ckv_ref, kr_ref = outs
    o = 3 * c
    cq = _dot(hb, w_ref[:, o:o + Q_LORA])
    cq_ref[0] = (_rms(cq) * qg_ref[...]).astype(BF16)
    ckv = _dot(hb, w_ref[:, o + Q_LORA:o + Q_LORA + KV_LORA])
    ckv_ref[0] = (_rms(ckv) * kvg_ref[...]).astype(BF16)
    kr_ref[0] = _dot(hb, wkr_ref[...])


def _inproj(x, scale, shift, gain, w_main, w_kr2, q_g, kv_g, *, with_conv, tm):
    b, s, d = x.shape
    per_batch = scale.shape[0] > 1
    vec = pl.BlockSpec((1, 1, d), (lambda bi, i: (bi, 0, 0)) if per_batch else (lambda bi, i: (0, 0, 0)))
    row = lambda n: pl.BlockSpec((1, tm, n), lambda bi, i: (bi, i, 0))
    outs, specs = [], []
    if with_conv:
        outs += [jax.ShapeDtypeStruct((b, s, CONV_WIDTH), BF16)] * 2
        specs += [row(CONV_WIDTH)] * 2
    outs += [jax.ShapeDtypeStruct((b, s, Q_LORA), BF16), jax.ShapeDtypeStruct((b, s, KV_LORA), BF16),
             jax.ShapeDtypeStruct((b, s, 2 * QK_ROPE), F32)]
    specs += [row(Q_LORA), row(KV_LORA), row(2 * QK_ROPE)]
    return pl.pallas_call(
        functools.partial(_inproj_kernel, with_conv=with_conv),
        grid=(b, s // tm),
        in_specs=[row(d), vec, vec, _resident((1, d)), _resident(w_main.shape), _resident(w_kr2.shape),
                  _resident((1, Q_LORA)), _resident((1, KV_LORA))],
        out_specs=specs,
        out_shape=outs,
        compiler_params=_params(("arbitrary", "arbitrary")),
        name="inproj_conv" if with_conv else "inproj_ctx",
    )(x, scale, shift, gain, w_main, w_kr2, q_g, kv_g)


def _qproj_kernel(cq_ref, cos_ref, sin_ref, wn_ref, wr_ref, wrs_ref, q_ref):
    cq = cq_ref[0]
    qn = _dot(cq, wn_ref[...])
    qr = _dot(cq, wr_ref[...])
    qrs = _dot(cq, wrs_ref[...])
    reps = MLA_HEADS * QK_ROPE // LANES
    cos = jnp.concatenate([cos_ref[...]] * reps, axis=-1)
    sin = jnp.concatenate([sin_ref[...]] * reps, axis=-1)
    qrot = qr * cos + qrs * sin
    sc = ATTN_SCALE * LOG2E
    for h in range(MLA_HEADS):
        q_ref[0, h, :, 0:QK_NOPE] = (qn[:, h * QK_NOPE:(h + 1) * QK_NOPE] * sc).astype(BF16)
        q_ref[0, h, :, QK_NOPE:QK_DIM] = (qrot[:, h * QK_ROPE:(h + 1) * QK_ROPE] * sc).astype(BF16)


def _qproj(cqn, cos2, sin2, w_qn, w_qr, w_qrs, *, tm):
    b, s, r = cqn.shape
    return pl.pallas_call(
        _qproj_kernel,
        grid=(b, s // tm),
        in_specs=[pl.BlockSpec((1, tm, r), lambda bi, i: (bi, i, 0)),
                  pl.BlockSpec((tm, LANES), lambda bi, i: (i, 0)),
                  pl.BlockSpec((tm, LANES), lambda bi, i: (i, 0)),
                  _resident(w_qn.shape), _resident(w_qr.shape), _resident(w_qrs.shape)],
        out_specs=pl.BlockSpec((1, MLA_HEADS, tm, QK_DIM), lambda bi, i: (bi, 0, i, 0)),
        out_shape=jax.ShapeDtypeStruct((b, MLA_HEADS, s, QK_DIM), BF16),
        compiler_params=_params(("arbitrary", "arbitrary")),
        name="qproj",
    )(cqn, cos2, sin2, w_qn, w_qr, w_qrs)


def _kvproj_kernel(ckv_ref, kr_ref, *rest, rope):
    if rope:
        tab_ref, wk_ref, wv_ref, k_ref, v_ref = rest
        prod = kr_ref[0] * tab_ref[...]
        krot = prod[:, 0:QK_ROPE] + prod[:, QK_ROPE:2 * QK_ROPE]
    else:
        wk_ref, wv_ref, k_ref, v_ref = rest
        krot = kr_ref[0][:, 0:QK_ROPE]
    ckv = ckv_ref[0]
    kn = _dot(ckv, wk_ref[...])
    vv = _dot(ckv, wv_ref[...])
    krot = krot.astype(BF16)
    for h in range(MLA_HEADS):
        k_ref[0, h, :, 0:QK_NOPE] = kn[:, h * QK_NOPE:(h + 1) * QK_NOPE].astype(BF16)
        k_ref[0, h, :, QK_NOPE:QK_DIM] = krot
        v_ref[0, h, :, 0:V_DIM] = vv[:, h * V_DIM:(h + 1) * V_DIM].astype(BF16)
        v_ref[0, h, :, V_DIM:V_EXT] = jnp.ones((vv.shape[0], V_EXT - V_DIM), BF16)


def _kvproj(ckvn, kr2, tab, w_kn, w_v, *, tm):
    b, s, r = ckvn.shape
    rope = tab is not None
    ins = [ckvn, kr2]
    specs = [pl.BlockSpec((1, tm, r), lambda bi, i: (bi, i, 0)),
             pl.BlockSpec((1, tm, 2 * QK_ROPE), lambda bi, i: (bi, i, 0))]
    if rope:
        ins.append(tab)
        specs.append(pl.BlockSpec((tm, 2 * QK_ROPE), lambda bi, i: (i, 0)))
    ins += [w_kn, w_v]
    specs += [_resident(w_kn.shape), _resident(w_v.shape)]
    return pl.pallas_call(
        functools.partial(_kvproj_kernel, rope=rope),
        grid=(b, s // tm),
        in_specs=specs,
        out_specs=[pl.BlockSpec((1, MLA_HEADS, tm, QK_DIM), lambda bi, i: (bi, 0, i, 0)),
                   pl.BlockSpec((1, MLA_HEADS, tm, V_EXT), lambda bi, i: (bi, 0, i, 0))],
        out_shape=[jax.ShapeDtypeStruct((b, MLA_HEADS, s, QK_DIM), BF16),
                   jax.ShapeDtypeStruct((b, MLA_HEADS, s, V_EXT), BF16)],
        compiler_params=_params(("arbitrary", "arbitrary")),
        name="kvproj_rope" if rope else "kvproj_ctx",
    )(*ins)


def _attn_kernel(q_ref, k_ref, v_ref, kc_ref, vc_ref, o_ref, m_ref, acc_ref, sa_ref, sb_ref, *, tk):
    q = q_ref[0, 0]
    m_ref[...] = jnp.full(m_ref.shape, -jnp.inf, F32)
    acc_ref[...] = jnp.zeros(acc_ref.shape, F32)

    def lanes(a, n):
        return jnp.concatenate([a] * (n // LANES), axis=1)

    def scores(k):
        return lax.dot_general(q, k, (((1,), (1,)), ((), ())), preferred_element_type=F32)

    def update(s, v):
        m_old = m_ref[...]
        m_new = jnp.maximum(m_old, jnp.max(s, axis=-1, keepdims=True))
        alpha = jnp.exp2(m_old - m_new)
        p = jnp.exp2(s - lanes(m_new, s.shape[1]))
        acc_ref[...] = lanes(alpha, acc_ref.shape[1]) * acc_ref[...] + _dot(p.astype(BF16), v)
        m_ref[...] = m_new

    def chunk(ref, j):
        return ref[0, 0, pl.ds(pl.multiple_of(j * tk, tk), tk), :]

    n = k_ref.shape[2] // tk
    bufs = (sa_ref, sb_ref)
    sa_ref[...] = scores(chunk(k_ref, 0))
    for j in range(n):
        nxt = bufs[(j + 1) % 2]
        if j + 1 < n:
            nxt[...] = scores(chunk(k_ref, j + 1))
        else:
            s_ctx = scores(kc_ref[0, 0])
        update(bufs[j % 2][...], chunk(v_ref, j))
    update(s_ctx, vc_ref[0, 0])
    o_ref[0] = (acc_ref[:, 0:V_DIM] / acc_ref[:, V_DIM:V_EXT]).astype(BF16)


def _attention(q, k, v, kc, vc, *, tq, tk):
    b, h, s, dq = q.shape
    lc = kc.shape[2]
    return pl.pallas_call(
        functools.partial(_attn_kernel, tk=tk),
        grid=(b, h, s // tq),
        in_specs=[pl.BlockSpec((1, 1, tq, dq), lambda bi, hi, i: (bi, hi, i, 0)),
                  pl.BlockSpec((1, 1, s, dq), lambda bi, hi, i: (bi, hi, 0, 0)),
                  pl.BlockSpec((1, 1, s, V_EXT), lambda bi, hi, i: (bi, hi, 0, 0)),
                  pl.BlockSpec((1, 1, lc, dq), lambda bi, hi, i: (bi, hi, 0, 0)),
                  pl.BlockSpec((1, 1, lc, V_EXT), lambda bi, hi, i: (bi, hi, 0, 0))],
        out_specs=pl.BlockSpec((1, tq, V_DIM), lambda bi, hi, i: (bi, i, hi)),
        out_shape=jax.ShapeDtypeStruct((b, s, h * V_DIM), BF16),
        scratch_shapes=[pltpu.VMEM((tq, LANES), F32), pltpu.VMEM((tq, V_EXT), F32),
                        pltpu.VMEM((tq, tk), F32), pltpu.VMEM((tq, tk), F32)],
        compiler_params=_params(("arbitrary", "arbitrary", "arbitrary")),
        name="attention",
    )(q, k, v, kc, vc)


HALO = 16


def _outproj_kernel(u_ref, up_ref, un_ref, bg_ref, cw_ref, ya_ref, wo_ref, x_ref, g1_ref, n2_ref, sc_ref, sh_ref,
                    wr2_ref, x1_ref, tab_ref, afft_ref, *, tm, d):
    i = pl.program_id(1)
    last = pl.num_programs(1) - 1
    u = u_ref[0].astype(F32)
    rows = lax.broadcasted_iota(I32, (tm, 1), 0)
    prev_row = jnp.where(i > 0, up_ref[0][HALO - 1:HALO, :].astype(F32), 0.0)
    next_row = jnp.where(i < last, un_ref[0][0:1, :].astype(F32), 0.0)
    um1 = jnp.where(rows == 0, prev_row, pltpu.roll(u, 1, 0))
    up1 = jnp.where(rows == tm - 1, next_row, pltpu.roll(u, tm - 1, 0))
    cw = cw_ref[...]
    yc = (bg_ref[0].astype(F32) * (um1 * cw[0:1] + u * cw[1:2] + up1 * cw[2:3])).astype(BF16)
    nsub = 2
    tr = tm // nsub
    for r in range(nsub):
        rs = slice(r * tr, (r + 1) * tr)
        y = _dot(yc[rs], wo_ref[0:CONV_WIDTH, :]) + _dot(ya_ref[0, rs, :], wo_ref[CONV_WIDTH:, :])
        x1 = x_ref[0, rs, :] + g1_ref[0] * y
        x1_ref[0, rs, :] = x1
        h2 = _rms(x1) * n2_ref[...]
        h2 = h2 * (1.0 + sc_ref[0]) + sh_ref[0]
        tab_ref[0, rs, 0:d] = h2
        h_hi = h2.astype(BF16)
        h_lo = (h2 - h_hi.astype(F32)).astype(BF16)
        l2 = _dot(h_hi, wr2_ref[...])
        logits = l2[:, 0:LANES] + l2[:, LANES:2 * LANES] + _dot(h_lo, wr2_ref[:, 0:LANES])
        lane = lax.broadcasted_iota(I32, logits.shape, 1)
        logits = jnp.where(lane < N_EXPERTS, logits, -jnp.inf)
        e = jnp.exp(logits - jnp.max(logits, axis=-1, keepdims=True))
        aff = e / jnp.sum(e, axis=-1, keepdims=True)
        tab_ref[0, rs, d:d + LANES] = aff
        afft_ref[0, :, rs] = aff.T[0:N_EXPERTS, :]


def _outproj(u, bg, conv_w, y_attn, w_out, x, g1, n2, sc2, sh2, wr2, *, tm):
    b, s, d = x.shape
    nh = tm // HALO
    nhb = s // HALO
    row = lambda n: pl.BlockSpec((1, tm, n), lambda bi, i: (bi, i, 0))
    vec = pl.BlockSpec((1, 1, d), lambda bi, i: (bi, 0, 0))
    return pl.pallas_call(
        functools.partial(_outproj_kernel, tm=tm, d=d),
        grid=(b, s // tm),
        in_specs=[row(CONV_WIDTH),
                  pl.BlockSpec((1, HALO, CONV_WIDTH), lambda bi, i: (bi, jnp.maximum(i * nh - 1, 0), 0)),
                  pl.BlockSpec((1, HALO, CONV_WIDTH), lambda bi, i: (bi, jnp.minimum((i + 1) * nh, nhb - 1), 0)),
                  row(CONV_WIDTH), _resident(conv_w.shape), row(MLA_WIDTH), _resident(w_out.shape), row(d),
                  vec, _resident((1, d)), vec, vec, _resident(wr2.shape)],
        out_specs=[row(d), row(d + LANES), pl.BlockSpec((1, N_EXPERTS, tm), lambda bi, i: (bi, 0, i))],
        out_shape=[jax.ShapeDtypeStruct((b, s, d), F32), jax.ShapeDtypeStruct((b, s, d + LANES), F32),
                   jax.ShapeDtypeStruct((b, N_EXPERTS, s), F32)],
        compiler_params=_params(("arbitrary", "arbitrary")),
        name="outproj_router",
    )(u, u, u, bg, conv_w, y_attn, w_out, x, g1, n2, sc2, sh2, wr2)


def _routing_kernel(aff_ref, tri_ref, pos_ref, offs_ref, *, cap, blk, rows_per_expert):
    bi = pl.program_id(0)
    aff = aff_ref[0]
    ne, s = aff.shape
    capf = jnp.float32(cap)

    def bisect(t, prefix):
        cand = prefix | jnp.left_shift(jnp.int32(1), 30 - t)
        cnt = jnp.sum(jnp.where(aff >= lax.bitcast_convert_type(cand, F32), 1.0, 0.0), axis=1, keepdims=True)
        return jnp.where(cnt >= capf, cand, prefix)

    floor_bits = lax.fori_loop(0, 31, bisect, jnp.zeros((ne, 1), I32))
    thr = jnp.min(jnp.where(aff >= lax.bitcast_convert_type(floor_bits, F32), aff, jnp.inf), axis=1, keepdims=True)
    gt = aff > thr
    eq = aff == thr
    need = capf - jnp.sum(jnp.where(gt, 1.0, 0.0), axis=1, keepdims=True)
    tri = tri_ref[...]

    def cumsum_blocks(mask_f32):
        run = jnp.zeros((ne, 1), F32)
        parts, starts = [], []
        for kb in range(s // blk):
            c = _dot(mask_f32[:, kb * blk:(kb + 1) * blk].astype(BF16), tri)
            starts.append(run)
            parts.append(c + run)
            run = run + c[:, blk - 1:blk]
        return jnp.concatenate(parts, axis=1), starts, run

    eqf = jnp.where(eq, 1.0, 0.0)
    eq_incl, _, _ = cumsum_blocks(eqf)
    sel = jnp.where(gt, 1.0, jnp.where(eq & (eq_incl - eqf < need), 1.0, 0.0))
    incl, starts, _ = cumsum_blocks(sel)

    base = (lax.broadcasted_iota(I32, (ne, 1), 0) * rows_per_expert + bi * cap)
    pos_ref[0] = jnp.where(sel > 0.0, (incl - sel).astype(I32) + base, -1)
    lane = lax.broadcasted_iota(I32, (ne, LANES), 1)
    offs = jnp.full((ne, LANES), cap, I32) + base
    for kb, st in enumerate(starts):
        offs = jnp.where(lane == kb, st.astype(I32) + base, offs)
    offs_ref[0] = offs


def _routing(aff_t, *, cap, blk):
    b, ne, s = aff_t.shape
    assert s // blk + 1 <= LANES
    tri = jnp.asarray(np.triu(np.ones((blk, blk), np.float32)), BF16)
    return pl.pallas_call(
        functools.partial(_routing_kernel, cap=cap, blk=blk, rows_per_expert=b * cap),
        grid=(b,),
        in_specs=[pl.BlockSpec((1, ne, s), lambda bi: (bi, 0, 0)), _resident(tri.shape)],
        out_specs=[pl.BlockSpec((1, ne, s), lambda bi: (bi, 0, 0)),
                   pl.BlockSpec((1, ne, LANES), lambda bi: (bi, 0, 0))],
        out_shape=[jax.ShapeDtypeStruct((b, ne, s), I32), jax.ShapeDtypeStruct((b, ne, LANES), I32)],
        compiler_params=_params(("arbitrary",)),
        name="routing",
    )(aff_t, tri)


def _slots_kernel(offs_ref, pos_ref, idx_ref, *, cap, blk, w, rows_per_expert):
    bi = pl.program_id(0)
    ne, s = pos_ref.shape[1], pos_ref.shape[2]
    nb = s // blk
    idx_ref[...] = jnp.zeros(idx_ref.shape, I32)
    sub = lax.broadcasted_iota(I32, (w, blk), 0)
    tok = lax.broadcasted_iota(I32, (w, blk), 1)

    def block(kb, carry):
        tok1 = tok + (kb * blk + bi * s + 1)
        windows, trips = [], []
        for e in range(ne):
            base = e * rows_per_expert + bi * cap
            o = (bi * (nb + 1) + kb) * ne + e
            lo8 = _floor8(offs_ref[o] - base)
            hi = offs_ref[o + ne] - base
            posrow = pos_ref[0, e:e + 1, pl.ds(pl.multiple_of(kb * blk, blk), blk)] - base

            def window(r, c, e=e, lo8=lo8, posrow=posrow):
                start = pl.multiple_of(jnp.minimum(lo8 + r * w, cap - w), 8)
                hit = jnp.where(sub + start == posrow, tok1, 0)
                val = jnp.sum(hit.astype(F32), axis=1, keepdims=True).astype(I32)
                old = idx_ref[0, pl.ds(start, w), e:e + 1]
                idx_ref[0, pl.ds(start, w), e:e + 1] = jnp.where(val > 0, val - 1, old)
                return c

            windows.append(window)
            trips.append(_cdiv_pow2(hi - lo8, w))
        for window in windows:
            window(0, 0)
        for window, n in zip(windows, trips):
            lax.fori_loop(1, n, window, 0)
        return carry

    lax.fori_loop(0, nb, block, 0)


def _slots(offs_flat, pos, *, cap, blk, w):
    b, ne, s = pos.shape
    return pl.pallas_call(
        functools.partial(_slots_kernel, cap=cap, blk=blk, w=w, rows_per_expert=b * cap),
        grid_spec=pltpu.PrefetchScalarGridSpec(
            num_scalar_prefetch=1,
            grid=(b,),
            in_specs=[pl.BlockSpec((1, ne, s), lambda bi, o: (bi, 0, 0))],
            out_specs=pl.BlockSpec((1, cap, ne), lambda bi, o: (bi, 0, 0))),
        out_shape=jax.ShapeDtypeStruct((b, cap, ne), I32),
        compiler_params=_params(("arbitrary",)),
        name="slots",
    )(offs_flat, pos)


def _ffn_kernel(idx_ref, tab_hbm, wg_hbm, wu_hbm, wd_hbm, y_ref, buf_a, buf_b, buf_c, wg_ref, wu_ref, wd_ref,
                sg_ref, su_ref, sd_ref, sem, wsem, *, t, d):
    e = pl.program_id(0)
    j = pl.program_id(1)
    ne, nt = pl.num_programs(0), pl.num_programs(1)
    step = e * nt + j
    nsteps = ne * nt
    rc, fc = sg_ref.shape[0], sd_ref.shape[0]

    def row_copy(base, i, buf, sl):
        return pltpu.make_async_copy(tab_hbm.at[pl.ds(idx_ref[base + i], 1)], buf.at[pl.ds(i, 1)], sem.at[sl])

    def wait_rows(buf, sl):
        pltpu.make_async_copy(tab_hbm.at[pl.ds(0, t)], buf, sem.at[sl]).wait()

    def piece_copies(ex, pc):
        return [pltpu.make_async_copy(wg_hbm.at[ex, pl.ds(pl.multiple_of(pc * rc, rc), rc)], sg_ref, wsem.at[0]),
                pltpu.make_async_copy(wu_hbm.at[ex, pl.ds(pl.multiple_of(pc * rc, rc), rc)], su_ref, wsem.at[1]),
                pltpu.make_async_copy(wd_hbm.at[ex, pl.ds(pl.multiple_of(pc * fc, fc), fc)], sd_ref, wsem.at[2])]

    def cast_piece(st, pc):
        wg_ref[st, pl.ds(pl.multiple_of(pc * rc, rc), rc), :] = sg_ref[...].astype(BF16)
        wu_ref[st, pl.ds(pl.multiple_of(pc * rc, rc), rc), :] = su_ref[...].astype(BF16)
        wd_ref[st, pl.ds(pl.multiple_of(pc * fc, fc), fc), :] = sd_ref[...].astype(BF16)

    @pl.when(step == 0)
    def _():
        def issue(i, carry):
            row_copy(0, i, buf_a, 0).start()
            row_copy(jnp.minimum(1, nsteps - 1) * t, i, buf_b, 1).start()
            return carry
        lax.fori_loop(0, t, issue, 0)

        def load(pc, carry):
            cps = piece_copies(0, pc)
            for cp in cps:
                cp.start()
            for cp in cps:
                cp.wait()
            cast_piece(0, pc)
            return carry
        lax.fori_loop(0, nt - 1, load, 0)
        for cp in piece_copies(0, nt - 1):
            cp.start()

    pe = jnp.where(j > 0, e, e - 1)
    pj = jnp.where(j > 0, j - 1, nt - 1)
    nxt_e = jnp.minimum(e + 1, ne - 1)
    ws = e % 2

    def run(cur, cur_sl, mid, mid_sl, far, far_sl):
        wait_rows(cur, cur_sl)
        for cp in piece_copies(jnp.minimum(pe + 1, ne - 1), pj):
            cp.wait()
        cast_piece((pe + 1) % 2, pj)
        for cp in piece_copies(nxt_e, j):
            cp.start()
        base = jnp.minimum(step + 2, nsteps - 1) * t
        head_rows = FFN_HEAD_ROWS * t // FFN_T
        for i in range(head_rows):
            row_copy(base, i, far, far_sl).start()

        @pl.when(idx_ref[base] < 0)
        def _():
            y_ref[0:16, 0:LANES] = jnp.zeros((16, LANES), BF16)

        for i in range(head_rows, t):
            row_copy(base, i, far, far_sl).start()
        xb = cur[:, 0:d].astype(BF16)
        aff = cur[:, d:d + LANES]
        lane = lax.broadcasted_iota(I32, aff.shape, 1)
        gate_w = jnp.sum(jnp.where(lane == e, aff, 0.0), axis=-1, keepdims=True)
        g = _dot(xb, wg_ref[ws])
        up = _dot(xb, wu_ref[ws])
        hid = ((g / (1.0 + jnp.exp(-g))) * up).astype(BF16)
        y = _dot(hid, wd_ref[ws])
        y_ref[...] = (y * gate_w).astype(BF16)

        @pl.when(step == nsteps - 1)
        def _():
            wait_rows(mid, mid_sl)
            wait_rows(far, far_sl)
            for cp in piece_copies(nxt_e, j):
                cp.wait()

    bufs = (buf_a, buf_b, buf_c)
    for r in range(3):
        @pl.when(step % 3 == r)
        def _(r=r):
            run(bufs[r], r, bufs[(r + 1) % 3], (r + 1) % 3, bufs[(r + 2) % 3], (r + 2) % 3)


def _expert_ffn(idx_flat, tab, w_gate, w_up, w_down, *, t):
    ne, d, ff = w_gate.shape
    rows = idx_flat.shape[0]
    nt = rows // ne // t
    dw = tab.shape[1]
    bf16_rows = 16
    assert d % (nt * bf16_rows) == 0 and ff % (nt * bf16_rows) == 0
    return pl.pallas_call(
        functools.partial(_ffn_kernel, t=t, d=d),
        grid_spec=pltpu.PrefetchScalarGridSpec(
            num_scalar_prefetch=1,
            grid=(ne, nt),
            in_specs=[pl.BlockSpec(memory_space=pl.ANY)] * 4,
            out_specs=pl.BlockSpec((t, d), lambda e, j, idx: (e * nt + j, 0)),
            scratch_shapes=[pltpu.VMEM((t, dw), F32), pltpu.VMEM((t, dw), F32), pltpu.VMEM((t, dw), F32),
                            pltpu.VMEM((2, d, ff), BF16), pltpu.VMEM((2, d, ff), BF16), pltpu.VMEM((2, ff, d), BF16),
                            pltpu.VMEM((d // nt, ff), F32), pltpu.VMEM((d // nt, ff), F32),
                            pltpu.VMEM((ff // nt, d), F32),
                            pltpu.SemaphoreType.DMA((3,)), pltpu.SemaphoreType.DMA((3,))]),
        out_shape=jax.ShapeDtypeStruct((rows, d), BF16),
        compiler_params=_params(("arbitrary", "arbitrary"), vmem=FFN_VMEM_LIMIT),
        name="expert_ffn",
    )(idx_flat, tab, w_gate, w_up, w_down)


def _combine_kernel(offs_ref, pos_ref, y_hbm, x1_ref, g2_ref, fg_ref, o_ref, ybuf, acc_ref, sem, *, tt, w, total_rows):
    ne = pos_ref.shape[1]
    nk = pl.num_programs(1)
    step = pl.program_id(0) * nk + pl.program_id(1)
    nsteps = pl.num_programs(0) * nk
    slot = step % 2

    def bounds(st):
        o0 = (st + lax.div(st, nk)) * ne
        lo8 = [_floor8(offs_ref[o0 + e]) for e in range(ne)]
        hi = [offs_ref[o0 + ne + e] for e in range(ne)]
        return lo8, hi

    def window_copies(lo8, r, sl):
        starts = [pl.multiple_of(jnp.minimum(lo8[e] + r * w, total_rows - w), 8) for e in range(ne)]
        copies = [pltpu.make_async_copy(y_hbm.at[pl.ds(starts[e], w)], ybuf.at[sl, pl.ds(e * w, w)], sem.at[sl, e])
                  for e in range(ne)]
        return starts, copies

    @pl.when(step == 0)
    def _():
        for cp in window_copies(bounds(0)[0], 0, 0)[1]:
            cp.start()

    lo8, hi = bounds(step)
    starts0, copies0 = window_copies(lo8, 0, slot)
    for cp in copies0:
        cp.wait()

    @pl.when(step + 1 < nsteps)
    def _():
        for cp in window_copies(bounds(step + 1)[0], 0, 1 - slot)[1]:
            cp.start()

    pos = pos_ref[0]

    d = x1_ref.shape[2]
    halves = (slice(0, d // 2), slice(d // 2, d))

    def onehot(r, starts):
        sub = lax.broadcasted_iota(I32, (w, tt), 0)
        want = [jnp.where(pos[e:e + 1, :] >= lo8[e] + r * w, pos[e:e + 1, :], -1) for e in range(ne)]
        blocks = [jnp.where(sub + starts[e] == want[e], 1.0, 0.0).astype(BF16) for e in range(ne)]
        return jnp.concatenate(blocks, axis=0)

    def contribution(oh, cols):
        return lax.dot_general(oh, ybuf[slot, :, cols], (((0,), (0,)), ((), ())), preferred_element_type=F32)

    def finish(parts):
        x2 = [x1_ref[0, :, cols] + g2_ref[0, :, cols] * part for cols, part in zip(halves, parts)]
        ms = sum(jnp.sum(v * v, axis=-1, keepdims=True) for v in x2) / d
        scale = lax.rsqrt(ms + EPS)
        for cols, v in zip(halves, x2):
            o_ref[0, :, cols] = (v * scale) * fg_ref[:, cols]

    oh0 = onehot(0, starts0)
    parts = [contribution(oh0, cols) for cols in halves]
    finish(parts)

    rounds = jnp.int32(1)
    for e in range(ne):
        rounds = jnp.maximum(rounds, _cdiv_pow2(hi[e] - lo8[e], w))

    @pl.when(rounds > 1)
    def _():
        for cols, part in zip(halves, parts):
            acc_ref[:, cols] = part

        def extra_round(r, carry):
            starts, copies = window_copies(lo8, r, slot)
            for cp in copies:
                cp.start()
            for cp in copies:
                cp.wait()
            oh = onehot(r, starts)
            for cols in halves:
                acc_ref[:, cols] += contribution(oh, cols)
            return carry

        lax.fori_loop(1, rounds, extra_round, 0)
        finish([acc_ref[:, cols] for cols in halves])


def _combine(offs_flat, pos, y, x1, g2, final_g, *, tt, w):
    b, s, d = x1.shape
    ne = pos.shape[1]
    total_rows = y.shape[0]
    return pl.pallas_call(
        functools.partial(_combine_kernel, tt=tt, w=w, total_rows=total_rows),
        grid_spec=pltpu.PrefetchScalarGridSpec(
            num_scalar_prefetch=1,
            grid=(b, s // tt),
            in_specs=[pl.BlockSpec((1, ne, tt), lambda bi, k, o: (bi, 0, k)),
                      pl.BlockSpec(memory_space=pl.ANY),
                      pl.BlockSpec((1, tt, d), lambda bi, k, o: (bi, k, 0)),
                      pl.BlockSpec((1, 1, d), lambda bi, k, o: (bi, 0, 0)),
                      pl.BlockSpec((1, d), lambda bi, k, o: (0, 0))],
            out_specs=pl.BlockSpec((1, tt, d), lambda bi, k, o: (bi, k, 0)),
            scratch_shapes=[pltpu.VMEM((2, ne * w, d), BF16), pltpu.VMEM((tt, d), F32),
                            pltpu.SemaphoreType.DMA((2, ne))]),
        out_shape=jax.ShapeDtypeStruct((b, s, d), F32),
        compiler_params=_params(("arbitrary", "arbitrary")),
        name="combine_final",
    )(offs_flat, pos, y, x1, g2, final_g)


def _rope_tables(s):
    n_rows = s // GRID_W
    row = np.repeat(np.arange(n_rows), GRID_W).astype(np.float64)
    col = np.tile(np.arange(GRID_W), n_rows).astype(np.float64)
    inv_freq = ROPE_THETA ** (-np.arange(ROPE_PAIRS, dtype=np.float64) / ROPE_PAIRS)
    ang_r = row[:, None] * inv_freq[None, :]
    ang_c = col[:, None] * inv_freq[None, :]
    cos64 = np.concatenate([np.cos(ang_r)] * 2 + [np.cos(ang_c)] * 2, axis=-1).astype(np.float32)
    sin64 = np.concatenate([-np.sin(ang_r), np.sin(ang_r), -np.sin(ang_c), np.sin(ang_c)], axis=-1).astype(np.float32)
    return cos64, sin64


ROPE_SWAP = np.concatenate([np.arange(16, 32), np.arange(0, 16), np.arange(48, 64), np.arange(32, 48)])


def kernel(x, c, ctx, c_ctx, w_mod, b_mod, norm1_g, norm2_g, w_in, conv_w, q_norm_g, w_uq, kv_norm_g, w_ukv, w_out,
           w_router, w_gate, w_up, w_down, final_g):
    b, s, d = x.shape
    lc = ctx.shape[1]
    assert w_in.shape[0] == 1, "single-layer stack"
    assert b <= 7 and s % max(ROW_TM, QKV_TM, ATT_TQ, ATT_TK, CMB_TT) == 0
    cap = EC_FACTOR * s // N_EXPERTS
    assert (b * cap) % FFN_T == 0 and cap % CMB_W == 0

    cvec = jnp.zeros((8, d), F32).at[0:b].set(c).at[b].set(c_ctx)
    mod = _modulation(cvec, w_mod[0], b_mod[0][None, :])
    sh1, sc1, g1, sh2, sc2, g2 = [mod[0:b, None, i * d:(i + 1) * d] for i in range(6)]
    shc1, scc1 = mod[b:b + 1, None, 0:d], mod[b:b + 1, None, d:2 * d]

    n_main = 3 * CONV_WIDTH + Q_LORA + KV_LORA
    w_main = w_in[0, :, 0:n_main].astype(BF16)
    w_kr = w_in[0, :, n_main:]
    w_kr2 = jnp.concatenate([w_kr, w_kr[:, ROPE_SWAP]], axis=-1).astype(BF16)
    w_qn = w_uq[0, :, :, 0:QK_NOPE].reshape(Q_LORA, MLA_HEADS * QK_NOPE).astype(BF16)
    w_qr3 = w_uq[0, :, :, QK_NOPE:]
    w_qr = w_qr3.reshape(Q_LORA, MLA_HEADS * QK_ROPE).astype(BF16)
    w_qrs = w_qr3[:, :, ROPE_SWAP].reshape(Q_LORA, MLA_HEADS * QK_ROPE).astype(BF16)
    w_kn = w_ukv[0, :, :, 0:QK_NOPE].reshape(KV_LORA, MLA_HEADS * QK_NOPE).astype(BF16)
    w_v = w_ukv[0, :, :, QK_NOPE:].reshape(KV_LORA, MLA_HEADS * V_DIM).astype(BF16)
    w_o = w_out[0].astype(BF16)
    wr = jnp.zeros((d, LANES), F32).at[:, 0:N_EXPERTS].set(w_router[0])
    wr_hi = wr.astype(BF16)
    wr2 = jnp.concatenate([wr_hi, (wr - wr_hi.astype(F32)).astype(BF16)], axis=-1)

    cos64, sin64 = _rope_tables(s)
    cos2 = jnp.asarray(np.concatenate([cos64, cos64], axis=-1))
    sin2 = jnp.asarray(np.concatenate([sin64, sin64], axis=-1))
    ktab = jnp.asarray(np.concatenate([cos64, sin64], axis=-1))

    n1 = norm1_g[0][None, :]
    qg, kvg = q_norm_g[0][None, :], kv_norm_g[0][None, :]
    u, bg, cqn, ckvn, kr2 = _inproj(x, sc1, sh1, n1, w_main, w_kr2, qg, kvg, with_conv=True, tm=ROW_TM)
    _, ckvn_c, kr2_c = _inproj(ctx, scc1, shc1, n1, w_main, w_kr2, qg, kvg, with_conv=False, tm=lc)

    q = _qproj(cqn, cos2, sin2, w_qn, w_qr, w_qrs, tm=QKV_TM)
    k, v = _kvproj(ckvn, kr2, ktab, w_kn, w_v, tm=QKV_TM)
    kc, vc = _kvproj(ckvn_c, kr2_c, None, w_kn, w_v, tm=lc)
    y_attn = _attention(q, k, v, kc, vc, tq=ATT_TQ, tk=ATT_TK)

    x1, tab, aff_t = _outproj(u, bg, conv_w[0], y_attn, w_o, x, g1, norm2_g[0][None, :], sc2, sh2, wr2, tm=ROW_TM)

    pos, offs = _routing(aff_t, cap=cap, blk=CMB_TT)
    nk1 = s // CMB_TT + 1
    offs_flat = jnp.swapaxes(offs[:, :, 0:nk1], 1, 2).reshape(-1)
    idx_t = _slots(offs_flat, pos, cap=cap, blk=CMB_TT, w=CMB_W)
    idx_flat = jnp.transpose(idx_t, (2, 0, 1)).reshape(-1)
    y = _expert_ffn(idx_flat, tab.reshape(b * s, d + LANES), w_gate[0], w_up[0], w_down[0], t=FFN_T)
    return _combine(offs_flat, pos, y, x1, g2, final_g[None, :], tt=CMB_TT, w=CMB_W)
```
